```python
import jax, jax.numpy as jnp
from jax import lax
import numpy as np

D_MODEL = 2048
BATCH = 2
SEQ = 4096
DEPTH = 1
DEC_BATCH = 8
DEC_SEQ = 8
PAST_LEN = 16384
PAGE_SIZE = 128

D_GMLP = D_MODEL // 2
GMLP_GROUPS = 8
GMLP_GROUP_DIM = D_GMLP // GMLP_GROUPS
CHUNK = 128
D_ATTN = D_MODEL // 2
HEAD_DIM = 128
N_HEADS = D_ATTN // HEAD_DIM
MOBA_BLOCK = 256
MOBA_TOPK = 3
Q_BLOCK = 32
D_FF = 5632
CONV_W = 3
RMS_EPS = 1e-6
D_IN = 2 * D_GMLP + 3 * D_ATTN + 2 * D_MODEL
SPLITS = [D_GMLP, 2 * D_GMLP, 2 * D_GMLP + D_ATTN, 2 * D_GMLP + 2 * D_ATTN,
          2 * D_GMLP + 3 * D_ATTN, 2 * D_GMLP + 3 * D_ATTN + D_MODEL]

kernel_name = 'hybrid_gmlp_moba_convffn_step'


def rmsnorm(x, g):
    xf = x.astype(jnp.float32)
    xf = xf * lax.rsqrt(jnp.mean(xf * xf, axis=-1, keepdims=True) + RMS_EPS)
    return xf.astype(x.dtype) * g


def alibi_slopes():
    return jnp.exp2(-8.0 * jnp.arange(1, N_HEADS + 1, dtype=jnp.float32) / N_HEADS)


def chunk_gmlp(u, v, w_s, b_s):
    bsz, T, _ = v.shape
    Tp = -(-T // CHUNK) * CHUNK
    vc = jnp.pad(v, ((0, 0), (0, Tp - T), (0, 0))).reshape(bsz, Tp // CHUNK, CHUNK, GMLP_GROUPS, GMLP_GROUP_DIM)
    causal = jnp.tril(jnp.ones((CHUNK, CHUNK), dtype=bool))
    ws = jnp.where(causal[None], w_s, jnp.zeros_like(w_s))
    mixed = jnp.einsum('gij,bcjgd->bcigd', ws, vc) + b_s.T[None, None, :, :, None]
    return u * mixed.reshape(bsz, Tp, D_GMLP)[:, :T]


def key_blocks(k_past, k_new):
    bsz = k_new.shape[0]
    L = k_past.shape[1] + k_new.shape[1]
    nb = max(-(-L // MOBA_BLOCK), MOBA_TOPK)
    pad = jnp.zeros((bsz, nb * MOBA_BLOCK - L, N_HEADS, HEAD_DIM), k_new.dtype)
    return jnp.concatenate([k_past.astype(k_new.dtype), k_new, pad], axis=1).reshape(bsz, nb, MOBA_BLOCK, N_HEADS, HEAD_DIM)


def moba_block_attend(q, q_pos, kb, vb, kmean):
    bsz, nb = kb.shape[:2]
    own = q_pos // MOBA_BLOCK
    gate = jnp.einsum('bthd,bnhd->bhtn', q.astype(jnp.float32), kmean)
    past = jnp.arange(nb)[None, :] < own[:, None]
    gate = jnp.where(past[None, None], gate, -jnp.inf)
    _, sel = lax.top_k(gate, MOBA_TOPK)
    own_b = jnp.broadcast_to(own[None, None, :, None], sel.shape[:3] + (1,)).astype(sel.dtype)
    blocks = jnp.concatenate([sel, own_b], axis=-1)
    b_ix = jnp.arange(bsz)[:, None, None, None]
    h_ix = jnp.arange(N_HEADS)[None, :, None, None]
    kg = kb[b_ix, blocks, :, h_ix]
    vg = vb[b_ix, blocks, :, h_ix]
    k_pos = blocks[..., None] * MOBA_BLOCK + jnp.arange(MOBA_BLOCK)
    dist = (q_pos[None, None, :, None, None] - k_pos).astype(jnp.float32)
    logits = jnp.einsum('bthd,bhtjsd->bhtjs', q, kg).astype(jnp.float32) * (HEAD_DIM ** -0.5)
    logits = logits - alibi_slopes()[None, :, None, None, None] * dist
    j = jnp.arange(MOBA_TOPK + 1)[None, :]
    sel_ok = (j < own[:, None]) | (j == MOBA_TOPK)
    valid = sel_ok[None, None, :, :, None] & (dist >= 0)
    logits = jnp.where(valid, logits, -jnp.inf)
    p = jax.nn.softmax(logits.reshape(logits.shape[:3] + (-1,)), axis=-1).reshape(logits.shape)
    return jnp.einsum('bhtjs,bhtjsd->bthd', p.astype(vg.dtype), vg)


def moba_attention(q, pos0, kb, vb):
    bsz, T = q.shape[:2]
    kmean = kb.astype(jnp.float32).mean(axis=2)
    qb = min(Q_BLOCK, T)
    nq = -(-T // qb)
    q_pos = (pos0 + jnp.minimum(jnp.arange(nq * qb), T - 1)).astype(jnp.int32).reshape(nq, qb)
    qp = jnp.pad(q, ((0, 0), (0, nq * qb - T), (0, 0), (0, 0)))
    qp = qp.reshape(bsz, nq, qb, N_HEADS, HEAD_DIM).transpose(1, 0, 2, 3, 4)
    out = lax.map(lambda a: moba_block_attend(a[0], a[1], kb, vb, kmean), (qp, q_pos))
    return out.transpose(1, 0, 2, 3, 4).reshape(bsz, nq * qb, D_ATTN)[:, :T]


def decoder_layer(x, pos0, k_past, v_past, conv_state, norm_mix, w_in, norm_gmlp_v, w_spatial,
                  b_spatial, w_branch, w_out, norm_ffn, w_up, w_conv, b_conv, w_down):
    bsz, T, _ = x.shape
    xn = rmsnorm(x, norm_mix)
    u, vz, q, k, v, g_a, g_b = jnp.split(xn @ w_in, SPLITS, axis=-1)
    u = jax.nn.gelu(u)
    vz = rmsnorm(jax.nn.gelu(vz), norm_gmlp_v)
    o_a = chunk_gmlp(u, vz, w_spatial, b_spatial)
    open_start = max(0, ((pos0 + T - 1) // CHUNK) * CHUNK - pos0)
    gmlp_state = vz[:, open_start:]
    q = q.reshape(bsz, T, N_HEADS, HEAD_DIM)
    k = k.reshape(bsz, T, N_HEADS, HEAD_DIM)
    v = v.reshape(bsz, T, N_HEADS, HEAD_DIM)
    o_b = moba_attention(q, pos0, key_blocks(k_past, k), key_blocks(v_past, v))
    mixed = jax.nn.sigmoid(g_a) * (o_a @ w_branch[0]) + jax.nn.sigmoid(g_b) * (o_b @ w_branch[1])
    h = x + mixed @ w_out
    hn = rmsnorm(h, norm_ffn)
    a, b = jnp.split(hn @ w_up, [D_FF], axis=-1)
    a_ext = jnp.concatenate([conv_state.astype(a.dtype), a], axis=1)
    conv = b_conv + a_ext[:, CONV_W - 1:] * w_conv[CONV_W - 1]
    for i in range(CONV_W - 1):
        conv = conv + a_ext[:, i:i + T] * w_conv[i]
    h = h + (jax.nn.gelu(conv) * b) @ w_down
    return h, gmlp_state, k, v, a_ext[:, -(CONV_W - 1):]


def setup_inputs(seed: int = 0) -> dict:
    key = jax.random.key(seed)
    ks = jax.random.split(key, 20)
    n_pages = PAST_LEN // PAGE_SIZE
    n_phys = (DEC_BATCH * n_pages * 5) // 4
    f32 = jnp.float32

    def nrm(k, shape, scale):
        return jax.random.normal(k, shape, f32) * scale

    page_table = jax.random.permutation(ks[0], n_phys)[: DEC_BATCH * n_pages].reshape(DEC_BATCH, n_pages).astype(jnp.int32)
    return {
        'x_prompt': nrm(ks[1], (BATCH, SEQ, D_MODEL), 1.0),
        'x_sample': nrm(ks[2], (DEC_BATCH, DEC_SEQ, D_MODEL), 1.0),
        'cache_k': nrm(ks[3], (DEPTH, n_phys, PAGE_SIZE, N_HEADS, HEAD_DIM), 1.0),
        'cache_v': nrm(ks[4], (DEPTH, n_phys, PAGE_SIZE, N_HEADS, HEAD_DIM), 1.0),
        'state_ffn_conv': nrm(ks[5], (DEPTH, DEC_BATCH, CONV_W - 1, D_FF), 1.0),
        'page_table': page_table,
        'norm_mix': 1.0 + nrm(ks[6], (DEPTH, D_MODEL), 0.02),
        'w_in': nrm(ks[7], (DEPTH, D_MODEL, D_IN), D_MODEL ** -0.5),
        'norm_gmlp_v': 1.0 + nrm(ks[8], (DEPTH, D_GMLP), 0.02),
        'w_spatial': nrm(ks[9], (DEPTH, GMLP_GROUPS, CHUNK, CHUNK), CHUNK ** -0.5),
        'b_spatial': 1.0 + nrm(ks[10], (DEPTH, GMLP_GROUPS, CHUNK), 0.1),
        'w_branch': nrm(ks[11], (DEPTH, 2, D_GMLP, D_MODEL), D_GMLP ** -0.5),
        'w_out': nrm(ks[12], (DEPTH, D_MODEL, D_MODEL), D_MODEL ** -0.5),
        'norm_ffn': 1.0 + nrm(ks[13], (DEPTH, D_MODEL), 0.02),
        'w_up': nrm(ks[14], (DEPTH, D_MODEL, 2 * D_FF), D_MODEL ** -0.5),
        'w_conv': nrm(ks[15], (DEPTH, CONV_W, D_FF), CONV_W ** -0.5),
        'b_conv': nrm(ks[16], (DEPTH, D_FF), 0.02),
        'w_down': nrm(ks[17], (DEPTH, D_FF, D_MODEL), D_FF ** -0.5),
        'norm_final': 1.0 + nrm(ks[18], (D_MODEL,), 0.02),
    }


def reference(x_prompt, x_sample, cache_k, cache_v, state_ffn_conv, page_table, norm_mix, w_in,
              norm_gmlp_v, w_spatial, b_spatial, w_branch, w_out, norm_ffn, w_up, w_conv, b_conv,
              w_down, norm_final):
    bp = x_prompt.shape[0]
    bs = x_sample.shape[0]
    past_len = page_table.shape[1] * PAGE_SIZE
    hp, hs = x_prompt, x_sample
    gv_p, gv_s, k_p, v_p, k_s, v_s, c_p, c_s = [], [], [], [], [], [], [], []
    for l in range(DEPTH):
        params = (norm_mix[l], w_in[l], norm_gmlp_v[l], w_spatial[l], b_spatial[l], w_branch[l],
                  w_out[l], norm_ffn[l], w_up[l], w_conv[l], b_conv[l], w_down[l])
        empty_kv = jnp.zeros((bp, 0, N_HEADS, HEAD_DIM), x_prompt.dtype)
        zero_conv = jnp.zeros((bp, CONV_W - 1, D_FF), x_prompt.dtype)
        hp, gv, kk, vv, cc = decoder_layer(hp, 0, empty_kv, empty_kv, zero_conv, *params)
        gv_p.append(gv); k_p.append(kk); v_p.append(vv); c_p.append(cc)
        k_past = cache_k[l][page_table].reshape(bs, past_len, N_HEADS, HEAD_DIM)
        v_past = cache_v[l][page_table].reshape(bs, past_len, N_HEADS, HEAD_DIM)
        hs, gv, kk, vv, cc = decoder_layer(hs, past_len, k_past, v_past, state_ffn_conv[l], *params)
        gv_s.append(gv); k_s.append(kk); v_s.append(vv); c_s.append(cc)
    y_prompt = rmsnorm(hp, norm_final)
    y_sample = rmsnorm(hs, norm_final)
    return (y_prompt, y_sample, jnp.stack(gv_p), jnp.stack(gv_s), jnp.stack(k_p), jnp.stack(v_p),
            jnp.stack(k_s), jnp.stack(v_s), jnp.stack(c_p), jnp.stack(c_s))
```

```python
import functools

import jax
import jax.numpy as jnp
from jax import lax
from jax.experimental import pallas as pl
from jax.experimental.pallas import tpu as pltpu

F32 = jnp.float32
BF16 = jnp.bfloat16

D_MODEL = 2048
D_GMLP = D_MODEL // 2
GMLP_GROUPS = 8
GMLP_GROUP_DIM = D_GMLP // GMLP_GROUPS
CHUNK = 128
D_ATTN = D_MODEL // 2
HEAD_DIM = 128
N_HEADS = D_ATTN // HEAD_DIM
MOBA_BLOCK = 256
MOBA_TOPK = 3
D_FF = 5632
CONV_W = 3
RMS_EPS = 1e-6
PAGE_SIZE = 128
N_SECTIONS = 9

LANES = 128
SUBLANES = 8
VMEM_LIMIT_BYTES = 56 * 1024 * 1024

NEG_BIG = -1e30
FF_TILE = 512
PAGES_PER_STEP = 8

_NT = (((1,), (1,)), ((), ()))


def _dot(a, b):
    return jnp.dot(a, b, preferred_element_type=F32)


def _dot_nt(a, b):
    return lax.dot_general(a, b, _NT, preferred_element_type=F32)


def _rms(x, g):
    return x * lax.rsqrt(jnp.mean(x * x, axis=-1, keepdims=True) + RMS_EPS) * g


def _cparams(n_axes):
    return pltpu.CompilerParams(dimension_semantics=("arbitrary",) * n_axes,
                                vmem_limit_bytes=VMEM_LIMIT_BYTES)


def _top3_mask(gate, lane_f):
    sel = jnp.zeros_like(gate)
    for _ in range(MOBA_TOPK):
        mx = jnp.max(gate, axis=1, keepdims=True)
        first = jnp.min(jnp.where(gate == mx, lane_f, float(LANES)), axis=1, keepdims=True)
        pick = (lane_f == first) & (mx > -jnp.inf)
        sel = jnp.where(pick, 1.0, sel)
        gate = jnp.where(pick, -jnp.inf, gate)
    return sel


def _store_heads(dst_ref, dst_b_ref, acc):
    tm = acc.shape[0]
    dst_b_ref[...] = acc.astype(BF16)
    for h in range(N_HEADS):
        dst_ref[pl.ds(h, tm, stride=N_HEADS), :] = acc[:, h * HEAD_DIM:(h + 1) * HEAD_DIM]


def _inproj_kernel(x_ref, nmix_ref, w_ref, ngv_ref, u_ref, vz_ref, q_ref, k_ref, kb_ref, v_ref,
                   vb_ref, ga_ref, gb_ref, xn_ref):
    n = pl.program_id(1)

    @pl.when(n == 0)
    def _():
        xn_ref[...] = _rms(x_ref[...], nmix_ref[...]).astype(BF16)

    acc = _dot(xn_ref[...], w_ref[...])

    @pl.when(n == 0)
    def _():
        u_ref[...] = jax.nn.gelu(acc).astype(BF16)

    @pl.when(n == 1)
    def _():
        vz_ref[...] = _rms(jax.nn.gelu(acc), ngv_ref[...])

    @pl.when(n == 2)
    def _():
        q_ref[...] = acc.astype(BF16)

    @pl.when(n == 3)
    def _():
        _store_heads(k_ref, kb_ref, acc)

    @pl.when(n == 4)
    def _():
        _store_heads(v_ref, vb_ref, acc)

    @pl.when((n == 5) | (n == 6))
    def _():
        ga_ref[...] = jax.nn.sigmoid(acc).astype(BF16)

    @pl.when(n >= 7)
    def _():
        gb_ref[...] = jax.nn.sigmoid(acc).astype(BF16)


def _inproj(x, norm_mix, w_in_b, norm_gmlp_v, tm):
    m_rows = x.shape[0]
    sec = D_GMLP
    row = lambda m, n: (m, 0)
    in_specs = [
        pl.BlockSpec((tm, D_MODEL), row),
        pl.BlockSpec((1, D_MODEL), lambda m, n: (0, 0)),
        pl.BlockSpec((D_MODEL, sec), lambda m, n: (0, n)),
        pl.BlockSpec((1, sec), lambda m, n: (0, 0)),
    ]
    out_specs = [
        pl.BlockSpec((tm, sec), row),
        pl.BlockSpec((tm, sec), row),
        pl.BlockSpec((tm, sec), row),
        pl.BlockSpec((tm * N_HEADS, HEAD_DIM), row),
        pl.BlockSpec((tm, sec), row),
        pl.BlockSpec((tm * N_HEADS, HEAD_DIM), row),
        pl.BlockSpec((tm, sec), row),
        pl.BlockSpec((tm, sec), lambda m, n: (m, jnp.clip(n - 5, 0, 1))),
        pl.BlockSpec((tm, sec), lambda m, n: (m, jnp.clip(n - 7, 0, 1))),
    ]
    out_shape = [
        jax.ShapeDtypeStruct((m_rows, sec), BF16),
        jax.ShapeDtypeStruct((m_rows, sec), F32),
        jax.ShapeDtypeStruct((m_rows, sec), BF16),
        jax.ShapeDtypeStruct((m_rows * N_HEADS, HEAD_DIM), F32),
        jax.ShapeDtypeStruct((m_rows, sec), BF16),
        jax.ShapeDtypeStruct((m_rows * N_HEADS, HEAD_DIM), F32),
        jax.ShapeDtypeStruct((m_rows, sec), BF16),
        jax.ShapeDtypeStruct((m_rows, D_MODEL), BF16),
        jax.ShapeDtypeStruct((m_rows, D_MODEL), BF16),
    ]
    return pl.pallas_call(
        _inproj_kernel,
        grid=(m_rows // tm, N_SECTIONS),
        in_specs=in_specs,
        out_specs=out_specs,
        out_shape=out_shape,
        scratch_shapes=[pltpu.VMEM((tm, D_MODEL), BF16)],
        compiler_params=_cparams(2),
        name="inproj",
    )(x, norm_mix.reshape(1, D_MODEL), w_in_b, norm_gmlp_v.reshape(1, sec))


def _moba_prompt_kernel(slopes_ref, q_ref, kb_ref, vb_ref, o_ref, km_ref, m_ref, l_ref, acc_ref,
                        *, n_blocks):
    h = pl.program_id(1)
    i = pl.program_id(2)
    blk = MOBA_BLOCK

    @pl.when(i == 0)
    def _():
        km_ref[...] = jnp.zeros_like(km_ref)
        for j in range(n_blocks):
            km_ref[j:j + 1, :] = jnp.sum(kb_ref[j * blk:(j + 1) * blk, :].astype(F32), axis=0,
                                         keepdims=True) * (1.0 / blk)

    q = q_ref[...]
    scale = HEAD_DIM ** -0.5
    slope = slopes_ref[h]

    km = km_ref[...]
    km_hi = km.astype(BF16)
    km_lo = (km - km_hi.astype(F32)).astype(BF16)
    gate = _dot_nt(q, km_hi) + _dot_nt(q, km_lo)
    lane = lax.broadcasted_iota(jnp.int32, (blk, LANES), 1)
    lane_f = lane.astype(F32)
    gate = jnp.where(lane < i, gate, -jnp.inf)
    sel = _top3_mask(gate, lane_f)

    rel = (lax.broadcasted_iota(jnp.int32, (blk, blk), 0)
           - lax.broadcasted_iota(jnp.int32, (blk, blk), 1)).astype(F32)

    start = pl.multiple_of(i * blk, blk)
    s = _dot_nt(q, kb_ref[pl.ds(start, blk), :]) * scale - slope * rel
    s = jnp.where(rel >= 0, s, NEG_BIG)
    m0 = jnp.max(s, axis=1, keepdims=True)
    p = jnp.exp(s - m0)
    m_ref[...] = m0
    l_ref[...] = jnp.sum(p, axis=1, keepdims=True)
    acc_ref[...] = _dot(p.astype(BF16), vb_ref[pl.ds(start, blk), :])

    def body(j, carry):
        off = pl.multiple_of(j * blk, blk)
        dist = rel + ((i - j) * blk).astype(F32)
        sj = _dot_nt(q, kb_ref[pl.ds(off, blk), :]) * scale - slope * dist
        picked = jnp.sum(jnp.where(lane == j, sel, 0.0), axis=1, keepdims=True)
        sj = jnp.where(picked > 0, sj, NEG_BIG)
        m_old = m_ref[...]
        m_new = jnp.maximum(m_old, jnp.max(sj, axis=1, keepdims=True))
        alpha = jnp.exp(m_old - m_new)
        pj = jnp.exp(sj - m_new)
        l_ref[...] = alpha * l_ref[...] + jnp.sum(pj, axis=1, keepdims=True)
        acc_ref[...] = alpha * acc_ref[...] + _dot(pj.astype(BF16), vb_ref[pl.ds(off, blk), :])
        m_ref[...] = m_new
        return carry

    lax.fori_loop(0, i, body, 0)
    o_ref[...] = (acc_ref[...] / l_ref[...]).astype(BF16)


def _moba_prompt(q, k, v, slopes, bsz, seq):
    n_blocks = seq // MOBA_BLOCK
    blk = MOBA_BLOCK
    qmap = lambda b, h, i: (b * n_blocks + i, h)
    kvmap = lambda b, h, i: (b, h)
    return pl.pallas_call(
        functools.partial(_moba_prompt_kernel, n_blocks=n_blocks),
        grid=(bsz, N_HEADS, n_blocks),
        in_specs=[
            pl.BlockSpec(memory_space=pltpu.SMEM),
            pl.BlockSpec((blk, HEAD_DIM), qmap),
            pl.BlockSpec((seq, HEAD_DIM), kvmap),
            pl.BlockSpec((seq, HEAD_DIM), kvmap),
        ],
        out_specs=pl.BlockSpec((blk, HEAD_DIM), qmap),
        out_shape=jax.ShapeDtypeStruct((bsz * seq, D_ATTN), BF16),
        scratch_shapes=[
            pltpu.VMEM((LANES, HEAD_DIM), F32),
            pltpu.VMEM((blk, 1), F32),
            pltpu.VMEM((blk, 1), F32),
            pltpu.VMEM((blk, HEAD_DIM), F32),
        ],
        compiler_params=_cparams(3),
        name="moba_prompt",
    )(slopes, q, k, v)


def _load_page(page_ref):
    return jnp.concatenate(
        [page_ref[pl.ds(h, PAGE_SIZE, stride=N_HEADS), :].astype(BF16) for h in range(N_HEADS)],
        axis=1)


def _page_specs():
    rows = PAGE_SIZE * N_HEADS
    return [
        pl.BlockSpec((None, rows, HEAD_DIM),
                     functools.partial(lambda b, g, pt, pp: (pt[b, g * PAGES_PER_STEP + pp], 0, 0), pp=pp))
        for pp in range(PAGES_PER_STEP)
    ]


def _sample_scores_kernel(pt_ref, qbd_ref, knew_ref, slope_ref, *rest, t_q, past_len):
    del pt_ref
    k_refs = rest[:PAGES_PER_STEP]
    p_ref, pown_ref, l_ref, gate_ref = rest[PAGES_PER_STEP:]
    g = pl.program_id(1)
    n_steps = pl.num_programs(1)
    rows = qbd_ref.shape[0]
    blk = MOBA_BLOCK
    n_past_blocks = past_len // blk
    qbd = qbd_ref[...]
    lane = lax.broadcasted_iota(jnp.int32, (rows, LANES), 1)
    lane_f = lane.astype(F32)

    @pl.when(g == 0)
    def _():
        gate_ref[...] = jnp.zeros_like(gate_ref)

    gates = gate_ref[...]
    prev = None
    for pp in range(PAGES_PER_STEP):
        lg = _dot_nt(qbd, _load_page(k_refs[pp]))
        off = pl.multiple_of((g * PAGES_PER_STEP + pp) * PAGE_SIZE, PAGE_SIZE)
        p_ref[:, pl.ds(off, PAGE_SIZE)] = lg
        if pp % 2 == 0:
            prev = lg
        else:
            bsum = jnp.sum(prev + lg, axis=1, keepdims=True)
            gates = jnp.where(lane == g * (PAGES_PER_STEP // 2) + pp // 2, bsum, gates)
    gate_ref[...] = gates

    @pl.when(g == n_steps - 1)
    def _():
        scale = HEAD_DIM ** -0.5
        slope = slope_ref[...]
        t_row = (lax.broadcasted_iota(jnp.int32, (rows, 1), 0) % t_q).astype(F32)
        sel = _top3_mask(jnp.where(lane < n_past_blocks, gate_ref[...], -jnp.inf), lane_f)

        dist_own = t_row - lane_f
        s_own = _dot_nt(qbd, knew_ref[...]) * scale - slope * dist_own
        s_own = jnp.where(dist_own >= 0, s_own, NEG_BIG)
        m0 = jnp.max(s_own, axis=1, keepdims=True)

        kpos = lax.broadcasted_iota(jnp.int32, (rows, blk), 1).astype(F32)

        def masked_logits(j, carry):
            off = pl.multiple_of(j * blk, blk)
            dist = (past_len + t_row) - ((j * blk).astype(F32) + kpos)
            sj = p_ref[:, pl.ds(off, blk)] * scale - slope * dist
            picked = jnp.sum(jnp.where(lane == j, sel, 0.0), axis=1, keepdims=True)
            sj = jnp.where(picked > 0, sj, NEG_BIG)
            p_ref[:, pl.ds(off, blk)] = sj
            return jnp.maximum(carry, jnp.max(sj, axis=1, keepdims=True))

        m = lax.fori_loop(0, n_past_blocks, masked_logits, m0)

        p_own = jnp.exp(s_own - m)
        pown_ref[...] = p_own

        def exp_sum(j, carry):
            off = pl.multiple_of(j * blk, blk)
            pj = jnp.exp(p_ref[:, pl.ds(off, blk)] - m)
            p_ref[:, pl.ds(off, blk)] = pj
            return carry + jnp.sum(pj, axis=1, keepdims=True)

        l = lax.fori_loop(0, n_past_blocks, exp_sum, jnp.sum(p_own, axis=1, keepdims=True))
        l_ref[...] = jnp.broadcast_to(l, l_ref.shape)


def _sample_scores(page_table, qbd, knew_pad, slope_rows, cache_k_pages, t_q, past_len):
    bsz, rows, _ = qbd.shape
    n_pages = page_table.shape[1]
    n_steps = n_pages // PAGES_PER_STEP
    grid_spec = pltpu.PrefetchScalarGridSpec(
        num_scalar_prefetch=1,
        grid=(bsz, n_steps),
        in_specs=[
            pl.BlockSpec((None, rows, D_ATTN), lambda b, g, pt: (b, 0, 0)),
            pl.BlockSpec((None, LANES, D_ATTN), lambda b, g, pt: (b, 0, 0)),
            pl.BlockSpec((rows, 1), lambda b, g, pt: (0, 0)),
        ] + _page_specs(),
        out_specs=[
            pl.BlockSpec((None, rows, past_len), lambda b, g, pt: (b, 0, 0)),
            pl.BlockSpec((None, rows, LANES), lambda b, g, pt: (b, 0, 0)),
            pl.BlockSpec((None, rows, LANES), lambda b, g, pt: (b, 0, 0)),
        ],
        scratch_shapes=[pltpu.VMEM((rows, LANES), F32)],
    )
    return pl.pallas_call(
        functools.partial(_sample_scores_kernel, t_q=t_q, past_len=past_len),
        grid_spec=grid_spec,
        out_shape=[
            jax.ShapeDtypeStruct((bsz, rows, past_len), F32),
            jax.ShapeDtypeStruct((bsz, rows, LANES), F32),
            jax.ShapeDtypeStruct((bsz, rows, LANES), F32),
        ],
        compiler_params=_cparams(2),
        name="sample_scores",
    )(page_table, qbd, knew_pad, slope_rows, *([cache_k_pages] * PAGES_PER_STEP))


def _sample_pv_kernel(pt_ref, p_ref, pown_ref, l_ref, vnew_ref, *rest, t_q):
    del pt_ref
    v_refs = rest[:PAGES_PER_STEP]
    o_ref, acc_ref = rest[PAGES_PER_STEP:]
    g = pl.program_id(1)
    n_steps = pl.num_programs(1)

    @pl.when(g == 0)
    def _():
        acc_ref[...] = _dot(pown_ref[...].astype(BF16), vnew_ref[...])

    acc = acc_ref[...]
    for pp in range(PAGES_PER_STEP):
        acc = acc + _dot(p_ref[:, pp * PAGE_SIZE:(pp + 1) * PAGE_SIZE].astype(BF16),
                         _load_page(v_refs[pp]))
    acc_ref[...] = acc

    @pl.when(g == n_steps - 1)
    def _():
        for h in range(N_HEADS):
            rs = slice(h * t_q, (h + 1) * t_q)
            cs = slice(h * HEAD_DIM, (h + 1) * HEAD_DIM)
            o_ref[:, cs] = (acc_ref[rs, cs] / l_ref[rs, 0:1]).astype(BF16)


def _sample_pv(page_table, p, pown, l, vnew_pad, cache_v_pages, t_q):
    bsz, rows, _ = p.shape
    n_pages = page_table.shape[1]
    n_steps = n_pages // PAGES_PER_STEP
    step_keys = PAGES_PER_STEP * PAGE_SIZE
    grid_spec = pltpu.PrefetchScalarGridSpec(
        num_scalar_prefetch=1,
        grid=(bsz, n_steps),
        in_specs=[
            pl.BlockSpec((None, rows, step_keys), lambda b, g, pt: (b, 0, g)),
            pl.BlockSpec((None, rows, LANES), lambda b, g, pt: (b, 0, 0)),
            pl.BlockSpec((None, rows, LANES), lambda b, g, pt: (b, 0, 0)),
            pl.BlockSpec((None, LANES, D_ATTN), lambda b, g, pt: (b, 0, 0)),
        ] + _page_specs(),
        out_specs=pl.BlockSpec((None, t_q, D_ATTN), lambda b, g, pt: (b, 0, 0)),
        scratch_shapes=[pltpu.VMEM((rows, D_ATTN), F32)],
    )
    return pl.pallas_call(
        functools.partial(_sample_pv_kernel, t_q=t_q),
        grid_spec=grid_spec,
        out_shape=jax.ShapeDtypeStruct((bsz, t_q, D_ATTN), BF16),
        compiler_params=_cparams(2),
        name="sample_pv",
    )(page_table, p, pown, l, vnew_pad, *([cache_v_pages] * PAGES_PER_STEP))


def _mix_kernel(u_ref, vz_ref, ob_ref, ga_ref, gb_ref, x_ref, ws_ref, bst_ref, wbr_ref, wout_ref,
                nffn_ref, h_ref, hn_ref, oa_ref):
    tm = u_ref.shape[0]
    chunk = ws_ref.shape[1]
    causal = (lax.broadcasted_iota(jnp.int32, (chunk, chunk), 0)
              >= lax.broadcasted_iota(jnp.int32, (chunk, chunk), 1))
    for g in range(GMLP_GROUPS):
        cs = slice(g * GMLP_GROUP_DIM, (g + 1) * GMLP_GROUP_DIM)
        w_g = jnp.where(causal, ws_ref[g], 0.0).astype(BF16)
        b_g = bst_ref[:, g:g + 1]
        for c in range(tm // chunk):
            rs = slice(c * chunk, (c + 1) * chunk)
            mixed = _dot(w_g, vz_ref[rs, cs].astype(BF16)) + b_g
            oa_ref[rs, cs] = (u_ref[rs, cs].astype(F32) * mixed).astype(BF16)
    merged = (ga_ref[...].astype(F32) * _dot(oa_ref[...], wbr_ref[0])
              + gb_ref[...].astype(F32) * _dot(ob_ref[...], wbr_ref[1]))
    h = x_ref[...] + _dot(merged.astype(BF16), wout_ref[...])
    h_ref[...] = h
    hn_ref[...] = _rms(h, nffn_ref[...]).astype(BF16)


def _mix(u, vz, ob, ga, gb, x, ws_chunk, bs_t, w_br_b, w_out_b, norm_ffn, tm):
    m_rows = x.shape[0]
    chunk = ws_chunk.shape[1]
    row = lambda m: (m, 0)
    const2 = lambda m: (0, 0)
    const3 = lambda m: (0, 0, 0)
    once = pl.Buffered(1)
    return pl.pallas_call(
        _mix_kernel,
        grid=(m_rows // tm,),
        in_specs=[
            pl.BlockSpec((tm, D_GMLP), row),
            pl.BlockSpec((tm, D_GMLP), row),
            pl.BlockSpec((tm, D_ATTN), row),
            pl.BlockSpec((tm, D_MODEL), row),
            pl.BlockSpec((tm, D_MODEL), row),
            pl.BlockSpec((tm, D_MODEL), row),
            pl.BlockSpec((GMLP_GROUPS, chunk, chunk), const3, pipeline_mode=once),
            pl.BlockSpec((chunk, GMLP_GROUPS), const2, pipeline_mode=once),
            pl.BlockSpec((2, D_GMLP, D_MODEL), const3, pipeline_mode=once),
            pl.BlockSpec((D_MODEL, D_MODEL), const2, pipeline_mode=once),
            pl.BlockSpec((1, D_MODEL), const2, pipeline_mode=once),
        ],
        out_specs=[pl.BlockSpec((tm, D_MODEL), row), pl.BlockSpec((tm, D_MODEL), row)],
        out_shape=[jax.ShapeDtypeStruct((m_rows, D_MODEL), F32),
                   jax.ShapeDtypeStruct((m_rows, D_MODEL), BF16)],
        scratch_shapes=[pltpu.VMEM((tm, D_GMLP), BF16)],
        compiler_params=_cparams(1),
        name="mix",
    )(u, vz, ob, ga, gb, x, ws_chunk, bs_t, w_br_b, w_out_b, norm_ffn.reshape(1, D_MODEL))


def _ffn_kernel(hn_ref, wa_ref, wb_ref, wc_ref, bc_ref, wd_ref, h_ref, nfin_ref, init_ref,
                y_ref, tail_ref, acc_ref, halo_ref, *, seq_len, tiles_per_seq):
    m = pl.program_id(0)
    n = pl.program_id(1)
    tm = hn_ref.shape[0]
    tf = wa_ref.shape[1]
    hn = hn_ref[...]
    a = _dot(hn, wa_ref[...])
    b = _dot(hn, wb_ref[...])
    row = lax.broadcasted_iota(jnp.int32, (tm, tf), 0)
    r1 = pltpu.roll(a, 1, 0)
    r2 = pltpu.roll(a, 2, 0)
    if tiles_per_seq is None:
        pos = row % seq_len
        a1 = jnp.where(pos >= 1, r1, init_ref[0])
        a2 = jnp.where(pos >= 2, r2, init_ref[1])
        tail_ref[...] = a
    else:
        prev = jnp.where(m % tiles_per_seq == 0, init_ref[0], halo_ref[n])
        p1 = prev[SUBLANES - 1:SUBLANES, :]
        p2 = prev[SUBLANES - 2:SUBLANES - 1, :]
        a1 = jnp.where(row == 0, p1, r1)
        a2 = jnp.where(row == 0, p2, jnp.where(row == 1, p1, r2))
        last = a[tm - SUBLANES:, :]
        halo_ref[n] = last
        tail_ref[...] = last
    wc = wc_ref[...]
    conv = bc_ref[...] + a * wc[2:3, :] + a2 * wc[0:1, :] + a1 * wc[1:2, :]
    act = (jax.nn.gelu(conv) * b).astype(BF16)
    part = _dot(act, wd_ref[...])

    @pl.when(n == 0)
    def _():
        acc_ref[...] = part

    @pl.when(n > 0)
    def _():
        acc_ref[...] += part

    @pl.when(n == pl.num_programs(1) - 1)
    def _():
        y_ref[...] = _rms(h_ref[...] + acc_ref[...], nfin_ref[...])


def _ffn(hn, h, w_up_b, w_conv, b_conv, w_down_b, norm_final, init, tm, seq_len):
    m_rows = hn.shape[0]
    tf = FF_TILE
    n_ff = D_FF // tf
    if seq_len >= tm:
        tiles_per_seq = seq_len // tm
        init_spec = pl.BlockSpec((1, SUBLANES, tf), lambda m, n: (0, m // tiles_per_seq, n))
        tail_rows = (m_rows // tm) * SUBLANES
        tail_spec = pl.BlockSpec((SUBLANES, tf), lambda m, n: (m, n))
    else:
        tiles_per_seq = None
        init_spec = pl.BlockSpec((2, tm, tf), lambda m, n: (0, m, n))
        tail_rows = m_rows
        tail_spec = pl.BlockSpec((tm, tf), lambda m, n: (m, n))
    row = lambda m, n: (m, 0)
    return pl.pallas_call(
        functools.partial(_ffn_kernel, seq_len=seq_len, tiles_per_seq=tiles_per_seq),
        grid=(m_rows // tm, n_ff),
        in_specs=[
            pl.BlockSpec((tm, D_MODEL), row),
            pl.BlockSpec((D_MODEL, tf), lambda m, n: (0, n)),
            pl.BlockSpec((D_MODEL, tf), lambda m, n: (0, n_ff + n)),
            pl.BlockSpec((CONV_W, tf), lambda m, n: (0, n)),
            pl.BlockSpec((1, tf), lambda m, n: (0, n)),
            pl.BlockSpec((tf, D_MODEL), lambda m, n: (n, 0)),
            pl.BlockSpec((tm, D_MODEL), row),
            pl.BlockSpec((1, D_MODEL), lambda m, n: (0, 0)),
            init_spec,
        ],
        out_specs=[pl.BlockSpec((tm, D_MODEL), row), tail_spec],
        out_shape=[jax.ShapeDtypeStruct((m_rows, D_MODEL), F32),
                   jax.ShapeDtypeStruct((tail_rows, D_FF), F32)],
        scratch_shapes=[pltpu.VMEM((tm, D_MODEL), F32),
                        pltpu.VMEM((n_ff, SUBLANES, tf), F32)],
        compiler_params=_cparams(2),
        name="ffn",
    )(hn, w_up_b, w_up_b, w_conv, b_conv.reshape(1, D_FF), w_down_b, h,
      norm_final.reshape(1, D_MODEL), init)


def _alibi_slopes():
    return jnp.exp2(-8.0 * jnp.arange(1, N_HEADS + 1, dtype=F32) / N_HEADS)


def kernel(x_prompt, x_sample, cache_k, cache_v, state_ffn_conv, page_table, norm_mix, w_in,
           norm_gmlp_v, w_spatial, b_spatial, w_branch, w_out, norm_ffn, w_up, w_conv, b_conv,
           w_down, norm_final):
    assert w_in.shape[0] == 1, "single layer"
    bp, seq, _ = x_prompt.shape
    bs, t_q, _ = x_sample.shape
    n_pages = page_table.shape[1]
    past_len = n_pages * PAGE_SIZE
    assert seq % MOBA_BLOCK == 0 and past_len % MOBA_BLOCK == 0 and past_len % CHUNK == 0
    assert t_q <= SUBLANES and past_len // MOBA_BLOCK <= LANES and seq // MOBA_BLOCK <= LANES

    slopes = _alibi_slopes()
    w_in_b = w_in[0].astype(BF16)
    w_br_b = w_branch[0].astype(BF16)
    w_out_b = w_out[0].astype(BF16)
    w_up_b = w_up[0].astype(BF16)
    w_down_b = w_down[0].astype(BF16)
    ws, bsp = w_spatial[0], b_spatial[0]

    tm_p = 512
    xp = x_prompt.reshape(bp * seq, D_MODEL)
    u, vz, q, k, kb, v, vb, ga, gb = _inproj(xp, norm_mix[0], w_in_b, norm_gmlp_v[0], tm_p)
    ob = _moba_prompt(q, kb, vb, slopes, bp, seq)
    h, hn = _mix(u, vz, ob, ga, gb, xp, ws, bsp.T, w_br_b, w_out_b, norm_ffn[0], 256)
    zero_state = jnp.zeros((1, bp * SUBLANES, D_FF), F32)
    yp, tail_p = _ffn(hn, h, w_up_b, w_conv[0], b_conv[0], w_down_b, norm_final, zero_state,
                      tm_p, seq)
    y_prompt = yp.reshape(bp, seq, D_MODEL)
    gv_p = vz.reshape(bp, seq, D_GMLP)[:, seq - CHUNK:][None]
    k_p = k.reshape(1, bp, seq, N_HEADS, HEAD_DIM)
    v_p = v.reshape(1, bp, seq, N_HEADS, HEAD_DIM)
    c_p = tail_p.reshape(bp, seq // tm_p, SUBLANES, D_FF)[:, -1, SUBLANES - (CONV_W - 1):][None]

    m_s = bs * t_q
    xs = x_sample.reshape(m_s, D_MODEL)
    us, vzs, qs, ks, kbs, vs, vbs, gas, gbs = _inproj(xs, norm_mix[0], w_in_b, norm_gmlp_v[0], m_s)
    q4 = qs.reshape(bs, t_q, N_HEADS, HEAD_DIM)
    eye = jnp.eye(N_HEADS, dtype=BF16)
    qbd = (q4.transpose(0, 2, 1, 3)[:, :, :, None, :] * eye[None, :, None, :, None]
           ).reshape(bs, N_HEADS * t_q, D_ATTN)
    pad_rows = ((0, 0), (0, LANES - t_q), (0, 0))
    knew = jnp.pad(kbs.reshape(bs, t_q, D_ATTN), pad_rows)
    vnew = jnp.pad(vbs.reshape(bs, t_q, D_ATTN), pad_rows)
    slope_rows = jnp.repeat(slopes, t_q).reshape(N_HEADS * t_q, 1)
    ck = cache_k[0].reshape(-1, PAGE_SIZE * N_HEADS, HEAD_DIM)
    cv = cache_v[0].reshape(-1, PAGE_SIZE * N_HEADS, HEAD_DIM)
    p, pown, l = _sample_scores(page_table, qbd, knew, slope_rows, ck, t_q, past_len)
    obs = _sample_pv(page_table, p, pown, l, vnew, cv, t_q).reshape(m_s, D_ATTN)
    ws_s = (jnp.eye(bs, dtype=F32)[None, :, None, :, None]
            * ws[:, None, :t_q, None, :t_q]).reshape(GMLP_GROUPS, m_s, m_s)
    bs_t_s = jnp.tile(bsp[:, :t_q], (1, bs)).T
    hs, hns = _mix(us, vzs, obs, gas, gbs, xs, ws_s, bs_t_s, w_br_b, w_out_b, norm_ffn[0], m_s)
    st = state_ffn_conv[0]
    zero_rows = jnp.zeros((bs, t_q - 1, D_FF), F32)
    init1 = jnp.concatenate([st[:, 1:2], zero_rows], axis=1)
    init2 = jnp.concatenate([st[:, 0:1], st[:, 1:2], zero_rows[:, 1:]], axis=1)
    init = jnp.stack([init1.reshape(m_s, D_FF), init2.reshape(m_s, D_FF)])
    ys, tail_s = _ffn(hns, hs, w_up_b, w_conv[0], b_conv[0], w_down_b, norm_final, init, m_s, t_q)
    y_sample = ys.reshape(bs, t_q, D_MODEL)
    gv_s = vzs.reshape(1, bs, t_q, D_GMLP)
    k_s = ks.reshape(1, bs, t_q, N_HEADS, HEAD_DIM)
    v_s = vs.reshape(1, bs, t_q, N_HEADS, HEAD_DIM)
    c_s = tail_s.reshape(bs, t_q, D_FF)[:, t_q - (CONV_W - 1):][None]

    return (y_prompt, y_sample, gv_p, gv_s, k_p, v_p, k_s, v_s, c_p, c_s)
```

```python
import functools

import jax
import jax.numpy as jnp
from jax import lax
from jax.experimental import pallas as pl
from jax.experimental.pallas import tpu as pltpu

F32 = jnp.float32
BF16 = jnp.bfloat16

D_MODEL = 2048
D_GMLP = D_MODEL // 2
GMLP_GROUPS = 8
GMLP_GROUP_DIM = D_GMLP // GMLP_GROUPS
CHUNK = 128
D_ATTN = D_MODEL // 2
HEAD_DIM = 128
N_HEADS = D_ATTN // HEAD_DIM
MOBA_BLOCK = 256
MOBA_TOPK = 3
D_FF = 5632
CONV_W = 3
RMS_EPS = 1e-6
PAGE_SIZE = 128
N_SECTIONS = 9

LANES = 128
SUBLANES = 8
VMEM_LIMIT_BYTES = 56 * 1024 * 1024

NEG_BIG = -1e30
LOG2E = 1.4426950408889634
MOBA_HEADS_PER_STEP = 4
MOBA_ONES_ROWS = 16
FF_TILE = 512
PAGES_PER_STEP = 8

_NT = (((1,), (1,)), ((), ()))


def _dot(a, b):
    return jnp.dot(a, b, preferred_element_type=F32)


def _dot_nt(a, b):
    return lax.dot_general(a, b, _NT, preferred_element_type=F32)


def _rms(x, g):
    return x * lax.rsqrt(jnp.mean(x * x, axis=-1, keepdims=True) + RMS_EPS) * g


def _cparams(n_axes):
    return pltpu.CompilerParams(dimension_semantics=("arbitrary",) * n_axes,
                                vmem_limit_bytes=VMEM_LIMIT_BYTES)


def _top3_mask(gate, idx_f, axis=1):
    sel = jnp.zeros_like(gate)
    for _ in range(MOBA_TOPK):
        mx = jnp.max(gate, axis=axis, keepdims=True)
        first = jnp.min(jnp.where(gate == mx, idx_f, float(LANES)), axis=axis, keepdims=True)
        pick = (idx_f == first) & (mx > -jnp.inf)
        sel = jnp.where(pick, 1.0, sel)
        gate = jnp.where(pick, -jnp.inf, gate)
    return sel


def _store_heads(dst_ref, dst_b_ref, acc):
    tm = acc.shape[0]
    dst_b_ref[...] = acc.astype(BF16)
    for h in range(N_HEADS):
        dst_ref[pl.ds(h, tm, stride=N_HEADS), :] = acc[:, h * HEAD_DIM:(h + 1) * HEAD_DIM]


def _inproj_kernel(x_ref, nmix_ref, w_ref, ngv_ref, u_ref, vz_ref, q_ref, k_ref, kb_ref, v_ref,
                   vb_ref, ga_ref, gb_ref, xn_ref):
    n = pl.program_id(1)

    @pl.when(n == 0)
    def _():
        xn_ref[...] = _rms(x_ref[...], nmix_ref[...]).astype(BF16)

    acc = _dot(xn_ref[...], w_ref[...])

    @pl.when(n == 0)
    def _():
        u_ref[...] = jax.nn.gelu(acc).astype(BF16)

    @pl.when(n == 1)
    def _():
        vz_ref[...] = _rms(jax.nn.gelu(acc), ngv_ref[...])

    @pl.when(n == 2)
    def _():
        q_ref[...] = acc.astype(BF16)

    @pl.when(n == 3)
    def _():
        _store_heads(k_ref, kb_ref, acc)

    @pl.when(n == 4)
    def _():
        _store_heads(v_ref, vb_ref, acc)

    @pl.when((n == 5) | (n == 6))
    def _():
        ga_ref[...] = jax.nn.sigmoid(acc).astype(BF16)

    @pl.when(n >= 7)
    def _():
        gb_ref[...] = jax.nn.sigmoid(acc).astype(BF16)


def _inproj(x, norm_mix, w_in_b, norm_gmlp_v, tm):
    m_rows = x.shape[0]
    sec = D_GMLP
    row = lambda m, n: (m, 0)
    in_specs = [
        pl.BlockSpec((tm, D_MODEL), row),
        pl.BlockSpec((1, D_MODEL), lambda m, n: (0, 0)),
        pl.BlockSpec((D_MODEL, sec), lambda m, n: (0, n)),
        pl.BlockSpec((1, sec), lambda m, n: (0, 0)),
    ]
    out_specs = [
        pl.BlockSpec((tm, sec), row),
        pl.BlockSpec((tm, sec), row),
        pl.BlockSpec((tm, sec), row),
        pl.BlockSpec((tm * N_HEADS, HEAD_DIM), row),
        pl.BlockSpec((tm, sec), row),
        pl.BlockSpec((tm * N_HEADS, HEAD_DIM), row),
        pl.BlockSpec((tm, sec), row),
        pl.BlockSpec((tm, sec), lambda m, n: (m, jnp.clip(n - 5, 0, 1))),
        pl.BlockSpec((tm, sec), lambda m, n: (m, jnp.clip(n - 7, 0, 1))),
    ]
    out_shape = [
        jax.ShapeDtypeStruct((m_rows, sec), BF16),
        jax.ShapeDtypeStruct((m_rows, sec), F32),
        jax.ShapeDtypeStruct((m_rows, sec), BF16),
        jax.ShapeDtypeStruct((m_rows * N_HEADS, HEAD_DIM), F32),
        jax.ShapeDtypeStruct((m_rows, sec), BF16),
        jax.ShapeDtypeStruct((m_rows * N_HEADS, HEAD_DIM), F32),
        jax.ShapeDtypeStruct((m_rows, sec), BF16),
        jax.ShapeDtypeStruct((m_rows, D_MODEL), BF16),
        jax.ShapeDtypeStruct((m_rows, D_MODEL), BF16),
    ]
    return pl.pallas_call(
        _inproj_kernel,
        grid=(m_rows // tm, N_SECTIONS),
        in_specs=in_specs,
        out_specs=out_specs,
        out_shape=out_shape,
        scratch_shapes=[pltpu.VMEM((tm, D_MODEL), BF16)],
        compiler_params=_cparams(2),
        name="inproj",
    )(x, norm_mix.reshape(1, D_MODEL), w_in_b, norm_gmlp_v.reshape(1, sec))


def _moba_prompt_kernel(slopes_ref, q_ref, kb_ref, vb_ref, o_ref, km_ref, vt_ref, bias_ref, sel_ref,
                        m_ref, t_ref, acc_ref, *, n_blocks):
    hg = pl.program_id(1)
    i = pl.program_id(2)
    blk = MOBA_BLOCK
    c1 = (HEAD_DIM ** -0.5) * LOG2E
    heads = [(hh, slice(hh * HEAD_DIM, (hh + 1) * HEAD_DIM)) for hh in range(MOBA_HEADS_PER_STEP)]

    def slope2(hh):
        return slopes_ref[hg * MOBA_HEADS_PER_STEP + hh] * LOG2E

    @pl.when(i == 0)
    def _():
        key_f = lax.broadcasted_iota(jnp.int32, (blk, blk), 0).astype(F32)
        for hh, cs in heads:
            bias_ref[hh] = slope2(hh) * key_f
            vt_ref[hh, HEAD_DIM:, :] = jnp.ones((MOBA_ONES_ROWS, n_blocks * blk), BF16)
            for j in range(n_blocks):
                rs = slice(j * blk, (j + 1) * blk)
                km_ref[hh, j:j + 1, :] = jnp.sum(kb_ref[rs, cs].astype(F32), axis=0,
                                                 keepdims=True) * (1.0 / blk)
                vt_ref[hh, :HEAD_DIM, rs] = vb_ref[rs, cs].astype(F32).T.astype(BF16)

    blk_i = lax.broadcasted_iota(jnp.int32, (n_blocks, blk), 0)
    causal = (lax.broadcasted_iota(jnp.int32, (blk, blk), 1)
              >= lax.broadcasted_iota(jnp.int32, (blk, blk), 0))
    start = pl.multiple_of(i * blk, blk)

    for hh, cs in heads:
        q = q_ref[:, cs]
        km = km_ref[hh]
        km_hi = km.astype(BF16)
        km_lo = (km - km_hi.astype(F32)).astype(BF16)
        gate = _dot_nt(km_hi, q) + _dot_nt(km_lo, q)
        gate = jnp.where(blk_i < i, gate, -jnp.inf)
        sel_ref[hh] = _top3_mask(gate, blk_i.astype(F32), axis=0)
        t = _dot_nt(kb_ref[pl.ds(start, blk), cs], q) * c1 + bias_ref[hh]
        t = jnp.where(causal, t, NEG_BIG)
        t_ref[hh, i] = t
        m_ref[hh] = jnp.max(t, axis=0, keepdims=True)

    def shift(hh, j):
        return slope2(hh) * ((i - j) * blk).astype(F32)

    def pass1(j, carry):
        off = pl.multiple_of(j * blk, blk)
        for hh, cs in heads:
            t = _dot_nt(kb_ref[pl.ds(off, blk), cs], q_ref[:, cs]) * c1 + bias_ref[hh]
            t_ref[hh, j] = t
            picked = sel_ref[hh, pl.ds(j, 1), :] > 0.0
            m_old = m_ref[hh]
            m_blk = jnp.max(t, axis=0, keepdims=True) - shift(hh, j)
            m_ref[hh] = jnp.where(picked, jnp.maximum(m_old, m_blk), m_old)
        return carry

    lax.fori_loop(0, i, pass1, 0)

    for hh, cs in heads:
        p = jnp.exp2(t_ref[hh, i] - m_ref[hh])
        acc_ref[hh] = _dot(vt_ref[hh, :, pl.ds(start, blk)], p.astype(BF16))

    def pass2(j, carry):
        off = pl.multiple_of(j * blk, blk)
        for hh, cs in heads:
            picked = sel_ref[hh, pl.ds(j, 1), :] > 0.0
            sub = jnp.where(picked, m_ref[hh] + shift(hh, j), -NEG_BIG)
            p = jnp.exp2(t_ref[hh, j] - sub)
            acc_ref[hh] += _dot(vt_ref[hh, :, pl.ds(off, blk)], p.astype(BF16))
        return carry

    lax.fori_loop(0, i, pass2, 0)
    for hh, cs in heads:
        acc = acc_ref[hh]
        o_ref[:, cs] = (acc[:HEAD_DIM] / acc[HEAD_DIM:HEAD_DIM + 1]).T.astype(BF16)


def _moba_prompt(q, kb, vb, slopes, bsz, seq):
    n_blocks = seq // MOBA_BLOCK
    blk = MOBA_BLOCK
    hps = MOBA_HEADS_PER_STEP
    width = hps * HEAD_DIM
    qmap = lambda b, hg, i: (b * n_blocks + i, hg)
    kvmap = lambda b, hg, i: (b, hg)
    return pl.pallas_call(
        functools.partial(_moba_prompt_kernel, n_blocks=n_blocks),
        grid=(bsz, N_HEADS // hps, n_blocks),
        in_specs=[
            pl.BlockSpec(memory_space=pltpu.SMEM),
            pl.BlockSpec((blk, width), qmap),
            pl.BlockSpec((seq, width), kvmap),
            pl.BlockSpec((seq, width), kvmap),
        ],
        out_specs=pl.BlockSpec((blk, width), qmap),
        out_shape=jax.ShapeDtypeStruct((bsz * seq, D_ATTN), BF16),
        scratch_shapes=[
            pltpu.VMEM((hps, n_blocks, HEAD_DIM), F32),
            pltpu.VMEM((hps, HEAD_DIM + MOBA_ONES_ROWS, seq), BF16),
            pltpu.VMEM((hps, blk, blk), F32),
            pltpu.VMEM((hps, n_blocks, blk), F32),
            pltpu.VMEM((hps, 1, blk), F32),
            pltpu.VMEM((hps, n_blocks, blk, blk), F32),
            pltpu.VMEM((hps, HEAD_DIM + MOBA_ONES_ROWS, blk), F32),
        ],
        compiler_params=_cparams(3),
        name="moba_prompt",
    )(slopes, q, kb, vb)


def _load_page(page_ref):
    return jnp.concatenate(
        [page_ref[pl.ds(h, PAGE_SIZE, stride=N_HEADS), :].astype(BF16) for h in range(N_HEADS)],
        axis=1)


def _page_specs():
    rows = PAGE_SIZE * N_HEADS
    return [
        pl.BlockSpec((None, rows, HEAD_DIM),
                     functools.partial(lambda b, g, pt, pp: (pt[b, g * PAGES_PER_STEP + pp], 0, 0), pp=pp))
        for pp in range(PAGES_PER_STEP)
    ]


def _sample_scores_kernel(pt_ref, qbd_ref, knew_ref, slope_ref, *rest, t_q, past_len):
    del pt_ref
    k_refs = rest[:PAGES_PER_STEP]
    p_ref, pown_ref, l_ref, gate_ref = rest[PAGES_PER_STEP:]
    g = pl.program_id(1)
    n_steps = pl.num_programs(1)
    rows = qbd_ref.shape[0]
    blk = MOBA_BLOCK
    n_past_blocks = past_len // blk
    qbd = qbd_ref[...]
    lane = lax.broadcasted_iota(jnp.int32, (rows, LANES), 1)
    lane_f = lane.astype(F32)

    @pl.when(g == 0)
    def _():
        gate_ref[...] = jnp.zeros_like(gate_ref)

    gates = gate_ref[...]
    prev = None
    for pp in range(PAGES_PER_STEP):
        lg = _dot_nt(qbd, _load_page(k_refs[pp]))
        off = pl.multiple_of((g * PAGES_PER_STEP + pp) * PAGE_SIZE, PAGE_SIZE)
        p_ref[:, pl.ds(off, PAGE_SIZE)] = lg
        if pp % 2 == 0:
            prev = lg
        else:
            bsum = jnp.sum(prev + lg, axis=1, keepdims=True)
            gates = jnp.where(lane == g * (PAGES_PER_STEP // 2) + pp // 2, bsum, gates)
    gate_ref[...] = gates

    @pl.when(g == n_steps - 1)
    def _():
        scale = HEAD_DIM ** -0.5
        slope = slope_ref[...]
        t_row = (lax.broadcasted_iota(jnp.int32, (rows, 1), 0) % t_q).astype(F32)
        sel = _top3_mask(jnp.where(lane < n_past_blocks, gate_ref[...], -jnp.inf), lane_f)

        dist_own = t_row - lane_f
        s_own = _dot_nt(qbd, knew_ref[...]) * scale - slope * dist_own
        s_own = jnp.where(dist_own >= 0, s_own, NEG_BIG)
        m0 = jnp.max(s_own, axis=1, keepdims=True)

        kpos = lax.broadcasted_iota(jnp.int32, (rows, blk), 1).astype(F32)

        def masked_logits(j, carry):
            off = pl.multiple_of(j * blk, blk)
            dist = (past_len + t_row) - ((j * blk).astype(F32) + kpos)
            sj = p_ref[:, pl.ds(off, blk)] * scale - slope * dist
            picked = jnp.sum(jnp.where(lane == j, sel, 0.0), axis=1, keepdims=True)
            sj = jnp.where(picked > 0, sj, NEG_BIG)
            p_ref[:, pl.ds(off, blk)] = sj
            return jnp.maximum(carry, jnp.max(sj, axis=1, keepdims=True))

        m = lax.fori_loop(0, n_past_blocks, masked_logits, m0)

        p_own = jnp.exp(s_own - m)
        pown_ref[...] = p_own

        def exp_sum(j, carry):
            off = pl.multiple_of(j * blk, blk)
            pj = jnp.exp(p_ref[:, pl.ds(off, blk)] - m)
            p_ref[:, pl.ds(off, blk)] = pj
            return carry + jnp.sum(pj, axis=1, keepdims=True)

        l = lax.fori_loop(0, n_past_blocks, exp_sum, jnp.sum(p_own, axis=1, keepdims=True))
        l_ref[...] = jnp.broadcast_to(l, l_ref.shape)


def _sample_scores(page_table, qbd, knew_pad, slope_rows, cache_k_pages, t_q, past_len):
    bsz, rows, _ = qbd.shape
    n_pages = page_table.shape[1]
    n_steps = n_pages // PAGES_PER_STEP
    grid_spec = pltpu.PrefetchScalarGridSpec(
        num_scalar_prefetch=1,
        grid=(bsz, n_steps),
        in_specs=[
            pl.BlockSpec((None, rows, D_ATTN), lambda b, g, pt: (b, 0, 0)),
            pl.BlockSpec((None, LANES, D_ATTN), lambda b, g, pt: (b, 0, 0)),
            pl.BlockSpec((rows, 1), lambda b, g, pt: (0, 0)),
        ] + _page_specs(),
        out_specs=[
            pl.BlockSpec((None, rows, past_len), lambda b, g, pt: (b, 0, 0)),
            pl.BlockSpec((None, rows, LANES), lambda b, g, pt: (b, 0, 0)),
            pl.BlockSpec((None, rows, LANES), lambda b, g, pt: (b, 0, 0)),
        ],
        scratch_shapes=[pltpu.VMEM((rows, LANES), F32)],
    )
    return pl.pallas_call(
        functools.partial(_sample_scores_kernel, t_q=t_q, past_len=past_len),
        grid_spec=grid_spec,
        out_shape=[
            jax.ShapeDtypeStruct((bsz, rows, past_len), F32),
            jax.ShapeDtypeStruct((bsz, rows, LANES), F32),
            jax.ShapeDtypeStruct((bsz, rows, LANES), F32),
        ],
        compiler_params=_cparams(2),
        name="sample_scores",
    )(page_table, qbd, knew_pad, slope_rows, *([cache_k_pages] * PAGES_PER_STEP))


def _sample_pv_kernel(pt_ref, p_ref, pown_ref, l_ref, vnew_ref, *rest, t_q):
    del pt_ref
    v_refs = rest[:PAGES_PER_STEP]
    o_ref, acc_ref = rest[PAGES_PER_STEP:]
    g = pl.program_id(1)
    n_steps = pl.num_programs(1)

    @pl.when(g == 0)
    def _():
        acc_ref[...] = _dot(pown_ref[...].astype(BF16), vnew_ref[...])

    acc = acc_ref[...]
    for pp in range(PAGES_PER_STEP):
        acc = acc + _dot(p_ref[:, pp * PAGE_SIZE:(pp + 1) * PAGE_SIZE].astype(BF16),
                         _load_page(v_refs[pp]))
    acc_ref[...] = acc

    @pl.when(g == n_steps - 1)
    def _():
        for h in range(N_HEADS):
            rs = slice(h * t_q, (h + 1) * t_q)
            cs = slice(h * HEAD_DIM, (h + 1) * HEAD_DIM)
            o_ref[:, cs] = (acc_ref[rs, cs] / l_ref[rs, 0:1]).astype(BF16)


def _sample_pv(page_table, p, pown, l, vnew_pad, cache_v_pages, t_q):
    bsz, rows, _ = p.shape
    n_pages = page_table.shape[1]
    n_steps = n_pages // PAGES_PER_STEP
    step_keys = PAGES_PER_STEP * PAGE_SIZE
    grid_spec = pltpu.PrefetchScalarGridSpec(
        num_scalar_prefetch=1,
        grid=(bsz, n_steps),
        in_specs=[
            pl.BlockSpec((None, rows, step_keys), lambda b, g, pt: (b, 0, g)),
            pl.BlockSpec((None, rows, LANES), lambda b, g, pt: (b, 0, 0)),
            pl.BlockSpec((None, rows, LANES), lambda b, g, pt: (b, 0, 0)),
            pl.BlockSpec((None, LANES, D_ATTN), lambda b, g, pt: (b, 0, 0)),
        ] + _page_specs(),
        out_specs=pl.BlockSpec((None, t_q, D_ATTN), lambda b, g, pt: (b, 0, 0)),
        scratch_shapes=[pltpu.VMEM((rows, D_ATTN), F32)],
    )
    return pl.pallas_call(
        functools.partial(_sample_pv_kernel, t_q=t_q),
        grid_spec=grid_spec,
        out_shape=jax.ShapeDtypeStruct((bsz, t_q, D_ATTN), BF16),
        compiler_params=_cparams(2),
        name="sample_pv",
    )(page_table, p, pown, l, vnew_pad, *([cache_v_pages] * PAGES_PER_STEP))


def _mix_kernel(u_ref, vz_ref, ob_ref, ga_ref, gb_ref, x_ref, ws_ref, bst_ref, wbr_ref, wout_ref,
                nffn_ref, h_ref, hn_ref, oa_ref):
    tm = u_ref.shape[0]
    chunk = ws_ref.shape[1]
    causal = (lax.broadcasted_iota(jnp.int32, (chunk, chunk), 0)
              >= lax.broadcasted_iota(jnp.int32, (chunk, chunk), 1))
    for g in range(GMLP_GROUPS):
        cs = slice(g * GMLP_GROUP_DIM, (g + 1) * GMLP_GROUP_DIM)
        w_g = jnp.where(causal, ws_ref[g], 0.0).astype(BF16)
        b_g = bst_ref[:, g:g + 1]
        for c in range(tm // chunk):
            rs = slice(c * chunk, (c + 1) * chunk)
            mixed = _dot(w_g, vz_ref[rs, cs].astype(BF16)) + b_g
            oa_ref[rs, cs] = (u_ref[rs, cs].astype(F32) * mixed).astype(BF16)
    merged = (ga_ref[...].astype(F32) * _dot(oa_ref[...], wbr_ref[0])
              + gb_ref[...].astype(F32) * _dot(ob_ref[...], wbr_ref[1]))
    h = x_ref[...] + _dot(merged.astype(BF16), wout_ref[...])
    h_ref[...] = h
    hn_ref[...] = _rms(h, nffn_ref[...]).astype(BF16)


def _mix(u, vz, ob, ga, gb, x, ws_chunk, bs_t, w_br_b, w_out_b, norm_ffn, tm):
    m_rows = x.shape[0]
    chunk = ws_chunk.shape[1]
    row = lambda m: (m, 0)
    const2 = lambda m: (0, 0)
    const3 = lambda m: (0, 0, 0)
    once = pl.Buffered(1)
    return pl.pallas_call(
        _mix_kernel,
        grid=(m_rows // tm,),
        in_specs=[
            pl.BlockSpec((tm, D_GMLP), row),
            pl.BlockSpec((tm, D_GMLP), row),
            pl.BlockSpec((tm, D_ATTN), row),
            pl.BlockSpec((tm, D_MODEL), row),
            pl.BlockSpec((tm, D_MODEL), row),
            pl.BlockSpec((tm, D_MODEL), row),
            pl.BlockSpec((GMLP_GROUPS, chunk, chunk), const3, pipeline_mode=once),
            pl.BlockSpec((chunk, GMLP_GROUPS), const2, pipeline_mode=once),
            pl.BlockSpec((2, D_GMLP, D_MODEL), const3, pipeline_mode=once),
            pl.BlockSpec((D_MODEL, D_MODEL), const2, pipeline_mode=once),
            pl.BlockSpec((1, D_MODEL), const2, pipeline_mode=once),
        ],
        out_specs=[pl.BlockSpec((tm, D_MODEL), row), pl.BlockSpec((tm, D_MODEL), row)],
        out_shape=[jax.ShapeDtypeStruct((m_rows, D_MODEL), F32),
                   jax.ShapeDtypeStruct((m_rows, D_MODEL), BF16)],
        scratch_shapes=[pltpu.VMEM((tm, D_GMLP), BF16)],
        compiler_params=_cparams(1),
        name="mix",
    )(u, vz, ob, ga, gb, x, ws_chunk, bs_t, w_br_b, w_out_b, norm_ffn.reshape(1, D_MODEL))


def _ffn_kernel(hn_ref, wa_ref, wb_ref, wc_ref, bc_ref, wd_ref, h_ref, nfin_ref, init_ref,
                y_ref, tail_ref, acc_ref, halo_ref, *, seq_len, tiles_per_seq):
    m = pl.program_id(0)
    n = pl.program_id(1)
    tm = hn_ref.shape[0]
    tf = wa_ref.shape[1]
    hn = hn_ref[...]
    a = _dot(hn, wa_ref[...])
    b = _dot(hn, wb_ref[...])
    row = lax.broadcasted_iota(jnp.int32, (tm, tf), 0)
    r1 = pltpu.roll(a, 1, 0)
    r2 = pltpu.roll(a, 2, 0)
    if tiles_per_seq is None:
        pos = row % seq_len
        a1 = jnp.where(pos >= 1, r1, init_ref[0])
        a2 = jnp.where(pos >= 2, r2, init_ref[1])
        tail_ref[...] = a
    else:
        prev = jnp.where(m % tiles_per_seq == 0, init_ref[0], halo_ref[n])
        p1 = prev[SUBLANES - 1:SUBLANES, :]
        p2 = prev[SUBLANES - 2:SUBLANES - 1, :]
        a1 = jnp.where(row == 0, p1, r1)
        a2 = jnp.where(row == 0, p2, jnp.where(row == 1, p1, r2))
        last = a[tm - SUBLANES:, :]
        halo_ref[n] = last
        tail_ref[...] = last
    wc = wc_ref[...]
    conv = bc_ref[...] + a * wc[2:3, :] + a2 * wc[0:1, :] + a1 * wc[1:2, :]
    act = (jax.nn.gelu(conv) * b).astype(BF16)
    part = _dot(act, wd_ref[...])

    @pl.when(n == 0)
    def _():
        acc_ref[...] = part

    @pl.when(n > 0)
    def _():
        acc_ref[...] += part

    @pl.when(n == pl.num_programs(1) - 1)
    def _():
        y_ref[...] = _rms(h_ref[...] + acc_ref[...], nfin_ref[...])


def _ffn(hn, h, w_up_b, w_conv, b_conv, w_down_b, norm_final, init, tm, seq_len):
    m_rows = hn.shape[0]
    tf = FF_TILE
    n_ff = D_FF // tf
    if seq_len >= tm:
        tiles_per_seq = seq_len // tm
        init_spec = pl.BlockSpec((1, SUBLANES, tf), lambda m, n: (0, m // tiles_per_seq, n))
        tail_rows = (m_rows // tm) * SUBLANES
        tail_spec = pl.BlockSpec((SUBLANES, tf), lambda m, n: (m, n))
    else:
        tiles_per_seq = None
        init_spec = pl.BlockSpec((2, tm, tf), lambda m, n: (0, m, n))
        tail_rows = m_rows
        tail_spec = pl.BlockSpec((tm, tf), lambda m, n: (m, n))
    row = lambda m, n: (m, 0)
    return pl.pallas_call(
        functools.partial(_ffn_kernel, seq_len=seq_len, tiles_per_seq=tiles_per_seq),
        grid=(m_rows // tm, n_ff),
        in_specs=[
            pl.BlockSpec((tm, D_MODEL), row),
            pl.BlockSpec((D_MODEL, tf), lambda m, n: (0, n)),
            pl.BlockSpec((D_MODEL, tf), lambda m, n: (0, n_ff + n)),
            pl.BlockSpec((CONV_W, tf), lambda m, n: (0, n)),
            pl.BlockSpec((1, tf), lambda m, n: (0, n)),
            pl.BlockSpec((tf, D_MODEL), lambda m, n: (n, 0)),
            pl.BlockSpec((tm, D_MODEL), row),
            pl.BlockSpec((1, D_MODEL), lambda m, n: (0, 0)),
            init_spec,
        ],
        out_specs=[pl.BlockSpec((tm, D_MODEL), row), tail_spec],
        out_shape=[jax.ShapeDtypeStruct((m_rows, D_MODEL), F32),
                   jax.ShapeDtypeStruct((tail_rows, D_FF), F32)],
        scratch_shapes=[pltpu.VMEM((tm, D_MODEL), F32),
                        pltpu.VMEM((n_ff, SUBLANES, tf), F32)],
        compiler_params=_cparams(2),
        name="ffn",
    )(hn, w_up_b, w_up_b, w_conv, b_conv.reshape(1, D_FF), w_down_b, h,
      norm_final.reshape(1, D_MODEL), init)


def _alibi_slopes():
    return jnp.exp2(-8.0 * jnp.arange(1, N_HEADS + 1, dtype=F32) / N_HEADS)


def kernel(x_prompt, x_sample, cache_k, cache_v, state_ffn_conv, page_table, norm_mix, w_in,
           norm_gmlp_v, w_spatial, b_spatial, w_branch, w_out, norm_ffn, w_up, w_conv, b_conv,
           w_down, norm_final):
    assert w_in.shape[0] == 1, "single layer"
    bp, seq, _ = x_prompt.shape
    bs, t_q, _ = x_sample.shape
    n_pages = page_table.shape[1]
    past_len = n_pages * PAGE_SIZE
    assert seq % MOBA_BLOCK == 0 and past_len % MOBA_BLOCK == 0 and past_len % CHUNK == 0
    assert t_q <= SUBLANES and past_len // MOBA_BLOCK <= LANES and seq // MOBA_BLOCK <= LANES

    slopes = _alibi_slopes()
    w_in_b = w_in[0].astype(BF16)
    w_br_b = w_branch[0].astype(BF16)
    w_out_b = w_out[0].astype(BF16)
    w_up_b = w_up[0].astype(BF16)
    w_down_b = w_down[0].astype(BF16)
    ws, bsp = w_spatial[0], b_spatial[0]

    tm_p = 512
    xp = x_prompt.reshape(bp * seq, D_MODEL)
    u, vz, q, k, kb, v, vb, ga, gb = _inproj(xp, norm_mix[0], w_in_b, norm_gmlp_v[0], tm_p)
    ob = _moba_prompt(q, kb, vb, slopes, bp, seq)
    h, hn = _mix(u, vz, ob, ga, gb, xp, ws, bsp.T, w_br_b, w_out_b, norm_ffn[0], 256)
    zero_state = jnp.zeros((1, bp * SUBLANES, D_FF), F32)
    yp, tail_p = _ffn(hn, h, w_up_b, w_conv[0], b_conv[0], w_down_b, norm_final, zero_state,
                      tm_p, seq)
    y_prompt = yp.reshape(bp, seq, D_MODEL)
    gv_p = vz.reshape(bp, seq, D_GMLP)[:, seq - CHUNK:][None]
    k_p = k.reshape(1, bp, seq, N_HEADS, HEAD_DIM)
    v_p = v.reshape(1, bp, seq, N_HEADS, HEAD_DIM)
    c_p = tail_p.reshape(bp, seq // tm_p, SUBLANES, D_FF)[:, -1, SUBLANES - (CONV_W - 1):][None]

    m_s = bs * t_q
    xs = x_sample.reshape(m_s, D_MODEL)
    us, vzs, qs, ks, kbs, vs, vbs, gas, gbs = _inproj(xs, norm_mix[0], w_in_b, norm_gmlp_v[0], m_s)
    q4 = qs.reshape(bs, t_q, N_HEADS, HEAD_DIM)
    eye = jnp.eye(N_HEADS, dtype=BF16)
    qbd = (q4.transpose(0, 2, 1, 3)[:, :, :, None, :] * eye[None, :, None, :, None]
           ).reshape(bs, N_HEADS * t_q, D_ATTN)
    pad_rows = ((0, 0), (0, LANES - t_q), (0, 0))
    knew = jnp.pad(kbs.reshape(bs, t_q, D_ATTN), pad_rows)
    vnew = jnp.pad(vbs.reshape(bs, t_q, D_ATTN), pad_rows)
    slope_rows = jnp.repeat(slopes, t_q).reshape(N_HEADS * t_q, 1)
    ck = cache_k[0].reshape(-1, PAGE_SIZE * N_HEADS, HEAD_DIM)
    cv = cache_v[0].reshape(-1, PAGE_SIZE * N_HEADS, HEAD_DIM)
    p, pown, l = _sample_scores(page_table, qbd, knew, slope_rows, ck, t_q, past_len)
    obs = _sample_pv(page_table, p, pown, l, vnew, cv, t_q).reshape(m_s, D_ATTN)
    ws_s = (jnp.eye(bs, dtype=F32)[None, :, None, :, None]
            * ws[:, None, :t_q, None, :t_q]).reshape(GMLP_GROUPS, m_s, m_s)
    bs_t_s = jnp.tile(bsp[:, :t_q], (1, bs)).T
    hs, hns = _mix(us, vzs, obs, gas, gbs, xs, ws_s, bs_t_s, w_br_b, w_out_b, norm_ffn[0], m_s)
    st = state_ffn_conv[0]
    zero_rows = jnp.zeros((bs, t_q - 1, D_FF), F32)
    init1 = jnp.concatenate([st[:, 1:2], zero_rows], axis=1)
    init2 = jnp.concatenate([st[:, 0:1], st[:, 1:2], zero_rows[:, 1:]], axis=1)
    init = jnp.stack([init1.reshape(m_s, D_FF), init2.reshape(m_s, D_FF)])
    ys, tail_s = _ffn(hns, hs, w_up_b, w_conv[0], b_conv[0], w_down_b, norm_final, init, m_s, t_q)
    y_sample = ys.reshape(bs, t_q, D_MODEL)
    gv_s = vzs.reshape(1, bs, t_q, D_GMLP)
    k_s = ks.reshape(1, bs, t_q, N_HEADS, HEAD_DIM)
    v_s = vs.reshape(1, bs, t_q, N_HEADS, HEAD_DIM)
    c_s = tail_s.reshape(bs, t_q, D_FF)[:, t_q - (CONV_W - 1):][None]

    return (y_prompt, y_sample, gv_p, gv_s, k_p, v_p, k_s, v_s, c_p, c_s)
```

```python
import functools

import jax
import jax.numpy as jnp
from jax import lax
from jax.experimental import pallas as pl
from jax.experimental.pallas import tpu as pltpu

F32 = jnp.float32
BF16 = jnp.bfloat16

D_MODEL = 2048
D_GMLP = D_MODEL // 2
GMLP_GROUPS = 8
GMLP_GROUP_DIM = D_GMLP // GMLP_GROUPS
CHUNK = 128
D_ATTN = D_MODEL // 2
HEAD_DIM = 128
N_HEADS = D_ATTN // HEAD_DIM
MOBA_BLOCK = 256
MOBA_TOPK = 3
D_FF = 5632
CONV_W = 3
RMS_EPS = 1e-6
PAGE_SIZE = 128
N_SECTIONS = 9

LANES = 128
SUBLANES = 8
VMEM_LIMIT_BYTES = 56 * 1024 * 1024

NEG_BIG = -1e30
LOG2E = 1.4426950408889634
MOBA_HEADS_PER_STEP = 4
MOBA_BLOCKS_PER_TRIP = 4
MOBA_ONES_ROWS = 16
SAMPLE_CHUNK_BLOCKS = 8
FF_TILE = 512
PAGES_PER_STEP = 8

_NT = (((1,), (1,)), ((), ()))


def _dot(a, b):
    return jnp.dot(a, b, preferred_element_type=F32)


def _dot_nt(a, b):
    return lax.dot_general(a, b, _NT, preferred_element_type=F32)


def _rms(x, g):
    return x * lax.rsqrt(jnp.mean(x * x, axis=-1, keepdims=True) + RMS_EPS) * g


def _cparams(n_axes):
    return pltpu.CompilerParams(dimension_semantics=("arbitrary",) * n_axes,
                                vmem_limit_bytes=VMEM_LIMIT_BYTES)


def _top3_mask(gate, idx_f, axis=1):
    sel = jnp.zeros_like(gate)
    for _ in range(MOBA_TOPK):
        mx = jnp.max(gate, axis=axis, keepdims=True)
        first = jnp.min(jnp.where(gate == mx, idx_f, float(LANES)), axis=axis, keepdims=True)
        pick = (idx_f == first) & (mx > -jnp.inf)
        sel = jnp.where(pick, 1.0, sel)
        gate = jnp.where(pick, -jnp.inf, gate)
    return sel


def _store_heads(dst_ref, dst_b_ref, acc):
    tm = acc.shape[0]
    dst_b_ref[...] = acc.astype(BF16)
    for h in range(N_HEADS):
        dst_ref[pl.ds(h, tm, stride=N_HEADS), :] = acc[:, h * HEAD_DIM:(h + 1) * HEAD_DIM]


def _inproj_kernel(x_ref, nmix_ref, w_ref, ngv_ref, u_ref, vz_ref, q_ref, k_ref, kb_ref, v_ref,
                   vb_ref, ga_ref, gb_ref, xn_ref):
    n = pl.program_id(1)

    @pl.when(n == 0)
    def _():
        xn_ref[...] = _rms(x_ref[...], nmix_ref[...]).astype(BF16)

    def section():
        return _dot(xn_ref[...], w_ref[...])

    @pl.when(n == 0)
    def _():
        u_ref[...] = jax.nn.gelu(section()).astype(BF16)

    @pl.when(n == 1)
    def _():
        vz_ref[...] = _rms(jax.nn.gelu(section()), ngv_ref[...])

    @pl.when(n == 2)
    def _():
        q_ref[...] = section().astype(BF16)

    @pl.when(n == 3)
    def _():
        _store_heads(k_ref, kb_ref, section())

    @pl.when(n == 4)
    def _():
        _store_heads(v_ref, vb_ref, section())

    @pl.when((n == 5) | (n == 6))
    def _():
        ga_ref[...] = jax.nn.sigmoid(section()).astype(BF16)

    @pl.when(n >= 7)
    def _():
        gb_ref[...] = jax.nn.sigmoid(section()).astype(BF16)


def _inproj(x, norm_mix, w_in_b, norm_gmlp_v, tm):
    m_rows = x.shape[0]
    sec = D_GMLP
    row = lambda m, n: (m, 0)
    in_specs = [
        pl.BlockSpec((tm, D_MODEL), row),
        pl.BlockSpec((1, D_MODEL), lambda m, n: (0, 0)),
        pl.BlockSpec((D_MODEL, sec), lambda m, n: (0, n)),
        pl.BlockSpec((1, sec), lambda m, n: (0, 0)),
    ]
    out_specs = [
        pl.BlockSpec((tm, sec), row),
        pl.BlockSpec((tm, sec), row),
        pl.BlockSpec((tm, sec), row),
        pl.BlockSpec((tm * N_HEADS, HEAD_DIM), row),
        pl.BlockSpec((tm, sec), row),
        pl.BlockSpec((tm * N_HEADS, HEAD_DIM), row),
        pl.BlockSpec((tm, sec), row),
        pl.BlockSpec((tm, sec), lambda m, n: (m, jnp.clip(n - 5, 0, 1))),
        pl.BlockSpec((tm, sec), lambda m, n: (m, jnp.clip(n - 7, 0, 1))),
    ]
    out_shape = [
        jax.ShapeDtypeStruct((m_rows, sec), BF16),
        jax.ShapeDtypeStruct((m_rows, sec), F32),
        jax.ShapeDtypeStruct((m_rows, sec), BF16),
        jax.ShapeDtypeStruct((m_rows * N_HEADS, HEAD_DIM), F32),
        jax.ShapeDtypeStruct((m_rows, sec), BF16),
        jax.ShapeDtypeStruct((m_rows * N_HEADS, HEAD_DIM), F32),
        jax.ShapeDtypeStruct((m_rows, sec), BF16),
        jax.ShapeDtypeStruct((m_rows, D_MODEL), BF16),
        jax.ShapeDtypeStruct((m_rows, D_MODEL), BF16),
    ]
    return pl.pallas_call(
        _inproj_kernel,
        grid=(m_rows // tm, N_SECTIONS),
        in_specs=in_specs,
        out_specs=out_specs,
        out_shape=out_shape,
        scratch_shapes=[pltpu.VMEM((tm, D_MODEL), BF16)],
        compiler_params=_cparams(2),
        name="inproj",
    )(x, norm_mix.reshape(1, D_MODEL), w_in_b, norm_gmlp_v.reshape(1, sec))


def _moba_prompt_kernel(slopes_ref, q_ref, kb_ref, vb_ref, o_ref, km_ref, vt_ref, bias_ref, sel_ref,
                        m_ref, t_ref, acc_ref, *, n_blocks):
    hg = pl.program_id(1)
    i = pl.program_id(2)
    blk = MOBA_BLOCK
    c1 = (HEAD_DIM ** -0.5) * LOG2E
    heads = [(hh, slice(hh * HEAD_DIM, (hh + 1) * HEAD_DIM)) for hh in range(MOBA_HEADS_PER_STEP)]

    def slope2(hh):
        return slopes_ref[hg * MOBA_HEADS_PER_STEP + hh] * LOG2E

    @pl.when(i == 0)
    def _():
        key_f = lax.broadcasted_iota(jnp.int32, (blk, blk), 0).astype(F32)
        for hh, cs in heads:
            bias_ref[hh] = slope2(hh) * key_f
            vt_ref[hh, HEAD_DIM:, :] = jnp.ones((MOBA_ONES_ROWS, n_blocks * blk), BF16)
            for j in range(n_blocks):
                rs = slice(j * blk, (j + 1) * blk)
                km_ref[hh, j:j + 1, :] = jnp.sum(kb_ref[rs, cs].astype(F32), axis=0,
                                                 keepdims=True) * (1.0 / blk)
                vt_ref[hh, :HEAD_DIM, rs] = vb_ref[rs, cs].astype(F32).T.astype(BF16)

    blk_i = lax.broadcasted_iota(jnp.int32, (n_blocks, blk), 0)
    causal = (lax.broadcasted_iota(jnp.int32, (blk, blk), 1)
              >= lax.broadcasted_iota(jnp.int32, (blk, blk), 0))
    start = pl.multiple_of(i * blk, blk)

    for hh, cs in heads:
        q = q_ref[:, cs]
        km = km_ref[hh]
        km_hi = km.astype(BF16)
        km_lo = (km - km_hi.astype(F32)).astype(BF16)
        gate = _dot_nt(km_hi, q) + _dot_nt(km_lo, q)
        gate = jnp.where(blk_i < i, gate, -jnp.inf)
        sel_ref[hh] = _top3_mask(gate, blk_i.astype(F32), axis=0)
        t = _dot_nt(kb_ref[pl.ds(start, blk), cs], q) * c1 + bias_ref[hh]
        t = jnp.where(causal, t, NEG_BIG)
        t_ref[hh, i] = t
        m_ref[hh] = jnp.max(t, axis=0, keepdims=True)

    def shift(hh, j):
        return slope2(hh) * lax.convert_element_type((i - j) * blk, F32)

    def pass1(j0, nb):
        for hh, cs in heads:
            m = m_ref[hh]
            for j in [j0 + d for d in range(nb)]:
                off = pl.multiple_of(j * blk, blk)
                t = _dot_nt(kb_ref[pl.ds(off, blk), cs], q_ref[:, cs]) * c1 + bias_ref[hh]
                t_ref[hh, j] = t
                picked = sel_ref[hh, pl.ds(j, 1), :] > 0.0
                m_blk = jnp.max(t, axis=0, keepdims=True) - shift(hh, j)
                m = jnp.where(picked, jnp.maximum(m, m_blk), m)
            m_ref[hh] = m

    def pass2(j0, nb):
        off = pl.multiple_of(j0 * blk, blk)
        for hh, cs in heads:
            ps = []
            for j in [j0 + d for d in range(nb)]:
                picked = sel_ref[hh, pl.ds(j, 1), :] > 0.0
                sub = jnp.where(picked, m_ref[hh] + shift(hh, j), -NEG_BIG)
                ps.append(jnp.exp2(t_ref[hh, j] - sub).astype(BF16))
            p = ps[0] if nb == 1 else jnp.concatenate(ps, axis=0)
            acc_ref[hh] += _dot(vt_ref[hh, :, pl.ds(off, nb * blk)], p)

    def over_past_blocks(fn):
        done = 0
        un = MOBA_BLOCKS_PER_TRIP
        while un >= 1:
            n_trips = (i - done) // un

            def trip(t, carry, un=un, base=done):
                fn(base + t * un, un)
                return carry

            lax.fori_loop(0, n_trips, trip, 0)
            done = done + n_trips * un
            un //= 2

    over_past_blocks(pass1)
    for hh, cs in heads:
        p = jnp.exp2(t_ref[hh, i] - m_ref[hh])
        acc_ref[hh] = _dot(vt_ref[hh, :, pl.ds(start, blk)], p.astype(BF16))
    over_past_blocks(pass2)
    for hh, cs in heads:
        acc = acc_ref[hh]
        o_ref[:, cs] = (acc[:HEAD_DIM] / acc[HEAD_DIM:HEAD_DIM + 1]).T.astype(BF16)


def _moba_prompt(q, kb, vb, slopes, bsz, seq):
    n_blocks = seq // MOBA_BLOCK
    blk = MOBA_BLOCK
    hps = MOBA_HEADS_PER_STEP
    width = hps * HEAD_DIM
    qmap = lambda b, hg, i: (b * n_blocks + i, hg)
    kvmap = lambda b, hg, i: (b, hg)
    return pl.pallas_call(
        functools.partial(_moba_prompt_kernel, n_blocks=n_blocks),
        grid=(bsz, N_HEADS // hps, n_blocks),
        in_specs=[
            pl.BlockSpec(memory_space=pltpu.SMEM),
            pl.BlockSpec((blk, width), qmap),
            pl.BlockSpec((seq, width), kvmap),
            pl.BlockSpec((seq, width), kvmap),
        ],
        out_specs=pl.BlockSpec((blk, width), qmap),
        out_shape=jax.ShapeDtypeStruct((bsz * seq, D_ATTN), BF16),
        scratch_shapes=[
            pltpu.VMEM((hps, n_blocks, HEAD_DIM), F32),
            pltpu.VMEM((hps, HEAD_DIM + MOBA_ONES_ROWS, seq), BF16),
            pltpu.VMEM((hps, blk, blk), F32),
            pltpu.VMEM((hps, n_blocks, blk), F32),
            pltpu.VMEM((hps, 1, blk), F32),
            pltpu.VMEM((hps, n_blocks, blk, blk), F32),
            pltpu.VMEM((hps, HEAD_DIM + MOBA_ONES_ROWS, blk), F32),
        ],
        compiler_params=_cparams(3),
        name="moba_prompt",
    )(slopes, q, kb, vb)


def _load_page(page_ref):
    return jnp.concatenate(
        [page_ref[pl.ds(h, PAGE_SIZE, stride=N_HEADS), :].astype(BF16) for h in range(N_HEADS)],
        axis=1)


def _page_specs():
    rows = PAGE_SIZE * N_HEADS
    return [
        pl.BlockSpec((None, rows, HEAD_DIM),
                     functools.partial(lambda b, g, pt, pp: (pt[b, g * PAGES_PER_STEP + pp], 0, 0), pp=pp))
        for pp in range(PAGES_PER_STEP)
    ]


def _sample_scores_kernel(pt_ref, qbd_ref, knew_ref, slope_ref, *rest, t_q, past_len):
    del pt_ref
    k_refs = rest[:PAGES_PER_STEP]
    p_ref, pown_ref, l_ref, gate_ref = rest[PAGES_PER_STEP:]
    g = pl.program_id(1)
    n_steps = pl.num_programs(1)
    rows = qbd_ref.shape[0]
    blk = MOBA_BLOCK
    n_past_blocks = past_len // blk
    qbd = qbd_ref[...]
    lane = lax.broadcasted_iota(jnp.int32, (rows, LANES), 1)
    lane_f = lane.astype(F32)

    @pl.when(g == 0)
    def _():
        gate_ref[...] = jnp.zeros_like(gate_ref)

    gates = gate_ref[...]
    prev = None
    for pp in range(PAGES_PER_STEP):
        lg = _dot_nt(qbd, _load_page(k_refs[pp]))
        off = pl.multiple_of((g * PAGES_PER_STEP + pp) * PAGE_SIZE, PAGE_SIZE)
        p_ref[:, pl.ds(off, PAGE_SIZE)] = lg
        if pp % 2 == 0:
            prev = lg
        else:
            bsum = jnp.sum(prev + lg, axis=1, keepdims=True)
            gates = jnp.where(lane == g * (PAGES_PER_STEP // 2) + pp // 2, bsum, gates)
    gate_ref[...] = gates

    @pl.when(g == n_steps - 1)
    def _():
        c1 = (HEAD_DIM ** -0.5) * LOG2E
        slope2 = slope_ref[...] * LOG2E
        t_row = (lax.broadcasted_iota(jnp.int32, (rows, 1), 0) % t_q).astype(F32)
        sel = _top3_mask(jnp.where(lane < n_past_blocks, gate_ref[...], -jnp.inf), lane_f)

        s_own = _dot_nt(qbd, knew_ref[...]) * c1 + slope2 * lane_f
        s_own = jnp.where(lane_f <= t_row, s_own, NEG_BIG)

        cb = SAMPLE_CHUNK_BLOCKS
        ch = cb * blk
        n_chunks = n_past_blocks // cb
        assert cb & (cb - 1) == 0 and blk & (blk - 1) == 0
        widen = jnp.where(
            jnp.bitwise_and(lax.broadcasted_iota(jnp.int32, (LANES, ch), 0), cb - 1)
            == jnp.right_shift(lax.broadcasted_iota(jnp.int32, (LANES, ch), 1), blk.bit_length() - 1),
            1.0, 0.0).astype(BF16)
        key_f = lax.broadcasted_iota(jnp.int32, (1, ch), 1).astype(F32)

        mx = jnp.full((rows, blk), NEG_BIG, F32)
        for c in range(n_chunks):
            sel_c = jnp.where((lane >= c * cb) & (lane < (c + 1) * cb), sel, 0.0).astype(BF16)
            picked = _dot(sel_c, widen)
            s = p_ref[:, c * ch:(c + 1) * ch] * c1 + slope2 * (key_f + float(c * ch - past_len))
            s = jnp.where(picked > 0.0, s, NEG_BIG)
            p_ref[:, c * ch:(c + 1) * ch] = s
            for b in range(cb):
                mx = jnp.maximum(mx, s[:, b * blk:(b + 1) * blk])
        m = jnp.maximum(jnp.max(mx, axis=1, keepdims=True), jnp.max(s_own, axis=1, keepdims=True))

        p_own = jnp.exp2(s_own - m)
        pown_ref[...] = p_own
        acc = jnp.zeros((rows, blk), F32)
        for c in range(n_chunks):
            pc = jnp.exp2(p_ref[:, c * ch:(c + 1) * ch] - m)
            p_ref[:, c * ch:(c + 1) * ch] = pc
            for b in range(cb):
                acc = acc + pc[:, b * blk:(b + 1) * blk]
        l = jnp.sum(acc, axis=1, keepdims=True) + jnp.sum(p_own, axis=1, keepdims=True)
        l_ref[...] = jnp.broadcast_to(l, l_ref.shape)


def _sample_scores(page_table, qbd, knew_pad, slope_rows, cache_k_pages, t_q, past_len):
    bsz, rows, _ = qbd.shape
    n_pages = page_table.shape[1]
    n_steps = n_pages // PAGES_PER_STEP
    grid_spec = pltpu.PrefetchScalarGridSpec(
        num_scalar_prefetch=1,
        grid=(bsz, n_steps),
        in_specs=[
            pl.BlockSpec((None, rows, D_ATTN), lambda b, g, pt: (b, 0, 0)),
            pl.BlockSpec((None, LANES, D_ATTN), lambda b, g, pt: (b, 0, 0)),
            pl.BlockSpec((rows, 1), lambda b, g, pt: (0, 0)),
        ] + _page_specs(),
        out_specs=[
            pl.BlockSpec((None, rows, past_len), lambda b, g, pt: (b, 0, 0)),
            pl.BlockSpec((None, rows, LANES), lambda b, g, pt: (b, 0, 0)),
            pl.BlockSpec((None, rows, LANES), lambda b, g, pt: (b, 0, 0)),
        ],
        scratch_shapes=[pltpu.VMEM((rows, LANES), F32)],
    )
    return pl.pallas_call(
        functools.partial(_sample_scores_kernel, t_q=t_q, past_len=past_len),
        grid_spec=grid_spec,
        out_shape=[
            jax.ShapeDtypeStruct((bsz, rows, past_len), F32),
            jax.ShapeDtypeStruct((bsz, rows, LANES), F32),
            jax.ShapeDtypeStruct((bsz, rows, LANES), F32),
        ],
        compiler_params=_cparams(2),
        name="sample_scores",
    )(page_table, qbd, knew_pad, slope_rows, *([cache_k_pages] * PAGES_PER_STEP))


def _sample_pv_kernel(pt_ref, p_ref, pown_ref, l_ref, vnew_ref, *rest, t_q):
    del pt_ref
    v_refs = rest[:PAGES_PER_STEP]
    o_ref, acc_ref = rest[PAGES_PER_STEP:]
    g = pl.program_id(1)
    n_steps = pl.num_programs(1)

    @pl.when(g == 0)
    def _():
        acc_ref[...] = _dot(pown_ref[...].astype(BF16), vnew_ref[...])

    acc = acc_ref[...]
    for pp in range(PAGES_PER_STEP):
        acc = acc + _dot(p_ref[:, pp * PAGE_SIZE:(pp + 1) * PAGE_SIZE].astype(BF16),
                         _load_page(v_refs[pp]))
    acc_ref[...] = acc

    @pl.when(g == n_steps - 1)
    def _():
        for h in range(N_HEADS):
            rs = slice(h * t_q, (h + 1) * t_q)
            cs = slice(h * HEAD_DIM, (h + 1) * HEAD_DIM)
            o_ref[:, cs] = (acc_ref[rs, cs] / l_ref[rs, 0:1]).astype(BF16)


def _sample_pv(page_table, p, pown, l, vnew_pad, cache_v_pages, t_q):
    bsz, rows, _ = p.shape
    n_pages = page_table.shape[1]
    n_steps = n_pages // PAGES_PER_STEP
    step_keys = PAGES_PER_STEP * PAGE_SIZE
    grid_spec = pltpu.PrefetchScalarGridSpec(
        num_scalar_prefetch=1,
        grid=(bsz, n_steps),
        in_specs=[
            pl.BlockSpec((None, rows, step_keys), lambda b, g, pt: (b, 0, g)),
            pl.BlockSpec((None, rows, LANES), lambda b, g, pt: (b, 0, 0)),
            pl.BlockSpec((None, rows, LANES), lambda b, g, pt: (b, 0, 0)),
            pl.BlockSpec((None, LANES, D_ATTN), lambda b, g, pt: (b, 0, 0)),
        ] + _page_specs(),
        out_specs=pl.BlockSpec((None, t_q, D_ATTN), lambda b, g, pt: (b, 0, 0)),
        scratch_shapes=[pltpu.VMEM((rows, D_ATTN), F32)],
    )
    return pl.pallas_call(
        functools.partial(_sample_pv_kernel, t_q=t_q),
        grid_spec=grid_spec,
        out_shape=jax.ShapeDtypeStruct((bsz, t_q, D_ATTN), BF16),
        compiler_params=_cparams(2),
        name="sample_pv",
    )(page_table, p, pown, l, vnew_pad, *([cache_v_pages] * PAGES_PER_STEP))


def _mix_kernel(u_ref, vz_ref, ob_ref, ga_ref, gb_ref, x_ref, ws_ref, bst_ref, wbr_ref, wout_ref,
                nffn_ref, h_ref, hn_ref, oa_ref):
    tm = u_ref.shape[0]
    chunk = ws_ref.shape[1]
    causal = (lax.broadcasted_iota(jnp.int32, (chunk, chunk), 0)
              >= lax.broadcasted_iota(jnp.int32, (chunk, chunk), 1))
    for g in range(GMLP_GROUPS):
        cs = slice(g * GMLP_GROUP_DIM, (g + 1) * GMLP_GROUP_DIM)
        w_g = jnp.where(causal, ws_ref[g], 0.0).astype(BF16)
        b_g = bst_ref[:, g:g + 1]
        for c in range(tm // chunk):
            rs = slice(c * chunk, (c + 1) * chunk)
            mixed = _dot(w_g, vz_ref[rs, cs].astype(BF16)) + b_g
            oa_ref[rs, cs] = (u_ref[rs, cs].astype(F32) * mixed).astype(BF16)
    merged = (ga_ref[...].astype(F32) * _dot(oa_ref[...], wbr_ref[0])
              + gb_ref[...].astype(F32) * _dot(ob_ref[...], wbr_ref[1]))
    h = x_ref[...] + _dot(merged.astype(BF16), wout_ref[...])
    h_ref[...] = h
    hn_ref[...] = _rms(h, nffn_ref[...]).astype(BF16)


def _mix(u, vz, ob, ga, gb, x, ws_chunk, bs_t, w_br_b, w_out_b, norm_ffn, tm):
    m_rows = x.shape[0]
    chunk = ws_chunk.shape[1]
    row = lambda m: (m, 0)
    const2 = lambda m: (0, 0)
    const3 = lambda m: (0, 0, 0)
    once = pl.Buffered(1)
    return pl.pallas_call(
        _mix_kernel,
        grid=(m_rows // tm,),
        in_specs=[
            pl.BlockSpec((tm, D_GMLP), row),
            pl.BlockSpec((tm, D_GMLP), row),
            pl.BlockSpec((tm, D_ATTN), row),
            pl.BlockSpec((tm, D_MODEL), row),
            pl.BlockSpec((tm, D_MODEL), row),
            pl.BlockSpec((tm, D_MODEL), row),
            pl.BlockSpec((GMLP_GROUPS, chunk, chunk), const3, pipeline_mode=once),
            pl.BlockSpec((chunk, GMLP_GROUPS), const2, pipeline_mode=once),
            pl.BlockSpec((2, D_GMLP, D_MODEL), const3, pipeline_mode=once),
            pl.BlockSpec((D_MODEL, D_MODEL), const2, pipeline_mode=once),
            pl.BlockSpec((1, D_MODEL), const2, pipeline_mode=once),
        ],
        out_specs=[pl.BlockSpec((tm, D_MODEL), row), pl.BlockSpec((tm, D_MODEL), row)],
        out_shape=[jax.ShapeDtypeStruct((m_rows, D_MODEL), F32),
                   jax.ShapeDtypeStruct((m_rows, D_MODEL), BF16)],
        scratch_shapes=[pltpu.VMEM((tm, D_GMLP), BF16)],
        compiler_params=_cparams(1),
        name="mix",
    )(u, vz, ob, ga, gb, x, ws_chunk, bs_t, w_br_b, w_out_b, norm_ffn.reshape(1, D_MODEL))


def _ffn_kernel(hn_ref, wa_ref, wb_ref, wc_ref, bc_ref, wd_ref, h_ref, nfin_ref, init_ref,
                y_ref, tail_ref, acc_ref, halo_ref, *, seq_len, tiles_per_seq):
    m = pl.program_id(0)
    n = pl.program_id(1)
    tm = hn_ref.shape[0]
    tf = wa_ref.shape[1]

    @pl.when(n == 0)
    def _():
        acc_ref[...] = h_ref[...]

    hn = hn_ref[...]
    a = _dot(hn, wa_ref[...])
    b = _dot(hn, wb_ref[...])
    row = lax.broadcasted_iota(jnp.int32, (tm, tf), 0)
    r1 = pltpu.roll(a, 1, 0)
    r2 = pltpu.roll(a, 2, 0)
    if tiles_per_seq is None:
        pos = row % seq_len
        a1 = jnp.where(pos >= 1, r1, init_ref[0])
        a2 = jnp.where(pos >= 2, r2, init_ref[1])
        tail_ref[...] = a
    else:
        prev = jnp.where(m % tiles_per_seq == 0, init_ref[0], halo_ref[n])
        p1 = prev[SUBLANES - 1:SUBLANES, :]
        p2 = prev[SUBLANES - 2:SUBLANES - 1, :]
        a1 = jnp.where(row == 0, p1, r1)
        a2 = jnp.where(row == 0, p2, jnp.where(row == 1, p1, r2))
        last = a[tm - SUBLANES:, :]
        halo_ref[n] = last
        tail_ref[...] = last
    wc = wc_ref[...]
    conv = bc_ref[...] + a * wc[2:3, :] + a2 * wc[0:1, :] + a1 * wc[1:2, :]
    act = (jax.nn.gelu(conv) * b).astype(BF16)
    acc_ref[...] += _dot(act, wd_ref[...])

    @pl.when(n == pl.num_programs(1) - 1)
    def _():
        y_ref[...] = _rms(acc_ref[...], nfin_ref[...])


def _ffn(hn, h, w_up_b, w_conv, b_conv, w_down_b, norm_final, init, tm, seq_len):
    m_rows = hn.shape[0]
    tf = FF_TILE
    n_ff = D_FF // tf
    if seq_len >= tm:
        tiles_per_seq = seq_len // tm
        init_spec = pl.BlockSpec((1, SUBLANES, tf), lambda m, n: (0, m // tiles_per_seq, n))
        tail_rows = (m_rows // tm) * SUBLANES
        tail_spec = pl.BlockSpec((SUBLANES, tf), lambda m, n: (m, n))
    else:
        tiles_per_seq = None
        init_spec = pl.BlockSpec((2, tm, tf), lambda m, n: (0, m, n))
        tail_rows = m_rows
        tail_spec = pl.BlockSpec((tm, tf), lambda m, n: (m, n))
    row = lambda m, n: (m, 0)
    return pl.pallas_call(
        functools.partial(_ffn_kernel, seq_len=seq_len, tiles_per_seq=tiles_per_seq),
        grid=(m_rows // tm, n_ff),
        in_specs=[
            pl.BlockSpec((tm, D_MODEL), row),
            pl.BlockSpec((D_MODEL, tf), lambda m, n: (0, n)),
            pl.BlockSpec((D_MODEL, tf), lambda m, n: (0, n_ff + n)),
            pl.BlockSpec((CONV_W, tf), lambda m, n: (0, n)),
            pl.BlockSpec((1, tf), lambda m, n: (0, n)),
            pl.BlockSpec((tf, D_MODEL), lambda m, n: (n, 0)),
            pl.BlockSpec((tm, D_MODEL), row),
            pl.BlockSpec((1, D_MODEL), lambda m, n: (0, 0)),
            init_spec,
        ],
        out_specs=[pl.BlockSpec((tm, D_MODEL), row), tail_spec],
        out_shape=[jax.ShapeDtypeStruct((m_rows, D_MODEL), F32),
                   jax.ShapeDtypeStruct((tail_rows, D_FF), F32)],
        scratch_shapes=[pltpu.VMEM((tm, D_MODEL), F32),
                        pltpu.VMEM((n_ff, SUBLANES, tf), F32)],
        compiler_params=_cparams(2),
        name="ffn",
    )(hn, w_up_b, w_up_b, w_conv, b_conv.reshape(1, D_FF), w_down_b, h,
      norm_final.reshape(1, D_MODEL), init)


def _alibi_slopes():
    return jnp.exp2(-8.0 * jnp.arange(1, N_HEADS + 1, dtype=F32) / N_HEADS)


def kernel(x_prompt, x_sample, cache_k, cache_v, state_ffn_conv, page_table, norm_mix, w_in,
           norm_gmlp_v, w_spatial, b_spatial, w_branch, w_out, norm_ffn, w_up, w_conv, b_conv,
           w_down, norm_final):
    assert w_in.shape[0] == 1, "single layer"
    bp, seq, _ = x_prompt.shape
    bs, t_q, _ = x_sample.shape
    n_pages = page_table.shape[1]
    past_len = n_pages * PAGE_SIZE
    assert seq % MOBA_BLOCK == 0 and past_len % MOBA_BLOCK == 0 and past_len % CHUNK == 0
    assert t_q <= SUBLANES and past_len // MOBA_BLOCK <= LANES and seq // MOBA_BLOCK <= LANES

    slopes = _alibi_slopes()
    w_in_b = w_in[0].astype(BF16)
    w_br_b = w_branch[0].astype(BF16)
    w_out_b = w_out[0].astype(BF16)
    w_up_b = w_up[0].astype(BF16)
    w_down_b = w_down[0].astype(BF16)
    ws, bsp = w_spatial[0], b_spatial[0]

    tm_p = 512
    xp = x_prompt.reshape(bp * seq, D_MODEL)
    u, vz, q, k, kb, v, vb, ga, gb = _inproj(xp, norm_mix[0], w_in_b, norm_gmlp_v[0], tm_p)
    ob = _moba_prompt(q, kb, vb, slopes, bp, seq)
    h, hn = _mix(u, vz, ob, ga, gb, xp, ws, bsp.T, w_br_b, w_out_b, norm_ffn[0], 256)
    zero_state = jnp.zeros((1, bp * SUBLANES, D_FF), F32)
    yp, tail_p = _ffn(hn, h, w_up_b, w_conv[0], b_conv[0], w_down_b, norm_final, zero_state,
                      tm_p, seq)
    y_prompt = yp.reshape(bp, seq, D_MODEL)
    gv_p = vz.reshape(bp, seq, D_GMLP)[:, seq - CHUNK:][None]
    k_p = k.reshape(1, bp, seq, N_HEADS, HEAD_DIM)
    v_p = v.reshape(1, bp, seq, N_HEADS, HEAD_DIM)
    c_p = tail_p.reshape(bp, seq // tm_p, SUBLANES, D_FF)[:, -1, SUBLANES - (CONV_W - 1):][None]

    m_s = bs * t_q
    xs = x_sample.reshape(m_s, D_MODEL)
    us, vzs, qs, ks, kbs, vs, vbs, gas, gbs = _inproj(xs, norm_mix[0], w_in_b, norm_gmlp_v[0], m_s)
    q4 = qs.reshape(bs, t_q, N_HEADS, HEAD_DIM)
    eye = jnp.eye(N_HEADS, dtype=BF16)
    qbd = (q4.transpose(0, 2, 1, 3)[:, :, :, None, :] * eye[None, :, None, :, None]
           ).reshape(bs, N_HEADS * t_q, D_ATTN)
    pad_rows = ((0, 0), (0, LANES - t_q), (0, 0))
    knew = jnp.pad(kbs.reshape(bs, t_q, D_ATTN), pad_rows)
    vnew = jnp.pad(vbs.reshape(bs, t_q, D_ATTN), pad_rows)
    slope_rows = jnp.repeat(slopes, t_q).reshape(N_HEADS * t_q, 1)
    ck = cache_k[0].reshape(-1, PAGE_SIZE * N_HEADS, HEAD_DIM)
    cv = cache_v[0].reshape(-1, PAGE_SIZE * N_HEADS, HEAD_DIM)
    p, pown, l = _sample_scores(page_table, qbd, knew, slope_rows, ck, t_q, past_len)
    obs = _sample_pv(page_table, p, pown, l, vnew, cv, t_q).reshape(m_s, D_ATTN)
    ws_s = (jnp.eye(bs, dtype=F32)[None, :, None, :, None]
            * ws[:, None, :t_q, None, :t_q]).reshape(GMLP_GROUPS, m_s, m_s)
    bs_t_s = jnp.tile(bsp[:, :t_q], (1, bs)).T
    hs, hns = _mix(us, vzs, obs, gas, gbs, xs, ws_s, bs_t_s, w_br_b, w_out_b, norm_ffn[0], m_s)
    st = state_ffn_conv[0]
    zero_rows = jnp.zeros((bs, t_q - 1, D_FF), F32)
    init1 = jnp.concatenate([st[:, 1:2], zero_rows], axis=1)
    init2 = jnp.concatenate([st[:, 0:1], st[:, 1:2], zero_rows[:, 1:]], axis=1)
    init = jnp.stack([init1.reshape(m_s, D_FF), init2.reshape(m_s, D_FF)])
    ys, tail_s = _ffn(hns, hs, w_up_b, w_conv[0], b_conv[0], w_down_b, norm_final, init, m_s, t_q)
    y_sample = ys.reshape(bs, t_q, D_MODEL)
    gv_s = vzs.reshape(1, bs, t_q, D_GMLP)
    k_s = ks.reshape(1, bs, t_q, N_HEADS, HEAD_DIM)
    v_s = vs.reshape(1, bs, t_q, N_HEADS, HEAD_DIM)
    c_s = tail_s.reshape(bs, t_q, D_FF)[:, t_q - (CONV_W - 1):][None]

    return (y_prompt, y_sample, gv_p, gv_s, k_p, v_p, k_s, v_s, c_p, c_s)
```

```python
import functools

import jax
import jax.numpy as jnp
from jax import lax
from jax.experimental import pallas as pl
from jax.experimental.pallas import tpu as pltpu

F32 = jnp.float32
BF16 = jnp.bfloat16

D_MODEL = 2048
D_GMLP = D_MODEL // 2
GMLP_GROUPS = 8
GMLP_GROUP_DIM = D_GMLP // GMLP_GROUPS
CHUNK = 128
D_ATTN = D_MODEL // 2
HEAD_DIM = 128
N_HEADS = D_ATTN // HEAD_DIM
MOBA_BLOCK = 256
MOBA_TOPK = 3
D_FF = 5632
CONV_W = 3
RMS_EPS = 1e-6
PAGE_SIZE = 128
N_SECTIONS = 9

LANES = 128
SUBLANES = 8
VMEM_LIMIT_BYTES = 56 * 1024 * 1024

NEG_BIG = -1e30
LOG2E = 1.4426950408889634
MOBA_HEADS_PER_STEP = 4
MOBA_BLOCKS_PER_TRIP = 4
MOBA_ONES_ROWS = 16
SAMPLE_CHUNK_BLOCKS = 8
FF_TILE = 512
PAGES_PER_STEP = 16
PROMPT_TM_NORM = 1024
PROMPT_TM_INPROJ = 512
PROMPT_TM_MIX = 256
PROMPT_TM_FFN = 1024

_NT = (((1,), (1,)), ((), ()))


def _dot(a, b):
    return jnp.dot(a, b, preferred_element_type=F32)


def _dot_nt(a, b):
    return lax.dot_general(a, b, _NT, preferred_element_type=F32)


def _rms(x, g):
    return x * lax.rsqrt(jnp.mean(x * x, axis=-1, keepdims=True) + RMS_EPS) * g


def _cparams(n_axes):
    return pltpu.CompilerParams(dimension_semantics=("arbitrary",) * n_axes,
                                vmem_limit_bytes=VMEM_LIMIT_BYTES)


def _top3_mask(gate, idx_f, axis=1):
    sel = jnp.zeros_like(gate)
    for _ in range(MOBA_TOPK):
        mx = jnp.max(gate, axis=axis, keepdims=True)
        first = jnp.min(jnp.where(gate == mx, idx_f, float(LANES)), axis=axis, keepdims=True)
        pick = (idx_f == first) & (mx > -jnp.inf)
        sel = jnp.where(pick, 1.0, sel)
        gate = jnp.where(pick, -jnp.inf, gate)
    return sel


def _store_heads(dst_ref, dst_b_ref, acc):
    tm = acc.shape[0]
    dst_b_ref[...] = acc.astype(BF16)
    for h in range(N_HEADS):
        dst_ref[pl.ds(h, tm, stride=N_HEADS), :] = acc[:, h * HEAD_DIM:(h + 1) * HEAD_DIM]


def _norm_kernel(x_ref, g_ref, o_ref):
    o_ref[...] = _rms(x_ref[...], g_ref[...]).astype(BF16)


def _norm_bf16(x, g, tm):
    m_rows, d = x.shape
    return pl.pallas_call(
        _norm_kernel,
        grid=(m_rows // tm,),
        in_specs=[pl.BlockSpec((tm, d), lambda m: (m, 0)), pl.BlockSpec((1, d), lambda m: (0, 0))],
        out_specs=pl.BlockSpec((tm, d), lambda m: (m, 0)),
        out_shape=jax.ShapeDtypeStruct((m_rows, d), BF16),
        compiler_params=_cparams(1),
        name="norm",
    )(x, g.reshape(1, d))


def _inproj_kernel(xn_ref, w_ref, ngv_ref, u_ref, vz_ref, q_ref, k_ref, kb_ref, v_ref, vb_ref,
                   ga_ref, gb_ref, wb_ref):
    n = pl.program_id(0)

    @pl.when(pl.program_id(1) == 0)
    def _():
        wb_ref[...] = w_ref[...].astype(BF16)

    def section():
        return _dot(xn_ref[...], wb_ref[...])

    @pl.when(n == 0)
    def _():
        u_ref[...] = jax.nn.gelu(section()).astype(BF16)

    @pl.when(n == 1)
    def _():
        vz_ref[...] = _rms(jax.nn.gelu(section()), ngv_ref[...])

    @pl.when(n == 2)
    def _():
        q_ref[...] = section().astype(BF16)

    @pl.when(n == 3)
    def _():
        _store_heads(k_ref, kb_ref, section())

    @pl.when(n == 4)
    def _():
        _store_heads(v_ref, vb_ref, section())

    @pl.when((n == 5) | (n == 6))
    def _():
        ga_ref[...] = jax.nn.sigmoid(section()).astype(BF16)

    @pl.when(n >= 7)
    def _():
        gb_ref[...] = jax.nn.sigmoid(section()).astype(BF16)


def _inproj(xn, w_in, norm_gmlp_v, tm):
    m_rows = xn.shape[0]
    sec = D_GMLP
    last = m_rows // tm - 1

    def rows(first, count):
        def index(n, m):
            return jnp.where(n < first, 0, jnp.where(n >= first + count, last, m))
        return index

    def out(first, count=1, block_rows=tm):
        r = rows(first, count)
        return pl.BlockSpec((block_rows, sec), lambda n, m: (r(n, m), jnp.clip(n - first, 0, count - 1)))

    in_specs = [
        pl.BlockSpec((tm, D_MODEL), lambda n, m: (m, 0)),
        pl.BlockSpec((D_MODEL, sec), lambda n, m: (0, n)),
        pl.BlockSpec((1, sec), lambda n, m: (0, 0)),
    ]
    out_specs = [
        out(0),
        out(1),
        out(2),
        pl.BlockSpec((tm * N_HEADS, HEAD_DIM), lambda n, m: (rows(3, 1)(n, m), 0)),
        out(3),
        pl.BlockSpec((tm * N_HEADS, HEAD_DIM), lambda n, m: (rows(4, 1)(n, m), 0)),
        out(4),
        out(5, 2),
        out(7, 2),
    ]
    out_shape = [
        jax.ShapeDtypeStruct((m_rows, sec), BF16),
        jax.ShapeDtypeStruct((m_rows, sec), F32),
        jax.ShapeDtypeStruct((m_rows, sec), BF16),
        jax.ShapeDtypeStruct((m_rows * N_HEADS, HEAD_DIM), F32),
        jax.ShapeDtypeStruct((m_rows, sec), BF16),
        jax.ShapeDtypeStruct((m_rows * N_HEADS, HEAD_DIM), F32),
        jax.ShapeDtypeStruct((m_rows, sec), BF16),
        jax.ShapeDtypeStruct((m_rows, D_MODEL), BF16),
        jax.ShapeDtypeStruct((m_rows, D_MODEL), BF16),
    ]
    return pl.pallas_call(
        _inproj_kernel,
        grid=(N_SECTIONS, m_rows // tm),
        in_specs=in_specs,
        out_specs=out_specs,
        out_shape=out_shape,
        scratch_shapes=[pltpu.VMEM((D_MODEL, sec), BF16)],
        compiler_params=_cparams(2),
        name="inproj",
    )(xn, w_in, norm_gmlp_v.reshape(1, sec))


def _moba_prompt_kernel(slopes_ref, q_ref, kb_ref, vb_ref, o_ref, km_ref, vt_ref, bias_ref, sel_ref,
                        m_ref, t_ref, acc_ref, *, n_blocks):
    hg = pl.program_id(1)
    i = pl.program_id(2)
    blk = MOBA_BLOCK
    c1 = (HEAD_DIM ** -0.5) * LOG2E
    heads = [(hh, slice(hh * HEAD_DIM, (hh + 1) * HEAD_DIM)) for hh in range(MOBA_HEADS_PER_STEP)]

    def slope2(hh):
        return slopes_ref[hg * MOBA_HEADS_PER_STEP + hh] * LOG2E

    @pl.when(i == 0)
    def _():
        key_f = lax.broadcasted_iota(jnp.int32, (blk, blk), 0).astype(F32)
        for hh, cs in heads:
            bias_ref[hh] = slope2(hh) * key_f
            vt_ref[hh, HEAD_DIM:, :] = jnp.ones((MOBA_ONES_ROWS, n_blocks * blk), BF16)
            for j in range(n_blocks):
                rs = slice(j * blk, (j + 1) * blk)
                km_ref[hh, j:j + 1, :] = jnp.sum(kb_ref[rs, cs].astype(F32), axis=0,
                                                 keepdims=True) * (1.0 / blk)
                vt_ref[hh, :HEAD_DIM, rs] = vb_ref[rs, cs].astype(F32).T.astype(BF16)

    blk_i = lax.broadcasted_iota(jnp.int32, (n_blocks, blk), 0)
    causal = (lax.broadcasted_iota(jnp.int32, (blk, blk), 1)
              >= lax.broadcasted_iota(jnp.int32, (blk, blk), 0))
    start = pl.multiple_of(i * blk, blk)

    for hh, cs in heads:
        q = q_ref[:, cs]
        km = km_ref[hh]
        km_hi = km.astype(BF16)
        km_lo = (km - km_hi.astype(F32)).astype(BF16)
        gate = _dot_nt(km_hi, q) + _dot_nt(km_lo, q)
        gate = jnp.where(blk_i < i, gate, -jnp.inf)
        sel_ref[hh] = _top3_mask(gate, blk_i.astype(F32), axis=0)
        t = _dot_nt(kb_ref[pl.ds(start, blk), cs], q) * c1 + bias_ref[hh]
        t = jnp.where(causal, t, NEG_BIG)
        t_ref[hh, i] = t
        m_ref[hh] = jnp.max(t, axis=0, keepdims=True)

    def shift(hh, j):
        return slope2(hh) * lax.convert_element_type((i - j) * blk, F32)

    def pass1(j0, nb):
        for hh, cs in heads:
            m = m_ref[hh]
            for j in [j0 + d for d in range(nb)]:
                off = pl.multiple_of(j * blk, blk)
                t = _dot_nt(kb_ref[pl.ds(off, blk), cs], q_ref[:, cs]) * c1 + bias_ref[hh]
                t_ref[hh, j] = t
                picked = sel_ref[hh, pl.ds(j, 1), :] > 0.0
                m_blk = jnp.max(t, axis=0, keepdims=True) - shift(hh, j)
                m = jnp.where(picked, jnp.maximum(m, m_blk), m)
            m_ref[hh] = m

    def pass2(j0, nb):
        off = pl.multiple_of(j0 * blk, blk)
        for hh, cs in heads:
            ps = []
            for j in [j0 + d for d in range(nb)]:
                picked = sel_ref[hh, pl.ds(j, 1), :] > 0.0
                sub = jnp.where(picked, m_ref[hh] + shift(hh, j), -NEG_BIG)
                ps.append(jnp.exp2(t_ref[hh, j] - sub).astype(BF16))
            p = ps[0] if nb == 1 else jnp.concatenate(ps, axis=0)
            acc_ref[hh] += _dot(vt_ref[hh, :, pl.ds(off, nb * blk)], p)

    def over_past_blocks(fn):
        done = 0
        un = MOBA_BLOCKS_PER_TRIP
        while un >= 1:
            n_trips = (i - done) // un

            def trip(t, carry, un=un, base=done):
                fn(base + t * un, un)
                return carry

            lax.fori_loop(0, n_trips, trip, 0)
            done = done + n_trips * un
            un //= 2

    over_past_blocks(pass1)
    for hh, cs in heads:
        p = jnp.exp2(t_ref[hh, i] - m_ref[hh])
        acc_ref[hh] = _dot(vt_ref[hh, :, pl.ds(start, blk)], p.astype(BF16))
    over_past_blocks(pass2)
    for hh, cs in heads:
        acc = acc_ref[hh]
        o_ref[:, cs] = (acc[:HEAD_DIM] / acc[HEAD_DIM:HEAD_DIM + 1]).T.astype(BF16)


def _moba_prompt(q, kb, vb, slopes, bsz, seq):
    n_blocks = seq // MOBA_BLOCK
    blk = MOBA_BLOCK
    hps = MOBA_HEADS_PER_STEP
    width = hps * HEAD_DIM
    qmap = lambda b, hg, i: (b * n_blocks + i, hg)
    kvmap = lambda b, hg, i: (b, hg)
    return pl.pallas_call(
        functools.partial(_moba_prompt_kernel, n_blocks=n_blocks),
        grid=(bsz, N_HEADS // hps, n_blocks),
        in_specs=[
            pl.BlockSpec(memory_space=pltpu.SMEM),
            pl.BlockSpec((blk, width), qmap),
            pl.BlockSpec((seq, width), kvmap),
            pl.BlockSpec((seq, width), kvmap),
        ],
        out_specs=pl.BlockSpec((blk, width), qmap),
        out_shape=jax.ShapeDtypeStruct((bsz * seq, D_ATTN), BF16),
        scratch_shapes=[
            pltpu.VMEM((hps, n_blocks, HEAD_DIM), F32),
            pltpu.VMEM((hps, HEAD_DIM + MOBA_ONES_ROWS, seq), BF16),
            pltpu.VMEM((hps, blk, blk), F32),
            pltpu.VMEM((hps, n_blocks, blk), F32),
            pltpu.VMEM((hps, 1, blk), F32),
            pltpu.VMEM((hps, n_blocks, blk, blk), F32),
            pltpu.VMEM((hps, HEAD_DIM + MOBA_ONES_ROWS, blk), F32),
        ],
        compiler_params=_cparams(3),
        name="moba_prompt",
    )(slopes, q, kb, vb)


def _load_page(page_ref):
    return jnp.concatenate(
        [page_ref[pl.ds(h, PAGE_SIZE, stride=N_HEADS), :].astype(BF16) for h in range(N_HEADS)],
        axis=1)


def _page_specs():
    rows = PAGE_SIZE * N_HEADS
    return [
        pl.BlockSpec((None, rows, HEAD_DIM),
                     functools.partial(lambda b, g, pt, pp: (pt[b, g * PAGES_PER_STEP + pp], 0, 0), pp=pp))
        for pp in range(PAGES_PER_STEP)
    ]


def _sample_scores_kernel(pt_ref, qbd_ref, knew_ref, slope_ref, *rest, t_q, past_len):
    del pt_ref
    k_refs = rest[:PAGES_PER_STEP]
    p_ref, pown_ref, l_ref, km_ref, ks_ref = rest[PAGES_PER_STEP:]
    g = pl.program_id(1)
    n_steps = pl.num_programs(1)
    rows = qbd_ref.shape[0]
    blk = MOBA_BLOCK
    pages_per_block = blk // PAGE_SIZE
    n_past_blocks = past_len // blk
    qbd = qbd_ref[...]
    lane = lax.broadcasted_iota(jnp.int32, (rows, LANES), 1)
    lane_f = lane.astype(F32)

    @pl.when(g == 0)
    def _():
        km_ref[n_past_blocks:, :] = jnp.zeros((LANES - n_past_blocks, D_ATTN), F32)

    ksum = None
    for pp in range(PAGES_PER_STEP):
        lg = _dot_nt(qbd, _load_page(k_refs[pp]))
        off = pl.multiple_of((g * PAGES_PER_STEP + pp) * PAGE_SIZE, PAGE_SIZE)
        p_ref[:, pl.ds(off, PAGE_SIZE)] = lg
        psum = jnp.sum(k_refs[pp][...].reshape(PAGE_SIZE, N_HEADS, HEAD_DIM), axis=0)
        ksum = psum if pp % pages_per_block == 0 else ksum + psum
        if pp % pages_per_block == pages_per_block - 1:
            bb = pp // pages_per_block
            ks_ref[bb * N_HEADS:(bb + 1) * N_HEADS, :] = ksum * (1.0 / blk)
    step_blocks = PAGES_PER_STEP // pages_per_block
    km_rows = jnp.concatenate(
        [ks_ref[pl.ds(h, step_blocks, stride=N_HEADS), :] for h in range(N_HEADS)], axis=1)
    km_ref[pl.ds(pl.multiple_of(g * step_blocks, step_blocks), step_blocks), :] = km_rows

    @pl.when(g == n_steps - 1)
    def _():
        gate = _dot_nt(qbd, km_ref[...].astype(BF16))
        c1 = (HEAD_DIM ** -0.5) * LOG2E
        slope2 = slope_ref[...] * LOG2E
        t_row = (lax.broadcasted_iota(jnp.int32, (rows, 1), 0) % t_q).astype(F32)
        sel = _top3_mask(jnp.where(lane < n_past_blocks, gate, -jnp.inf), lane_f)

        s_own = _dot_nt(qbd, knew_ref[...]) * c1 + slope2 * lane_f
        s_own = jnp.where(lane_f <= t_row, s_own, NEG_BIG)

        cb = SAMPLE_CHUNK_BLOCKS
        ch = cb * blk
        n_chunks = n_past_blocks // cb
        assert cb & (cb - 1) == 0 and blk & (blk - 1) == 0
        widen = jnp.where(
            jnp.bitwise_and(lax.broadcasted_iota(jnp.int32, (LANES, ch), 0), cb - 1)
            == jnp.right_shift(lax.broadcasted_iota(jnp.int32, (LANES, ch), 1), blk.bit_length() - 1),
            1.0, 0.0).astype(BF16)
        key_f = lax.broadcasted_iota(jnp.int32, (1, ch), 1).astype(F32)

        mx = jnp.full((rows, blk), NEG_BIG, F32)
        for c in range(n_chunks):
            sel_c = jnp.where((lane >= c * cb) & (lane < (c + 1) * cb), sel, 0.0).astype(BF16)
            picked = _dot(sel_c, widen)
            s = p_ref[:, c * ch:(c + 1) * ch] * c1 + slope2 * (key_f + float(c * ch - past_len))
            s = jnp.where(picked > 0.0, s, NEG_BIG)
            p_ref[:, c * ch:(c + 1) * ch] = s
            for b in range(cb):
                mx = jnp.maximum(mx, s[:, b * blk:(b + 1) * blk])
        m = jnp.maximum(jnp.max(mx, axis=1, keepdims=True), jnp.max(s_own, axis=1, keepdims=True))

        p_own = jnp.exp2(s_own - m)
        pown_ref[...] = p_own
        acc = jnp.zeros((rows, blk), F32)
        for c in range(n_chunks):
            pc = jnp.exp2(p_ref[:, c * ch:(c + 1) * ch] - m)
            p_ref[:, c * ch:(c + 1) * ch] = pc
            for b in range(cb):
                acc = acc + pc[:, b * blk:(b + 1) * blk]
        l = jnp.sum(acc, axis=1, keepdims=True) + jnp.sum(p_own, axis=1, keepdims=True)
        l_ref[...] = jnp.broadcast_to(l, l_ref.shape)


def _sample_scores(page_table, qbd, knew_pad, slope_rows, cache_k_pages, t_q, past_len):
    bsz, rows, _ = qbd.shape
    n_pages = page_table.shape[1]
    n_steps = n_pages // PAGES_PER_STEP
    grid_spec = pltpu.PrefetchScalarGridSpec(
        num_scalar_prefetch=1,
        grid=(bsz, n_steps),
        in_specs=[
            pl.BlockSpec((None, rows, D_ATTN), lambda b, g, pt: (b, 0, 0)),
            pl.BlockSpec((None, LANES, D_ATTN), lambda b, g, pt: (b, 0, 0)),
            pl.BlockSpec((rows, 1), lambda b, g, pt: (0, 0)),
        ] + _page_specs(),
        out_specs=[
            pl.BlockSpec((None, rows, past_len), lambda b, g, pt: (b, 0, 0)),
            pl.BlockSpec((None, rows, LANES), lambda b, g, pt: (b, 0, 0)),
            pl.BlockSpec((None, rows, LANES), lambda b, g, pt: (b, 0, 0)),
        ],
        scratch_shapes=[
            pltpu.VMEM((LANES, D_ATTN), F32),
            pltpu.VMEM((PAGES_PER_STEP * PAGE_SIZE // MOBA_BLOCK * N_HEADS, HEAD_DIM), F32),
        ],
    )
    return pl.pallas_call(
        functools.partial(_sample_scores_kernel, t_q=t_q, past_len=past_len),
        grid_spec=grid_spec,
        out_shape=[
            jax.ShapeDtypeStruct((bsz, rows, past_len), F32),
            jax.ShapeDtypeStruct((bsz, rows, LANES), F32),
            jax.ShapeDtypeStruct((bsz, rows, LANES), F32),
        ],
        compiler_params=_cparams(2),
        name="sample_scores",
    )(page_table, qbd, knew_pad, slope_rows, *([cache_k_pages] * PAGES_PER_STEP))


def _sample_pv_kernel(pt_ref, p_ref, pown_ref, l_ref, vnew_ref, *rest, t_q):
    del pt_ref
    v_refs = rest[:PAGES_PER_STEP]
    o_ref, acc_ref = rest[PAGES_PER_STEP:]
    g = pl.program_id(1)
    n_steps = pl.num_programs(1)

    @pl.when(g == 0)
    def _():
        acc_ref[...] = _dot(pown_ref[...].astype(BF16), vnew_ref[...])

    acc = acc_ref[...]
    for pp in range(PAGES_PER_STEP):
        acc = acc + _dot(p_ref[:, pp * PAGE_SIZE:(pp + 1) * PAGE_SIZE].astype(BF16),
                         _load_page(v_refs[pp]))
    acc_ref[...] = acc

    @pl.when(g == n_steps - 1)
    def _():
        for h in range(N_HEADS):
            rs = slice(h * t_q, (h + 1) * t_q)
            cs = slice(h * HEAD_DIM, (h + 1) * HEAD_DIM)
            o_ref[:, cs] = (acc_ref[rs, cs] / l_ref[rs, 0:1]).astype(BF16)


def _sample_pv(page_table, p, pown, l, vnew_pad, cache_v_pages, t_q):
    bsz, rows, _ = p.shape
    n_pages = page_table.shape[1]
    n_steps = n_pages // PAGES_PER_STEP
    step_keys = PAGES_PER_STEP * PAGE_SIZE
    grid_spec = pltpu.PrefetchScalarGridSpec(
        num_scalar_prefetch=1,
        grid=(bsz, n_steps),
        in_specs=[
            pl.BlockSpec((None, rows, step_keys), lambda b, g, pt: (b, 0, g)),
            pl.BlockSpec((None, rows, LANES), lambda b, g, pt: (b, 0, 0)),
            pl.BlockSpec((None, rows, LANES), lambda b, g, pt: (b, 0, 0)),
            pl.BlockSpec((None, LANES, D_ATTN), lambda b, g, pt: (b, 0, 0)),
        ] + _page_specs(),
        out_specs=pl.BlockSpec((None, t_q, D_ATTN), lambda b, g, pt: (b, 0, 0)),
        scratch_shapes=[pltpu.VMEM((rows, D_ATTN), F32)],
    )
    return pl.pallas_call(
        functools.partial(_sample_pv_kernel, t_q=t_q),
        grid_spec=grid_spec,
        out_shape=jax.ShapeDtypeStruct((bsz, t_q, D_ATTN), BF16),
        compiler_params=_cparams(2),
        name="sample_pv",
    )(page_table, p, pown, l, vnew_pad, *([cache_v_pages] * PAGES_PER_STEP))


def _mix_kernel(u_ref, vz_ref, ob_ref, ga_ref, gb_ref, x_ref, ws_ref, bst_ref, wbr_ref, wout_ref,
                nffn_ref, h_ref, hn_ref, oa_ref):
    tm = u_ref.shape[0]
    chunk = ws_ref.shape[1]
    causal = (lax.broadcasted_iota(jnp.int32, (chunk, chunk), 0)
              >= lax.broadcasted_iota(jnp.int32, (chunk, chunk), 1))
    for g in range(GMLP_GROUPS):
        cs = slice(g * GMLP_GROUP_DIM, (g + 1) * GMLP_GROUP_DIM)
        w_g = jnp.where(causal, ws_ref[g], 0.0).astype(BF16)
        b_g = bst_ref[:, g:g + 1]
        for c in range(tm // chunk):
            rs = slice(c * chunk, (c + 1) * chunk)
            mixed = _dot(w_g, vz_ref[rs, cs].astype(BF16)) + b_g
            oa_ref[rs, cs] = (u_ref[rs, cs].astype(F32) * mixed).astype(BF16)
    merged = (ga_ref[...].astype(F32) * _dot(oa_ref[...], wbr_ref[0])
              + gb_ref[...].astype(F32) * _dot(ob_ref[...], wbr_ref[1]))
    h = x_ref[...] + _dot(merged.astype(BF16), wout_ref[...])
    h_ref[...] = h
    hn_ref[...] = _rms(h, nffn_ref[...]).astype(BF16)


def _mix(u, vz, ob, ga, gb, x, ws_chunk, bs_t, w_br_b, w_out_b, norm_ffn, tm):
    m_rows = x.shape[0]
    chunk = ws_chunk.shape[1]
    row = lambda m: (m, 0)
    const2 = lambda m: (0, 0)
    const3 = lambda m: (0, 0, 0)
    once = pl.Buffered(1)
    return pl.pallas_call(
        _mix_kernel,
        grid=(m_rows // tm,),
        in_specs=[
            pl.BlockSpec((tm, D_GMLP), row),
            pl.BlockSpec((tm, D_GMLP), row),
            pl.BlockSpec((tm, D_ATTN), row),
            pl.BlockSpec((tm, D_MODEL), row),
            pl.BlockSpec((tm, D_MODEL), row),
            pl.BlockSpec((tm, D_MODEL), row),
            pl.BlockSpec((GMLP_GROUPS, chunk, chunk), const3, pipeline_mode=once),
            pl.BlockSpec((chunk, GMLP_GROUPS), const2, pipeline_mode=once),
            pl.BlockSpec((2, D_GMLP, D_MODEL), const3, pipeline_mode=once),
            pl.BlockSpec((D_MODEL, D_MODEL), const2, pipeline_mode=once),
            pl.BlockSpec((1, D_MODEL), const2, pipeline_mode=once),
        ],
        out_specs=[pl.BlockSpec((tm, D_MODEL), row), pl.BlockSpec((tm, D_MODEL), row)],
        out_shape=[jax.ShapeDtypeStruct((m_rows, D_MODEL), F32),
                   jax.ShapeDtypeStruct((m_rows, D_MODEL), BF16)],
        scratch_shapes=[pltpu.VMEM((tm, D_GMLP), BF16)],
        compiler_params=_cparams(1),
        name="mix",
    )(u, vz, ob, ga, gb, x, ws_chunk, bs_t, w_br_b, w_out_b, norm_ffn.reshape(1, D_MODEL))


def _ffn_kernel(hn_ref, wa_ref, wb_ref, wc_ref, bc_ref, wd_ref, h_ref, nfin_ref, init_ref,
                y_ref, tail_ref, halo_ref, *, seq_len, tiles_per_seq):
    m = pl.program_id(0)
    n = pl.program_id(1)
    tm = hn_ref.shape[0]
    tf = wa_ref.shape[1]

    @pl.when(n == 0)
    def _():
        y_ref[...] = h_ref[...]

    hn = hn_ref[...]
    a = _dot(hn, wa_ref[...])
    b = _dot(hn, wb_ref[...])
    row = lax.broadcasted_iota(jnp.int32, (tm, tf), 0)
    r1 = pltpu.roll(a, 1, 0)
    r2 = pltpu.roll(a, 2, 0)
    if tiles_per_seq is None:
        pos = row % seq_len
        a1 = jnp.where(pos >= 1, r1, init_ref[0])
        a2 = jnp.where(pos >= 2, r2, init_ref[1])
        tail_ref[...] = a
    else:
        prev = jnp.where(m % tiles_per_seq == 0, init_ref[0], halo_ref[n])
        p1 = prev[SUBLANES - 1:SUBLANES, :]
        p2 = prev[SUBLANES - 2:SUBLANES - 1, :]
        a1 = jnp.where(row == 0, p1, r1)
        a2 = jnp.where(row == 0, p2, jnp.where(row == 1, p1, r2))
        last = a[tm - SUBLANES:, :]
        halo_ref[n] = last
        tail_ref[...] = last
    wc = wc_ref[...]
    conv = bc_ref[...] + a * wc[2:3, :] + a2 * wc[0:1, :] + a1 * wc[1:2, :]
    act = (jax.nn.gelu(conv) * b).astype(BF16)
    y_ref[...] += _dot(act, wd_ref[...])

    @pl.when(n == pl.num_programs(1) - 1)
    def _():
        y_ref[...] = _rms(y_ref[...], nfin_ref[...])


def _ffn(hn, h, w_up_b, w_conv, b_conv, w_down_b, norm_final, init, tm, seq_len):
    m_rows = hn.shape[0]
    tf = FF_TILE
    n_ff = D_FF // tf
    if seq_len >= tm:
        tiles_per_seq = seq_len // tm
        init_spec = pl.BlockSpec((1, SUBLANES, tf), lambda m, n: (0, m // tiles_per_seq, n))
        tail_rows = (m_rows // tm) * SUBLANES
        tail_spec = pl.BlockSpec((SUBLANES, tf), lambda m, n: (m, n))
    else:
        tiles_per_seq = None
        init_spec = pl.BlockSpec((2, tm, tf), lambda m, n: (0, m, n))
        tail_rows = m_rows
        tail_spec = pl.BlockSpec((tm, tf), lambda m, n: (m, n))
    row = lambda m, n: (m, 0)
    once = pl.Buffered(1)
    return pl.pallas_call(
        functools.partial(_ffn_kernel, seq_len=seq_len, tiles_per_seq=tiles_per_seq),
        grid=(m_rows // tm, n_ff),
        in_specs=[
            pl.BlockSpec((tm, D_MODEL), row, pipeline_mode=once),
            pl.BlockSpec((D_MODEL, tf), lambda m, n: (0, n)),
            pl.BlockSpec((D_MODEL, tf), lambda m, n: (0, n_ff + n)),
            pl.BlockSpec((CONV_W, tf), lambda m, n: (0, n)),
            pl.BlockSpec((1, tf), lambda m, n: (0, n)),
            pl.BlockSpec((tf, D_MODEL), lambda m, n: (n, 0)),
            pl.BlockSpec((tm, D_MODEL), row, pipeline_mode=once),
            pl.BlockSpec((1, D_MODEL), lambda m, n: (0, 0)),
            init_spec,
        ],
        out_specs=[pl.BlockSpec((tm, D_MODEL), row), tail_spec],
        out_shape=[jax.ShapeDtypeStruct((m_rows, D_MODEL), F32),
                   jax.ShapeDtypeStruct((tail_rows, D_FF), F32)],
        scratch_shapes=[pltpu.VMEM((n_ff, SUBLANES, tf), F32)],
        compiler_params=_cparams(2),
        name="ffn",
    )(hn, w_up_b, w_up_b, w_conv, b_conv.reshape(1, D_FF), w_down_b, h,
      norm_final.reshape(1, D_MODEL), init)


def _alibi_slopes():
    return jnp.exp2(-8.0 * jnp.arange(1, N_HEADS + 1, dtype=F32) / N_HEADS)


def kernel(x_prompt, x_sample, cache_k, cache_v, state_ffn_conv, page_table, norm_mix, w_in,
           norm_gmlp_v, w_spatial, b_spatial, w_branch, w_out, norm_ffn, w_up, w_conv, b_conv,
           w_down, norm_final):
    assert w_in.shape[0] == 1, "single layer"
    bp, seq, _ = x_prompt.shape
    bs, t_q, _ = x_sample.shape
    n_pages = page_table.shape[1]
    past_len = n_pages * PAGE_SIZE
    assert seq % MOBA_BLOCK == 0 and past_len % MOBA_BLOCK == 0 and past_len % CHUNK == 0
    assert t_q <= SUBLANES and past_len // MOBA_BLOCK <= LANES and seq // MOBA_BLOCK <= LANES

    slopes = _alibi_slopes()
    w_br_b = w_branch[0].astype(BF16)
    w_out_b = w_out[0].astype(BF16)
    w_up_b = w_up[0].astype(BF16)
    w_down_b = w_down[0].astype(BF16)
    ws, bsp = w_spatial[0], b_spatial[0]

    xp = x_prompt.reshape(bp * seq, D_MODEL)
    xn_p = _norm_bf16(xp, norm_mix[0], PROMPT_TM_NORM)
    u, vz, q, k, kb, v, vb, ga, gb = _inproj(xn_p, w_in[0], norm_gmlp_v[0], PROMPT_TM_INPROJ)
    ob = _moba_prompt(q, kb, vb, slopes, bp, seq)
    h, hn = _mix(u, vz, ob, ga, gb, xp, ws, bsp.T, w_br_b, w_out_b, norm_ffn[0], PROMPT_TM_MIX)
    zero_state = jnp.zeros((1, bp * SUBLANES, D_FF), F32)
    tm_p = PROMPT_TM_FFN
    yp, tail_p = _ffn(hn, h, w_up_b, w_conv[0], b_conv[0], w_down_b, norm_final, zero_state,
                      tm_p, seq)
    y_prompt = yp.reshape(bp, seq, D_MODEL)
    gv_p = vz.reshape(bp, seq, D_GMLP)[:, seq - CHUNK:][None]
    k_p = k.reshape(1, bp, seq, N_HEADS, HEAD_DIM)
    v_p = v.reshape(1, bp, seq, N_HEADS, HEAD_DIM)
    c_p = tail_p.reshape(bp, seq // tm_p, SUBLANES, D_FF)[:, -1, SUBLANES - (CONV_W - 1):][None]

    m_s = bs * t_q
    xs = x_sample.reshape(m_s, D_MODEL)
    xn_s = _norm_bf16(xs, norm_mix[0], m_s)
    us, vzs, qs, ks, kbs, vs, vbs, gas, gbs = _inproj(xn_s, w_in[0], norm_gmlp_v[0], m_s)
    q4 = qs.reshape(bs, t_q, N_HEADS, HEAD_DIM)
    eye = jnp.eye(N_HEADS, dtype=BF16)
    qbd = (q4.transpose(0, 2, 1, 3)[:, :, :, None, :] * eye[None, :, None, :, None]
           ).reshape(bs, N_HEADS * t_q, D_ATTN)
    pad_rows = ((0, 0), (0, LANES - t_q), (0, 0))
    knew = jnp.pad(kbs.reshape(bs, t_q, D_ATTN), pad_rows)
    vnew = jnp.pad(vbs.reshape(bs, t_q, D_ATTN), pad_rows)
    slope_rows = jnp.repeat(slopes, t_q).reshape(N_HEADS * t_q, 1)
    ck = cache_k[0].reshape(-1, PAGE_SIZE * N_HEADS, HEAD_DIM)
    cv = cache_v[0].reshape(-1, PAGE_SIZE * N_HEADS, HEAD_DIM)
    p, pown, l = _sample_scores(page_table, qbd, knew, slope_rows, ck, t_q, past_len)
    obs = _sample_pv(page_table, p, pown, l, vnew, cv, t_q).reshape(m_s, D_ATTN)
    ws_s = (jnp.eye(bs, dtype=F32)[None, :, None, :, None]
            * ws[:, None, :t_q, None, :t_q]).reshape(GMLP_GROUPS, m_s, m_s)
    bs_t_s = jnp.tile(bsp[:, :t_q], (1, bs)).T
    hs, hns = _mix(us, vzs, obs, gas, gbs, xs, ws_s, bs_t_s, w_br_b, w_out_b, norm_ffn[0], m_s)
    st = state_ffn_conv[0]
    zero_rows = jnp.zeros((bs, t_q - 1, D_FF), F32)
    init1 = jnp.concatenate([st[:, 1:2], zero_rows], axis=1)
    init2 = jnp.concatenate([st[:, 0:1], st[:, 1:2], zero_rows[:, 1:]], axis=1)
    init = jnp.stack([init1.reshape(m_s, D_FF), init2.reshape(m_s, D_FF)])
    ys, tail_s = _ffn(hns, hs, w_up_b, w_conv[0], b_conv[0], w_down_b, norm_final, init, m_s, t_q)
    y_sample = ys.reshape(bs, t_q, D_MODEL)
    gv_s = vzs.reshape(1, bs, t_q, D_GMLP)
    k_s = ks.reshape(1, bs, t_q, N_HEADS, HEAD_DIM)
    v_s = vs.reshape(1, bs, t_q, N_HEADS, HEAD_DIM)
    c_s = tail_s.reshape(bs, t_q, D_FF)[:, t_q - (CONV_W - 1):][None]

    return (y_prompt, y_sample, gv_p, gv_s, k_p, v_p, k_s, v_s, c_p, c_s)
```

```python
import functools

import jax
import jax.numpy as jnp
from jax import lax
from jax.experimental import pallas as pl
from jax.experimental.pallas import tpu as pltpu

F32 = jnp.float32
BF16 = jnp.bfloat16

D_MODEL = 2048
D_GMLP = D_MODEL // 2
GMLP_GROUPS = 8
GMLP_GROUP_DIM = D_GMLP // GMLP_GROUPS
CHUNK = 128
D_ATTN = D_MODEL // 2
HEAD_DIM = 128
N_HEADS = D_ATTN // HEAD_DIM
MOBA_BLOCK = 256
MOBA_TOPK = 3
D_FF = 5632
CONV_W = 3
RMS_EPS = 1e-6
PAGE_SIZE = 128
N_SECTIONS = 9

LANES = 128
SUBLANES = 8
VMEM_LIMIT_BYTES = 56 * 1024 * 1024

NEG_BIG = -1e30
LOG2E = 1.4426950408889634
MOBA_HEADS_PER_STEP = 4
MOBA_BLOCKS_PER_TRIP = 4
MOBA_ONES_ROWS = 16
SAMPLE_CHUNK_BLOCKS = 8
FF_TILE = 512
PAGES_PER_STEP = 16
PROMPT_TM_NORM = 1024
PROMPT_TM_INPROJ = 512
PROMPT_TM_MIX = 256
PROMPT_TM_FFN = 1024

_NT = (((1,), (1,)), ((), ()))


def _dot(a, b):
    return jnp.dot(a, b, preferred_element_type=F32)


def _dot_nt(a, b):
    return lax.dot_general(a, b, _NT, preferred_element_type=F32)


def _rms(x, g):
    return x * lax.rsqrt(jnp.mean(x * x, axis=-1, keepdims=True) + RMS_EPS) * g


def _sigmoid(x):
    return 0.5 * (jnp.tanh(0.5 * x) + 1.0)


def _cparams(n_axes):
    return pltpu.CompilerParams(dimension_semantics=("arbitrary",) * n_axes,
                                vmem_limit_bytes=VMEM_LIMIT_BYTES)


def _top3_mask(gate, idx_f, axis=1):
    sel = jnp.zeros_like(gate)
    for _ in range(MOBA_TOPK):
        mx = jnp.max(gate, axis=axis, keepdims=True)
        first = jnp.min(jnp.where(gate == mx, idx_f, float(LANES)), axis=axis, keepdims=True)
        pick = (idx_f == first) & (mx > -jnp.inf)
        sel = jnp.where(pick, 1.0, sel)
        gate = jnp.where(pick, -jnp.inf, gate)
    return sel


def _store_heads(dst_ref, dst_b_ref, acc):
    tm = acc.shape[0]
    dst_b_ref[...] = acc.astype(BF16)
    for h in range(N_HEADS):
        dst_ref[pl.ds(h, tm, stride=N_HEADS), :] = acc[:, h * HEAD_DIM:(h + 1) * HEAD_DIM]


def _norm_kernel(x_ref, g_ref, o_ref):
    o_ref[...] = _rms(x_ref[...], g_ref[...]).astype(BF16)


def _norm_bf16(x, g, tm):
    m_rows, d = x.shape
    return pl.pallas_call(
        _norm_kernel,
        grid=(m_rows // tm,),
        in_specs=[pl.BlockSpec((tm, d), lambda m: (m, 0)), pl.BlockSpec((1, d), lambda m: (0, 0))],
        out_specs=pl.BlockSpec((tm, d), lambda m: (m, 0)),
        out_shape=jax.ShapeDtypeStruct((m_rows, d), BF16),
        compiler_params=_cparams(1),
        name="norm",
    )(x, g.reshape(1, d))


def _inproj_kernel(xn_ref, w_ref, ngv_ref, u_ref, vz_ref, q_ref, k_ref, kb_ref, v_ref, vb_ref,
                   ga_ref, gb_ref, wb_ref):
    n = pl.program_id(0)

    @pl.when(pl.program_id(1) == 0)
    def _():
        wb_ref[...] = w_ref[...].astype(BF16)

    def section():
        return _dot(xn_ref[...], wb_ref[...])

    @pl.when(n == 0)
    def _():
        u_ref[...] = jax.nn.gelu(section()).astype(BF16)

    @pl.when(n == 1)
    def _():
        vz_ref[...] = _rms(jax.nn.gelu(section()), ngv_ref[...])

    @pl.when(n == 2)
    def _():
        q_ref[...] = section().astype(BF16)

    @pl.when(n == 3)
    def _():
        _store_heads(k_ref, kb_ref, section())

    @pl.when(n == 4)
    def _():
        _store_heads(v_ref, vb_ref, section())

    @pl.when((n == 5) | (n == 6))
    def _():
        ga_ref[...] = _sigmoid(section()).astype(BF16)

    @pl.when(n >= 7)
    def _():
        gb_ref[...] = _sigmoid(section()).astype(BF16)


def _inproj(xn, w_in, norm_gmlp_v, tm):
    m_rows = xn.shape[0]
    sec = D_GMLP
    last = m_rows // tm - 1

    def rows(first, count):
        def index(n, m):
            return jnp.where(n < first, 0, jnp.where(n >= first + count, last, m))
        return index

    def out(first, count=1, block_rows=tm):
        r = rows(first, count)
        return pl.BlockSpec((block_rows, sec), lambda n, m: (r(n, m), jnp.clip(n - first, 0, count - 1)))

    in_specs = [
        pl.BlockSpec((tm, D_MODEL), lambda n, m: (m, 0)),
        pl.BlockSpec((D_MODEL, sec), lambda n, m: (0, n)),
        pl.BlockSpec((1, sec), lambda n, m: (0, 0)),
    ]
    out_specs = [
        out(0),
        out(1),
        out(2),
        pl.BlockSpec((tm * N_HEADS, HEAD_DIM), lambda n, m: (rows(3, 1)(n, m), 0)),
        out(3),
        pl.BlockSpec((tm * N_HEADS, HEAD_DIM), lambda n, m: (rows(4, 1)(n, m), 0)),
        out(4),
        out(5, 2),
        out(7, 2),
    ]
    out_shape = [
        jax.ShapeDtypeStruct((m_rows, sec), BF16),
        jax.ShapeDtypeStruct((m_rows, sec), F32),
        jax.ShapeDtypeStruct((m_rows, sec), BF16),
        jax.ShapeDtypeStruct((m_rows * N_HEADS, HEAD_DIM), F32),
        jax.ShapeDtypeStruct((m_rows, sec), BF16),
        jax.ShapeDtypeStruct((m_rows * N_HEADS, HEAD_DIM), F32),
        jax.ShapeDtypeStruct((m_rows, sec), BF16),
        jax.ShapeDtypeStruct((m_rows, D_MODEL), BF16),
        jax.ShapeDtypeStruct((m_rows, D_MODEL), BF16),
    ]
    return pl.pallas_call(
        _inproj_kernel,
        grid=(N_SECTIONS, m_rows // tm),
        in_specs=in_specs,
        out_specs=out_specs,
        out_shape=out_shape,
        scratch_shapes=[pltpu.VMEM((D_MODEL, sec), BF16)],
        compiler_params=_cparams(2),
        name="inproj",
    )(xn, w_in, norm_gmlp_v.reshape(1, sec))


def _moba_prompt_kernel(slopes_ref, q_ref, kb_ref, vb_ref, o_ref, km_ref, vt_ref, bias_ref, sel_ref,
                        m_ref, t_ref, acc_ref, *, n_blocks):
    hg = pl.program_id(1)
    i = pl.program_id(2)
    blk = MOBA_BLOCK
    c1 = (HEAD_DIM ** -0.5) * LOG2E
    heads = [(hh, slice(hh * HEAD_DIM, (hh + 1) * HEAD_DIM)) for hh in range(MOBA_HEADS_PER_STEP)]

    def slope2(hh):
        return slopes_ref[hg * MOBA_HEADS_PER_STEP + hh] * LOG2E

    @pl.when(i == 0)
    def _():
        key_f = lax.broadcasted_iota(jnp.int32, (blk, blk), 0).astype(F32)
        for hh, cs in heads:
            bias_ref[hh] = slope2(hh) * key_f
            vt_ref[hh, HEAD_DIM:, :] = jnp.ones((MOBA_ONES_ROWS, n_blocks * blk), BF16)
            for j in range(n_blocks):
                rs = slice(j * blk, (j + 1) * blk)
                km_ref[hh, j:j + 1, :] = jnp.sum(kb_ref[rs, cs].astype(F32), axis=0,
                                                 keepdims=True) * (1.0 / blk)
                vt_ref[hh, :HEAD_DIM, rs] = vb_ref[rs, cs].astype(F32).T.astype(BF16)

    blk_i = lax.broadcasted_iota(jnp.int32, (n_blocks, blk), 0)
    causal = (lax.broadcasted_iota(jnp.int32, (blk, blk), 1)
              >= lax.broadcasted_iota(jnp.int32, (blk, blk), 0))
    start = pl.multiple_of(i * blk, blk)

    for hh, cs in heads:
        q = q_ref[:, cs]
        km = km_ref[hh]
        km_hi = km.astype(BF16)
        km_lo = (km - km_hi.astype(F32)).astype(BF16)
        gate = _dot_nt(km_hi, q) + _dot_nt(km_lo, q)
        gate = jnp.where(blk_i < i, gate, -jnp.inf)
        sel_ref[hh] = _top3_mask(gate, blk_i.astype(F32), axis=0)
        t = _dot_nt(kb_ref[pl.ds(start, blk), cs], q) * c1 + bias_ref[hh]
        t = jnp.where(causal, t, NEG_BIG)
        t_ref[hh, i] = t
        m_ref[hh] = jnp.max(t, axis=0, keepdims=True)

    def shift(hh, j):
        return slope2(hh) * lax.convert_element_type((i - j) * blk, F32)

    def pass1(j0, nb):
        for hh, cs in heads:
            m = m_ref[hh]
            for j in [j0 + d for d in range(nb)]:
                off = pl.multiple_of(j * blk, blk)
                t = _dot_nt(kb_ref[pl.ds(off, blk), cs], q_ref[:, cs]) * c1 + bias_ref[hh]
                t_ref[hh, j] = t
                picked = sel_ref[hh, pl.ds(j, 1), :] > 0.0
                m_blk = jnp.max(t, axis=0, keepdims=True) - shift(hh, j)
                m = jnp.where(picked, jnp.maximum(m, m_blk), m)
            m_ref[hh] = m

    def pass2(j0, nb):
        off = pl.multiple_of(j0 * blk, blk)
        for hh, cs in heads:
            ps = []
            for j in [j0 + d for d in range(nb)]:
                picked = sel_ref[hh, pl.ds(j, 1), :] > 0.0
                sub = jnp.where(picked, m_ref[hh] + shift(hh, j), -NEG_BIG)
                ps.append(jnp.exp2(t_ref[hh, j] - sub).astype(BF16))
            p = ps[0] if nb == 1 else jnp.concatenate(ps, axis=0)
            acc_ref[hh] += _dot(vt_ref[hh, :, pl.ds(off, nb * blk)], p)

    def over_past_blocks(fn):
        done = 0
        un = MOBA_BLOCKS_PER_TRIP
        while un >= 1:
            n_trips = (i - done) // un

            def trip(t, carry, un=un, base=done):
                fn(base + t * un, un)
                return carry

            lax.fori_loop(0, n_trips, trip, 0)
            done = done + n_trips * un
            un //= 2

    over_past_blocks(pass1)
    for hh, cs in heads:
        p = jnp.exp2(t_ref[hh, i] - m_ref[hh])
        acc_ref[hh] = _dot(vt_ref[hh, :, pl.ds(start, blk)], p.astype(BF16))
    over_past_blocks(pass2)
    for hh, cs in heads:
        acc = acc_ref[hh]
        o_ref[:, cs] = (acc[:HEAD_DIM] / acc[HEAD_DIM:HEAD_DIM + 1]).T.astype(BF16)


def _moba_prompt(q, kb, vb, slopes, bsz, seq):
    n_blocks = seq // MOBA_BLOCK
    blk = MOBA_BLOCK
    hps = MOBA_HEADS_PER_STEP
    width = hps * HEAD_DIM
    qmap = lambda b, hg, i: (b * n_blocks + i, hg)
    kvmap = lambda b, hg, i: (b, hg)
    return pl.pallas_call(
        functools.partial(_moba_prompt_kernel, n_blocks=n_blocks),
        grid=(bsz, N_HEADS // hps, n_blocks),
        in_specs=[
            pl.BlockSpec(memory_space=pltpu.SMEM),
            pl.BlockSpec((blk, width), qmap),
            pl.BlockSpec((seq, width), kvmap),
            pl.BlockSpec((seq, width), kvmap),
        ],
        out_specs=pl.BlockSpec((blk, width), qmap),
        out_shape=jax.ShapeDtypeStruct((bsz * seq, D_ATTN), BF16),
        scratch_shapes=[
            pltpu.VMEM((hps, n_blocks, HEAD_DIM), F32),
            pltpu.VMEM((hps, HEAD_DIM + MOBA_ONES_ROWS, seq), BF16),
            pltpu.VMEM((hps, blk, blk), F32),
            pltpu.VMEM((hps, n_blocks, blk), F32),
            pltpu.VMEM((hps, 1, blk), F32),
            pltpu.VMEM((hps, n_blocks, blk, blk), F32),
            pltpu.VMEM((hps, HEAD_DIM + MOBA_ONES_ROWS, blk), F32),
        ],
        compiler_params=_cparams(3),
        name="moba_prompt",
    )(slopes, q, kb, vb)


def _load_page(page_ref):
    return jnp.concatenate(
        [page_ref[pl.ds(h, PAGE_SIZE, stride=N_HEADS), :].astype(BF16) for h in range(N_HEADS)],
        axis=1)


def _page_specs():
    rows = PAGE_SIZE * N_HEADS
    return [
        pl.BlockSpec((None, rows, HEAD_DIM),
                     functools.partial(lambda b, g, pt, pp: (pt[b, g * PAGES_PER_STEP + pp], 0, 0), pp=pp))
        for pp in range(PAGES_PER_STEP)
    ]


def _sample_scores_kernel(pt_ref, qbd_ref, knew_ref, slope_ref, *rest, t_q, past_len):
    del pt_ref
    k_refs = rest[:PAGES_PER_STEP]
    p_ref, pown_ref, l_ref, km_ref, ks_ref = rest[PAGES_PER_STEP:]
    g = pl.program_id(1)
    n_steps = pl.num_programs(1)
    rows = qbd_ref.shape[0]
    blk = MOBA_BLOCK
    pages_per_block = blk // PAGE_SIZE
    n_past_blocks = past_len // blk
    qbd = qbd_ref[...]
    lane = lax.broadcasted_iota(jnp.int32, (rows, LANES), 1)
    lane_f = lane.astype(F32)

    @pl.when(g == 0)
    def _():
        km_ref[n_past_blocks:, :] = jnp.zeros((LANES - n_past_blocks, D_ATTN), F32)

    ksum = None
    for pp in range(PAGES_PER_STEP):
        lg = _dot_nt(qbd, _load_page(k_refs[pp]))
        off = pl.multiple_of((g * PAGES_PER_STEP + pp) * PAGE_SIZE, PAGE_SIZE)
        p_ref[:, pl.ds(off, PAGE_SIZE)] = lg
        psum = jnp.sum(k_refs[pp][...].reshape(PAGE_SIZE, N_HEADS, HEAD_DIM), axis=0)
        ksum = psum if pp % pages_per_block == 0 else ksum + psum
        if pp % pages_per_block == pages_per_block - 1:
            bb = pp // pages_per_block
            ks_ref[bb * N_HEADS:(bb + 1) * N_HEADS, :] = ksum * (1.0 / blk)
    step_blocks = PAGES_PER_STEP // pages_per_block
    km_rows = jnp.concatenate(
        [ks_ref[pl.ds(h, step_blocks, stride=N_HEADS), :] for h in range(N_HEADS)], axis=1)
    km_ref[pl.ds(pl.multiple_of(g * step_blocks, step_blocks), step_blocks), :] = km_rows

    @pl.when(g == n_steps - 1)
    def _():
        gate = _dot_nt(qbd, km_ref[...].astype(BF16))
        c1 = (HEAD_DIM ** -0.5) * LOG2E
        slope2 = slope_ref[...] * LOG2E
        t_row = (lax.broadcasted_iota(jnp.int32, (rows, 1), 0) % t_q).astype(F32)
        sel = _top3_mask(jnp.where(lane < n_past_blocks, gate, -jnp.inf), lane_f)

        s_own = _dot_nt(qbd, knew_ref[...]) * c1 + slope2 * lane_f
        s_own = jnp.where(lane_f <= t_row, s_own, NEG_BIG)

        cb = SAMPLE_CHUNK_BLOCKS
        ch = cb * blk
        n_chunks = n_past_blocks // cb
        assert cb & (cb - 1) == 0 and blk & (blk - 1) == 0
        widen = jnp.where(
            jnp.bitwise_and(lax.broadcasted_iota(jnp.int32, (LANES, ch), 0), cb - 1)
            == jnp.right_shift(lax.broadcasted_iota(jnp.int32, (LANES, ch), 1), blk.bit_length() - 1),
            1.0, 0.0).astype(BF16)
        key_f = lax.broadcasted_iota(jnp.int32, (1, ch), 1).astype(F32)

        mx = jnp.full((rows, blk), NEG_BIG, F32)
        for c in range(n_chunks):
            sel_c = jnp.where((lane >= c * cb) & (lane < (c + 1) * cb), sel, 0.0).astype(BF16)
            picked = _dot(sel_c, widen)
            s = p_ref[:, c * ch:(c + 1) * ch] * c1 + slope2 * (key_f + float(c * ch - past_len))
            s = jnp.where(picked > 0.0, s, NEG_BIG)
            p_ref[:, c * ch:(c + 1) * ch] = s
            for b in range(cb):
                mx = jnp.maximum(mx, s[:, b * blk:(b + 1) * blk])
        m = jnp.maximum(jnp.max(mx, axis=1, keepdims=True), jnp.max(s_own, axis=1, keepdims=True))

        p_own = jnp.exp2(s_own - m)
        pown_ref[...] = p_own
        acc = jnp.zeros((rows, blk), F32)
        for c in range(n_chunks):
            pc = jnp.exp2(p_ref[:, c * ch:(c + 1) * ch] - m)
            p_ref[:, c * ch:(c + 1) * ch] = pc
            for b in range(cb):
                acc = acc + pc[:, b * blk:(b + 1) * blk]
        l = jnp.sum(acc, axis=1, keepdims=True) + jnp.sum(p_own, axis=1, keepdims=True)
        l_ref[...] = jnp.broadcast_to(l, l_ref.shape)


def _sample_scores(page_table, qbd, knew_pad, slope_rows, cache_k_pages, t_q, past_len):
    bsz, rows, _ = qbd.shape
    n_pages = page_table.shape[1]
    n_steps = n_pages // PAGES_PER_STEP
    grid_spec = pltpu.PrefetchScalarGridSpec(
        num_scalar_prefetch=1,
        grid=(bsz, n_steps),
        in_specs=[
            pl.BlockSpec((None, rows, D_ATTN), lambda b, g, pt: (b, 0, 0)),
            pl.BlockSpec((None, LANES, D_ATTN), lambda b, g, pt: (b, 0, 0)),
            pl.BlockSpec((rows, 1), lambda b, g, pt: (0, 0)),
        ] + _page_specs(),
        out_specs=[
            pl.BlockSpec((None, rows, past_len), lambda b, g, pt: (b, 0, 0)),
            pl.BlockSpec((None, rows, LANES), lambda b, g, pt: (b, 0, 0)),
            pl.BlockSpec((None, rows, LANES), lambda b, g, pt: (b, 0, 0)),
        ],
        scratch_shapes=[
            pltpu.VMEM((LANES, D_ATTN), F32),
            pltpu.VMEM((PAGES_PER_STEP * PAGE_SIZE // MOBA_BLOCK * N_HEADS, HEAD_DIM), F32),
        ],
    )
    return pl.pallas_call(
        functools.partial(_sample_scores_kernel, t_q=t_q, past_len=past_len),
        grid_spec=grid_spec,
        out_shape=[
            jax.ShapeDtypeStruct((bsz, rows, past_len), F32),
            jax.ShapeDtypeStruct((bsz, rows, LANES), F32),
            jax.ShapeDtypeStruct((bsz, rows, LANES), F32),
        ],
        compiler_params=_cparams(2),
        name="sample_scores",
    )(page_table, qbd, knew_pad, slope_rows, *([cache_k_pages] * PAGES_PER_STEP))


def _sample_pv_kernel(pt_ref, p_ref, pown_ref, l_ref, vnew_ref, *rest, t_q):
    del pt_ref
    v_refs = rest[:PAGES_PER_STEP]
    o_ref, acc_ref = rest[PAGES_PER_STEP:]
    g = pl.program_id(1)
    n_steps = pl.num_programs(1)

    @pl.when(g == 0)
    def _():
        acc_ref[...] = _dot(pown_ref[...].astype(BF16), vnew_ref[...])

    acc = acc_ref[...]
    for pp in range(PAGES_PER_STEP):
        acc = acc + _dot(p_ref[:, pp * PAGE_SIZE:(pp + 1) * PAGE_SIZE].astype(BF16),
                         _load_page(v_refs[pp]))
    acc_ref[...] = acc

    @pl.when(g == n_steps - 1)
    def _():
        for h in range(N_HEADS):
            rs = slice(h * t_q, (h + 1) * t_q)
            cs = slice(h * HEAD_DIM, (h + 1) * HEAD_DIM)
            o_ref[:, cs] = (acc_ref[rs, cs] / l_ref[rs, 0:1]).astype(BF16)


def _sample_pv(page_table, p, pown, l, vnew_pad, cache_v_pages, t_q):
    bsz, rows, _ = p.shape
    n_pages = page_table.shape[1]
    n_steps = n_pages // PAGES_PER_STEP
    step_keys = PAGES_PER_STEP * PAGE_SIZE
    grid_spec = pltpu.PrefetchScalarGridSpec(
        num_scalar_prefetch=1,
        grid=(bsz, n_steps),
        in_specs=[
            pl.BlockSpec((None, rows, step_keys), lambda b, g, pt: (b, 0, g)),
            pl.BlockSpec((None, rows, LANES), lambda b, g, pt: (b, 0, 0)),
            pl.BlockSpec((None, rows, LANES), lambda b, g, pt: (b, 0, 0)),
            pl.BlockSpec((None, LANES, D_ATTN), lambda b, g, pt: (b, 0, 0)),
        ] + _page_specs(),
        out_specs=pl.BlockSpec((None, t_q, D_ATTN), lambda b, g, pt: (b, 0, 0)),
        scratch_shapes=[pltpu.VMEM((rows, D_ATTN), F32)],
    )
    return pl.pallas_call(
        functools.partial(_sample_pv_kernel, t_q=t_q),
        grid_spec=grid_spec,
        out_shape=jax.ShapeDtypeStruct((bsz, t_q, D_ATTN), BF16),
        compiler_params=_cparams(2),
        name="sample_pv",
    )(page_table, p, pown, l, vnew_pad, *([cache_v_pages] * PAGES_PER_STEP))


def _mix_kernel(u_ref, vz_ref, ob_ref, ga_ref, gb_ref, x_ref, ws_ref, bst_ref, wbr_ref, wout_ref,
                nffn_ref, h_ref, hn_ref, oa_ref):
    tm = u_ref.shape[0]
    chunk = ws_ref.shape[1]
    causal = (lax.broadcasted_iota(jnp.int32, (chunk, chunk), 0)
              >= lax.broadcasted_iota(jnp.int32, (chunk, chunk), 1))
    for g in range(GMLP_GROUPS):
        cs = slice(g * GMLP_GROUP_DIM, (g + 1) * GMLP_GROUP_DIM)
        w_g = jnp.where(causal, ws_ref[g], 0.0).astype(BF16)
        b_g = bst_ref[:, g:g + 1]
        for c in range(tm // chunk):
            rs = slice(c * chunk, (c + 1) * chunk)
            mixed = _dot(w_g, vz_ref[rs, cs].astype(BF16)) + b_g
            oa_ref[rs, cs] = (u_ref[rs, cs].astype(F32) * mixed).astype(BF16)
    merged = (ga_ref[...].astype(F32) * _dot(oa_ref[...], wbr_ref[0])
              + gb_ref[...].astype(F32) * _dot(ob_ref[...], wbr_ref[1]))
    h = x_ref[...] + _dot(merged.astype(BF16), wout_ref[...])
    h_ref[...] = h
    hn_ref[...] = _rms(h, nffn_ref[...]).astype(BF16)


def _mix(u, vz, ob, ga, gb, x, ws_chunk, bs_t, w_br_b, w_out_b, norm_ffn, tm):
    m_rows = x.shape[0]
    chunk = ws_chunk.shape[1]
    row = lambda m: (m, 0)
    const2 = lambda m: (0, 0)
    const3 = lambda m: (0, 0, 0)
    once = pl.Buffered(1)
    return pl.pallas_call(
        _mix_kernel,
        grid=(m_rows // tm,),
        in_specs=[
            pl.BlockSpec((tm, D_GMLP), row),
            pl.BlockSpec((tm, D_GMLP), row),
            pl.BlockSpec((tm, D_ATTN), row),
            pl.BlockSpec((tm, D_MODEL), row),
            pl.BlockSpec((tm, D_MODEL), row),
            pl.BlockSpec((tm, D_MODEL), row),
            pl.BlockSpec((GMLP_GROUPS, chunk, chunk), const3, pipeline_mode=once),
            pl.BlockSpec((chunk, GMLP_GROUPS), const2, pipeline_mode=once),
            pl.BlockSpec((2, D_GMLP, D_MODEL), const3, pipeline_mode=once),
            pl.BlockSpec((D_MODEL, D_MODEL), const2, pipeline_mode=once),
            pl.BlockSpec((1, D_MODEL), const2, pipeline_mode=once),
        ],
        out_specs=[pl.BlockSpec((tm, D_MODEL), row), pl.BlockSpec((tm, D_MODEL), row)],
        out_shape=[jax.ShapeDtypeStruct((m_rows, D_MODEL), F32),
                   jax.ShapeDtypeStruct((m_rows, D_MODEL), BF16)],
        scratch_shapes=[pltpu.VMEM((tm, D_GMLP), BF16)],
        compiler_params=_cparams(1),
        name="mix",
    )(u, vz, ob, ga, gb, x, ws_chunk, bs_t, w_br_b, w_out_b, norm_ffn.reshape(1, D_MODEL))


def _ffn_kernel(hn_ref, wa_ref, wb_ref, wc_ref, bc_ref, wd_ref, h_ref, nfin_ref, init_ref,
                y_ref, tail_ref, halo_ref, *, seq_len, tiles_per_seq):
    m = pl.program_id(0)
    n = pl.program_id(1)
    tm = hn_ref.shape[0]
    tf = wa_ref.shape[1]

    @pl.when(n == 0)
    def _():
        y_ref[...] = h_ref[...]

    hn = hn_ref[...]
    a = _dot(hn, wa_ref[...].astype(BF16))
    b = _dot(hn, wb_ref[...].astype(BF16))
    row = lax.broadcasted_iota(jnp.int32, (tm, tf), 0)
    r1 = pltpu.roll(a, 1, 0)
    r2 = pltpu.roll(a, 2, 0)
    if tiles_per_seq is None:
        pos = row % seq_len
        a1 = jnp.where(pos >= 1, r1, init_ref[0])
        a2 = jnp.where(pos >= 2, r2, init_ref[1])
        tail_ref[...] = a
    else:
        prev = jnp.where(m % tiles_per_seq == 0, init_ref[0], halo_ref[n])
        p1 = prev[SUBLANES - 1:SUBLANES, :]
        p2 = prev[SUBLANES - 2:SUBLANES - 1, :]
        a1 = jnp.where(row == 0, p1, r1)
        a2 = jnp.where(row == 0, p2, jnp.where(row == 1, p1, r2))
        last = a[tm - SUBLANES:, :]
        halo_ref[n] = last
        tail_ref[...] = last
    wc = wc_ref[...]
    conv = bc_ref[...] + a * wc[2:3, :] + a2 * wc[0:1, :] + a1 * wc[1:2, :]
    act = (jax.nn.gelu(conv) * b).astype(BF16)
    y_ref[...] += _dot(act, wd_ref[...].astype(BF16))

    @pl.when(n == pl.num_programs(1) - 1)
    def _():
        y_ref[...] = _rms(y_ref[...], nfin_ref[...])


def _ffn(hn, h, w_up, w_conv, b_conv, w_down, norm_final, init, tm, seq_len):
    m_rows = hn.shape[0]
    tf = FF_TILE
    n_ff = D_FF // tf
    if seq_len >= tm:
        tiles_per_seq = seq_len // tm
        init_spec = pl.BlockSpec((1, SUBLANES, tf), lambda m, n: (0, m // tiles_per_seq, n))
        tail_rows = (m_rows // tm) * SUBLANES
        tail_spec = pl.BlockSpec((SUBLANES, tf), lambda m, n: (m, n))
    else:
        tiles_per_seq = None
        init_spec = pl.BlockSpec((2, tm, tf), lambda m, n: (0, m, n))
        tail_rows = m_rows
        tail_spec = pl.BlockSpec((tm, tf), lambda m, n: (m, n))
    row = lambda m, n: (m, 0)
    once = pl.Buffered(1)
    return pl.pallas_call(
        functools.partial(_ffn_kernel, seq_len=seq_len, tiles_per_seq=tiles_per_seq),
        grid=(m_rows // tm, n_ff),
        in_specs=[
            pl.BlockSpec((tm, D_MODEL), row, pipeline_mode=once),
            pl.BlockSpec((D_MODEL, tf), lambda m, n: (0, n)),
            pl.BlockSpec((D_MODEL, tf), lambda m, n: (0, n_ff + n)),
            pl.BlockSpec((CONV_W, tf), lambda m, n: (0, n)),
            pl.BlockSpec((1, tf), lambda m, n: (0, n)),
            pl.BlockSpec((tf, D_MODEL), lambda m, n: (n, 0)),
            pl.BlockSpec((tm, D_MODEL), row, pipeline_mode=once),
            pl.BlockSpec((1, D_MODEL), lambda m, n: (0, 0)),
            init_spec,
        ],
        out_specs=[pl.BlockSpec((tm, D_MODEL), row, pipeline_mode=once), tail_spec],
        out_shape=[jax.ShapeDtypeStruct((m_rows, D_MODEL), F32),
                   jax.ShapeDtypeStruct((tail_rows, D_FF), F32)],
        scratch_shapes=[pltpu.VMEM((n_ff, SUBLANES, tf), F32)],
        compiler_params=_cparams(2),
        name="ffn",
    )(hn, w_up, w_up, w_conv, b_conv.reshape(1, D_FF), w_down, h,
      norm_final.reshape(1, D_MODEL), init)


def _alibi_slopes():
    return jnp.exp2(-8.0 * jnp.arange(1, N_HEADS + 1, dtype=F32) / N_HEADS)


def kernel(x_prompt, x_sample, cache_k, cache_v, state_ffn_conv, page_table, norm_mix, w_in,
           norm_gmlp_v, w_spatial, b_spatial, w_branch, w_out, norm_ffn, w_up, w_conv, b_conv,
           w_down, norm_final):
    assert w_in.shape[0] == 1, "single layer"
    bp, seq, _ = x_prompt.shape
    bs, t_q, _ = x_sample.shape
    n_pages = page_table.shape[1]
    past_len = n_pages * PAGE_SIZE
    assert seq % MOBA_BLOCK == 0 and past_len % MOBA_BLOCK == 0 and past_len % CHUNK == 0
    assert t_q <= SUBLANES and past_len // MOBA_BLOCK <= LANES and seq // MOBA_BLOCK <= LANES

    slopes = _alibi_slopes()
    w_br_b = w_branch[0].astype(BF16)
    w_out_b = w_out[0].astype(BF16)
    ws, bsp = w_spatial[0], b_spatial[0]

    xp = x_prompt.reshape(bp * seq, D_MODEL)
    xn_p = _norm_bf16(xp, norm_mix[0], PROMPT_TM_NORM)
    u, vz, q, k, kb, v, vb, ga, gb = _inproj(xn_p, w_in[0], norm_gmlp_v[0], PROMPT_TM_INPROJ)
    ob = _moba_prompt(q, kb, vb, slopes, bp, seq)
    h, hn = _mix(u, vz, ob, ga, gb, xp, ws, bsp.T, w_br_b, w_out_b, norm_ffn[0], PROMPT_TM_MIX)
    zero_state = jnp.zeros((1, bp * SUBLANES, D_FF), F32)
    tm_p = PROMPT_TM_FFN
    yp, tail_p = _ffn(hn, h, w_up[0], w_conv[0], b_conv[0], w_down[0], norm_final, zero_state,
                      tm_p, seq)
    y_prompt = yp.reshape(bp, seq, D_MODEL)
    gv_p = vz.reshape(bp, seq, D_GMLP)[:, seq - CHUNK:][None]
    k_p = k.reshape(1, bp, seq, N_HEADS, HEAD_DIM)
    v_p = v.reshape(1, bp, seq, N_HEADS, HEAD_DIM)
    c_p = tail_p.reshape(bp, seq // tm_p, SUBLANES, D_FF)[:, -1, SUBLANES - (CONV_W - 1):][None]

    m_s = bs * t_q
    xs = x_sample.reshape(m_s, D_MODEL)
    xn_s = _norm_bf16(xs, norm_mix[0], m_s)
    us, vzs, qs, ks, kbs, vs, vbs, gas, gbs = _inproj(xn_s, w_in[0], norm_gmlp_v[0], m_s)
    q4 = qs.reshape(bs, t_q, N_HEADS, HEAD_DIM)
    eye = jnp.eye(N_HEADS, dtype=BF16)
    qbd = (q4.transpose(0, 2, 1, 3)[:, :, :, None, :] * eye[None, :, None, :, None]
           ).reshape(bs, N_HEADS * t_q, D_ATTN)
    pad_rows = ((0, 0), (0, LANES - t_q), (0, 0))
    knew = jnp.pad(kbs.reshape(bs, t_q, D_ATTN), pad_rows)
    vnew = jnp.pad(vbs.reshape(bs, t_q, D_ATTN), pad_rows)
    slope_rows = jnp.repeat(slopes, t_q).reshape(N_HEADS * t_q, 1)
    ck = cache_k[0].reshape(-1, PAGE_SIZE * N_HEADS, HEAD_DIM)
    cv = cache_v[0].reshape(-1, PAGE_SIZE * N_HEADS, HEAD_DIM)
    p, pown, l = _sample_scores(page_table, qbd, knew, slope_rows, ck, t_q, past_len)
    obs = _sample_pv(page_table, p, pown, l, vnew, cv, t_q).reshape(m_s, D_ATTN)
    ws_s = (jnp.eye(bs, dtype=F32)[None, :, None, :, None]
            * ws[:, None, :t_q, None, :t_q]).reshape(GMLP_GROUPS, m_s, m_s)
    bs_t_s = jnp.tile(bsp[:, :t_q], (1, bs)).T
    hs, hns = _mix(us, vzs, obs, gas, gbs, xs, ws_s, bs_t_s, w_br_b, w_out_b, norm_ffn[0], m_s)
    st = state_ffn_conv[0]
    zero_rows = jnp.zeros((bs, t_q - 1, D_FF), F32)
    init1 = jnp.concatenate([st[:, 1:2], zero_rows], axis=1)
    init2 = jnp.concatenate([st[:, 0:1], st[:, 1:2], zero_rows[:, 1:]], axis=1)
    init = jnp.stack([init1.reshape(m_s, D_FF), init2.reshape(m_s, D_FF)])
    ys, tail_s = _ffn(hns, hs, w_up[0], w_conv[0], b_conv[0], w_down[0], norm_final, init, m_s, t_q)
    y_sample = ys.reshape(bs, t_q, D_MODEL)
    gv_s = vzs.reshape(1, bs, t_q, D_GMLP)
    k_s = ks.reshape(1, bs, t_q, N_HEADS, HEAD_DIM)
    v_s = vs.reshape(1, bs, t_q, N_HEADS, HEAD_DIM)
    c_s = tail_s.reshape(bs, t_q, D_FF)[:, t_q - (CONV_W - 1):][None]

    return (y_prompt, y_sample, gv_p, gv_s, k_p, v_p, k_s, v_s, c_p, c_s)
```

```python
import functools

import jax
import jax.numpy as jnp
from jax import lax
from jax.experimental import pallas as pl
from jax.experimental.pallas import tpu as pltpu

F32 = jnp.float32
BF16 = jnp.bfloat16

D_MODEL = 2048
D_GMLP = D_MODEL // 2
GMLP_GROUPS = 8
GMLP_GROUP_DIM = D_GMLP // GMLP_GROUPS
CHUNK = 128
D_ATTN = D_MODEL // 2
HEAD_DIM = 128
N_HEADS = D_ATTN // HEAD_DIM
MOBA_BLOCK = 256
MOBA_TOPK = 3
D_FF = 5632
CONV_W = 3
RMS_EPS = 1e-6
PAGE_SIZE = 128
N_SECTIONS = 9

LANES = 128
SUBLANES = 8
VMEM_LIMIT_BYTES = 56 * 1024 * 1024

NEG_BIG = -1e30
LOG2E = 1.4426950408889634
MOBA_HEADS_PER_STEP = 4
MOBA_BLOCKS_PER_TRIP = 4
MOBA_ONES_ROWS = 16
SAMPLE_CHUNK_BLOCKS = 8
FF_TILE = 512
PAGES_PER_STEP = 16
PROMPT_TM_NORM = 1024
PROMPT_TM_INPROJ = 512
PROMPT_TM_MIX = 256
PROMPT_TM_FFN = 1024

_NT = (((1,), (1,)), ((), ()))


def _dot(a, b):
    return jnp.dot(a, b, preferred_element_type=F32)


def _dot_nt(a, b):
    return lax.dot_general(a, b, _NT, preferred_element_type=F32)


def _rms(x, g):
    return x * lax.rsqrt(jnp.mean(x * x, axis=-1, keepdims=True) + RMS_EPS) * g


def _sigmoid(x):
    return 0.5 * (jnp.tanh(0.5 * x) + 1.0)


def _cparams(n_axes):
    return pltpu.CompilerParams(dimension_semantics=("arbitrary",) * n_axes,
                                vmem_limit_bytes=VMEM_LIMIT_BYTES)


def _top3_mask(gate, idx_f, axis=1):
    sel = jnp.zeros_like(gate)
    for _ in range(MOBA_TOPK):
        mx = jnp.max(gate, axis=axis, keepdims=True)
        first = jnp.min(jnp.where(gate == mx, idx_f, float(LANES)), axis=axis, keepdims=True)
        pick = (idx_f == first) & (mx > -jnp.inf)
        sel = jnp.where(pick, 1.0, sel)
        gate = jnp.where(pick, -jnp.inf, gate)
    return sel


def _store_heads(dst_ref, dst_b_ref, acc):
    tm = acc.shape[0]
    dst_b_ref[...] = acc.astype(BF16)
    for h in range(N_HEADS):
        dst_ref[pl.ds(h, tm, stride=N_HEADS), :] = acc[:, h * HEAD_DIM:(h + 1) * HEAD_DIM]


def _norm_kernel(x_ref, g_ref, o_ref):
    o_ref[...] = _rms(x_ref[...], g_ref[...]).astype(BF16)


def _norm_bf16(x, g, tm):
    m_rows, d = x.shape
    return pl.pallas_call(
        _norm_kernel,
        grid=(m_rows // tm,),
        in_specs=[pl.BlockSpec((tm, d), lambda m: (m, 0)), pl.BlockSpec((1, d), lambda m: (0, 0))],
        out_specs=pl.BlockSpec((tm, d), lambda m: (m, 0)),
        out_shape=jax.ShapeDtypeStruct((m_rows, d), BF16),
        compiler_params=_cparams(1),
        name="norm",
    )(x, g.reshape(1, d))


N_INPROJ_OUTS = 9


def _inproj_kernel(xn_ref, xs_ref, w_ref, ngv_ref, *rest):
    prompt = rest[:N_INPROJ_OUTS]
    sample = rest[N_INPROJ_OUTS:2 * N_INPROJ_OUTS]
    wb_ref = rest[2 * N_INPROJ_OUTS]
    n = pl.program_id(0)
    first_tile = pl.program_id(1) == 0

    @pl.when(first_tile)
    def _():
        wb_ref[...] = w_ref[...].astype(BF16)

    def emit(pred, epilogue):
        @pl.when(pred)
        def _():
            epilogue(_dot(xn_ref[...], wb_ref[...]), prompt)

            @pl.when(first_tile)
            def _():
                epilogue(_dot(xs_ref[...], wb_ref[...]), sample)

    def gelu_u(acc, outs):
        outs[0][...] = jax.nn.gelu(acc).astype(BF16)

    def gelu_norm_v(acc, outs):
        outs[1][...] = _rms(jax.nn.gelu(acc), ngv_ref[...])

    def plain_q(acc, outs):
        outs[2][...] = acc.astype(BF16)

    def heads_k(acc, outs):
        _store_heads(outs[3], outs[4], acc)

    def heads_v(acc, outs):
        _store_heads(outs[5], outs[6], acc)

    def gate_a(acc, outs):
        outs[7][...] = _sigmoid(acc).astype(BF16)

    def gate_b(acc, outs):
        outs[8][...] = _sigmoid(acc).astype(BF16)

    emit(n == 0, gelu_u)
    emit(n == 1, gelu_norm_v)
    emit(n == 2, plain_q)
    emit(n == 3, heads_k)
    emit(n == 4, heads_v)
    emit((n == 5) | (n == 6), gate_a)
    emit(n >= 7, gate_b)


def _inproj(xn, xn_s, w_in, norm_gmlp_v, tm):
    m_rows = xn.shape[0]
    s_rows = xn_s.shape[0]
    sec = D_GMLP
    last = m_rows // tm - 1

    def rows(first, count):
        def index(n, m):
            return jnp.where(n < first, 0, jnp.where(n >= first + count, last, m))
        return index

    def col(first, count):
        return lambda n: jnp.clip(n - first, 0, count - 1)

    def out(first, count=1):
        r, c = rows(first, count), col(first, count)
        return pl.BlockSpec((tm, sec), lambda n, m: (r(n, m), c(n)))

    def out_heads(first):
        r = rows(first, 1)
        return pl.BlockSpec((tm * N_HEADS, HEAD_DIM), lambda n, m: (r(n, m), 0))

    def out_s(first, count=1):
        c = col(first, count)
        return pl.BlockSpec((s_rows, sec), lambda n, m: (0, c(n)))

    out_heads_s = pl.BlockSpec((s_rows * N_HEADS, HEAD_DIM), lambda n, m: (0, 0))

    def shapes(r):
        return [
            jax.ShapeDtypeStruct((r, sec), BF16),
            jax.ShapeDtypeStruct((r, sec), F32),
            jax.ShapeDtypeStruct((r, sec), BF16),
            jax.ShapeDtypeStruct((r * N_HEADS, HEAD_DIM), F32),
            jax.ShapeDtypeStruct((r, sec), BF16),
            jax.ShapeDtypeStruct((r * N_HEADS, HEAD_DIM), F32),
            jax.ShapeDtypeStruct((r, sec), BF16),
            jax.ShapeDtypeStruct((r, D_MODEL), BF16),
            jax.ShapeDtypeStruct((r, D_MODEL), BF16),
        ]

    in_specs = [
        pl.BlockSpec((tm, D_MODEL), lambda n, m: (m, 0)),
        pl.BlockSpec((s_rows, D_MODEL), lambda n, m: (0, 0)),
        pl.BlockSpec((D_MODEL, sec), lambda n, m: (0, n)),
        pl.BlockSpec((1, sec), lambda n, m: (0, 0)),
    ]
    out_specs = [out(0), out(1), out(2), out_heads(3), out(3), out_heads(4), out(4), out(5, 2),
                 out(7, 2),
                 out_s(0), out_s(1), out_s(2), out_heads_s, out_s(3), out_heads_s, out_s(4),
                 out_s(5, 2), out_s(7, 2)]
    outs = pl.pallas_call(
        _inproj_kernel,
        grid=(N_SECTIONS, m_rows // tm),
        in_specs=in_specs,
        out_specs=out_specs,
        out_shape=shapes(m_rows) + shapes(s_rows),
        scratch_shapes=[pltpu.VMEM((D_MODEL, sec), BF16)],
        compiler_params=_cparams(2),
        name="inproj",
    )(xn, xn_s, w_in, norm_gmlp_v.reshape(1, sec))
    return outs[:N_INPROJ_OUTS], outs[N_INPROJ_OUTS:]


def _moba_prompt_kernel(slopes_ref, q_ref, kb_ref, vb_ref, o_ref, km_ref, vt_ref, bias_ref, sel_ref,
                        m_ref, t_ref, acc_ref, *, n_blocks):
    hg = pl.program_id(1)
    i = pl.program_id(2)
    blk = MOBA_BLOCK
    c1 = (HEAD_DIM ** -0.5) * LOG2E
    heads = [(hh, slice(hh * HEAD_DIM, (hh + 1) * HEAD_DIM)) for hh in range(MOBA_HEADS_PER_STEP)]

    def slope2(hh):
        return slopes_ref[hg * MOBA_HEADS_PER_STEP + hh] * LOG2E

    @pl.when(i == 0)
    def _():
        key_f = lax.broadcasted_iota(jnp.int32, (blk, blk), 0).astype(F32)
        for hh, cs in heads:
            bias_ref[hh] = slope2(hh) * key_f
            vt_ref[hh, HEAD_DIM:, :] = jnp.ones((MOBA_ONES_ROWS, n_blocks * blk), BF16)
            for j in range(n_blocks):
                rs = slice(j * blk, (j + 1) * blk)
                km_ref[hh, j:j + 1, :] = jnp.sum(kb_ref[rs, cs].astype(F32), axis=0,
                                                 keepdims=True) * (1.0 / blk)
                vt_ref[hh, :HEAD_DIM, rs] = vb_ref[rs, cs].astype(F32).T.astype(BF16)

    blk_i = lax.broadcasted_iota(jnp.int32, (n_blocks, blk), 0)
    causal = (lax.broadcasted_iota(jnp.int32, (blk, blk), 1)
              >= lax.broadcasted_iota(jnp.int32, (blk, blk), 0))
    start = pl.multiple_of(i * blk, blk)

    for hh, cs in heads:
        q = q_ref[:, cs]
        km = km_ref[hh]
        km_hi = km.astype(BF16)
        km_lo = (km - km_hi.astype(F32)).astype(BF16)
        gate = _dot_nt(km_hi, q) + _dot_nt(km_lo, q)
        gate = jnp.where(blk_i < i, gate, -jnp.inf)
        sel_ref[hh] = _top3_mask(gate, blk_i.astype(F32), axis=0)
        t = _dot_nt(kb_ref[pl.ds(start, blk), cs], q) * c1 + bias_ref[hh]
        t = jnp.where(causal, t, NEG_BIG)
        t_ref[hh, i] = t
        m_ref[hh] = jnp.max(t, axis=0, keepdims=True)

    def shift(hh, j):
        return slope2(hh) * lax.convert_element_type((i - j) * blk, F32)

    def pass1(j0, nb):
        for hh, cs in heads:
            m = m_ref[hh]
            for j in [j0 + d for d in range(nb)]:
                off = pl.multiple_of(j * blk, blk)
                t = _dot_nt(kb_ref[pl.ds(off, blk), cs], q_ref[:, cs]) * c1 + bias_ref[hh]
                t_ref[hh, j] = t
                picked = sel_ref[hh, pl.ds(j, 1), :] > 0.0
                m_blk = jnp.max(t, axis=0, keepdims=True) - shift(hh, j)
                m = jnp.where(picked, jnp.maximum(m, m_blk), m)
            m_ref[hh] = m

    def pass2(j0, nb):
        off = pl.multiple_of(j0 * blk, blk)
        for hh, cs in heads:
            ps = []
            for j in [j0 + d for d in range(nb)]:
                picked = sel_ref[hh, pl.ds(j, 1), :] > 0.0
                sub = jnp.where(picked, m_ref[hh] + shift(hh, j), -NEG_BIG)
                ps.append(jnp.exp2(t_ref[hh, j] - sub).astype(BF16))
            p = ps[0] if nb == 1 else jnp.concatenate(ps, axis=0)
            acc_ref[hh] += _dot(vt_ref[hh, :, pl.ds(off, nb * blk)], p)

    def over_past_blocks(fn):
        done = 0
        un = MOBA_BLOCKS_PER_TRIP
        while un >= 1:
            n_trips = (i - done) // un

            def trip(t, carry, un=un, base=done):
                fn(base + t * un, un)
                return carry

            lax.fori_loop(0, n_trips, trip, 0)
            done = done + n_trips * un
            un //= 2

    over_past_blocks(pass1)
    for hh, cs in heads:
        p = jnp.exp2(t_ref[hh, i] - m_ref[hh])
        acc_ref[hh] = _dot(vt_ref[hh, :, pl.ds(start, blk)], p.astype(BF16))
    over_past_blocks(pass2)
    for hh, cs in heads:
        acc = acc_ref[hh]
        o_ref[:, cs] = (acc[:HEAD_DIM] / acc[HEAD_DIM:HEAD_DIM + 1]).T.astype(BF16)


def _moba_prompt(q, kb, vb, slopes, bsz, seq):
    n_blocks = seq // MOBA_BLOCK
    blk = MOBA_BLOCK
    hps = MOBA_HEADS_PER_STEP
    width = hps * HEAD_DIM
    qmap = lambda b, hg, i: (b * n_blocks + i, hg)
    kvmap = lambda b, hg, i: (b, hg)
    return pl.pallas_call(
        functools.partial(_moba_prompt_kernel, n_blocks=n_blocks),
        grid=(bsz, N_HEADS // hps, n_blocks),
        in_specs=[
            pl.BlockSpec(memory_space=pltpu.SMEM),
            pl.BlockSpec((blk, width), qmap),
            pl.BlockSpec((seq, width), kvmap),
            pl.BlockSpec((seq, width), kvmap),
        ],
        out_specs=pl.BlockSpec((blk, width), qmap),
        out_shape=jax.ShapeDtypeStruct((bsz * seq, D_ATTN), BF16),
        scratch_shapes=[
            pltpu.VMEM((hps, n_blocks, HEAD_DIM), F32),
            pltpu.VMEM((hps, HEAD_DIM + MOBA_ONES_ROWS, seq), BF16),
            pltpu.VMEM((hps, blk, blk), F32),
            pltpu.VMEM((hps, n_blocks, blk), F32),
            pltpu.VMEM((hps, 1, blk), F32),
            pltpu.VMEM((hps, n_blocks, blk, blk), F32),
            pltpu.VMEM((hps, HEAD_DIM + MOBA_ONES_ROWS, blk), F32),
        ],
        compiler_params=_cparams(3),
        name="moba_prompt",
    )(slopes, q, kb, vb)


def _page_heads(page_ref):
    return [page_ref[pl.ds(h, PAGE_SIZE, stride=N_HEADS), :] for h in range(N_HEADS)]


def _load_page(page_ref):
    return jnp.concatenate([s.astype(BF16) for s in _page_heads(page_ref)], axis=1)


def _page_specs():
    rows = PAGE_SIZE * N_HEADS
    return [
        pl.BlockSpec((None, rows, HEAD_DIM),
                     functools.partial(lambda b, g, pt, pp: (pt[b, g * PAGES_PER_STEP + pp], 0, 0), pp=pp))
        for pp in range(PAGES_PER_STEP)
    ]


def _sample_scores_kernel(pt_ref, qbd_ref, knew_ref, slope_ref, *rest, t_q, past_len):
    del pt_ref
    k_refs = rest[:PAGES_PER_STEP]
    p_ref, pown_ref, l_ref, km_ref = rest[PAGES_PER_STEP:]
    g = pl.program_id(1)
    n_steps = pl.num_programs(1)
    rows = qbd_ref.shape[0]
    blk = MOBA_BLOCK
    pages_per_block = blk // PAGE_SIZE
    n_past_blocks = past_len // blk
    qbd = qbd_ref[...]
    lane = lax.broadcasted_iota(jnp.int32, (rows, LANES), 1)
    lane_f = lane.astype(F32)

    @pl.when(g == 0)
    def _():
        km_ref[n_past_blocks:, :] = jnp.zeros((LANES - n_past_blocks, D_ATTN), F32)

    ksum = None
    km_rows = []
    for pp in range(PAGES_PER_STEP):
        heads = _page_heads(k_refs[pp])
        kpage = jnp.concatenate([s.astype(BF16) for s in heads], axis=1)
        lg = _dot_nt(qbd, kpage)
        off = pl.multiple_of((g * PAGES_PER_STEP + pp) * PAGE_SIZE, PAGE_SIZE)
        p_ref[:, pl.ds(off, PAGE_SIZE)] = lg
        psum = jnp.concatenate(
            [jnp.sum(s.reshape(PAGE_SIZE // SUBLANES, SUBLANES, HEAD_DIM), axis=0) for s in heads],
            axis=1)
        ksum = psum if pp % pages_per_block == 0 else ksum + psum
        if pp % pages_per_block == pages_per_block - 1:
            km_rows.append(jnp.sum(ksum, axis=0, keepdims=True) * (1.0 / blk))
    step_blocks = PAGES_PER_STEP // pages_per_block
    km_ref[pl.ds(pl.multiple_of(g * step_blocks, step_blocks), step_blocks), :] = (
        jnp.concatenate(km_rows, axis=0))

    @pl.when(g == n_steps - 1)
    def _():
        gate = _dot_nt(qbd, km_ref[...].astype(BF16))
        c1 = (HEAD_DIM ** -0.5) * LOG2E
        slope2 = slope_ref[...] * LOG2E
        t_row = (lax.broadcasted_iota(jnp.int32, (rows, 1), 0) % t_q).astype(F32)
        sel = _top3_mask(jnp.where(lane < n_past_blocks, gate, -jnp.inf), lane_f)

        s_own = _dot_nt(qbd, knew_ref[...]) * c1 + slope2 * lane_f
        s_own = jnp.where(lane_f <= t_row, s_own, NEG_BIG)

        cb = SAMPLE_CHUNK_BLOCKS
        ch = cb * blk
        n_chunks = n_past_blocks // cb
        assert cb & (cb - 1) == 0 and blk & (blk - 1) == 0
        widen = jnp.where(
            jnp.bitwise_and(lax.broadcasted_iota(jnp.int32, (LANES, ch), 0), cb - 1)
            == jnp.right_shift(lax.broadcasted_iota(jnp.int32, (LANES, ch), 1), blk.bit_length() - 1),
            1.0, 0.0).astype(BF16)
        key_f = lax.broadcasted_iota(jnp.int32, (1, ch), 1).astype(F32)

        mx = jnp.full((rows, blk), NEG_BIG, F32)
        for c in range(n_chunks):
            sel_c = jnp.where((lane >= c * cb) & (lane < (c + 1) * cb), sel, 0.0).astype(BF16)
            picked = _dot(sel_c, widen)
            s = p_ref[:, c * ch:(c + 1) * ch] * c1 + slope2 * (key_f + float(c * ch - past_len))
            s = jnp.where(picked > 0.0, s, NEG_BIG)
            p_ref[:, c * ch:(c + 1) * ch] = s
            for b in range(cb):
                mx = jnp.maximum(mx, s[:, b * blk:(b + 1) * blk])
        m = jnp.maximum(jnp.max(mx, axis=1, keepdims=True), jnp.max(s_own, axis=1, keepdims=True))

        p_own = jnp.exp2(s_own - m)
        pown_ref[...] = p_own
        acc = jnp.zeros((rows, blk), F32)
        for c in range(n_chunks):
            pc = jnp.exp2(p_ref[:, c * ch:(c + 1) * ch] - m)
            p_ref[:, c * ch:(c + 1) * ch] = pc
            for b in range(cb):
                acc = acc + pc[:, b * blk:(b + 1) * blk]
        l = jnp.sum(acc, axis=1, keepdims=True) + jnp.sum(p_own, axis=1, keepdims=True)
        l_ref[...] = jnp.broadcast_to(l, l_ref.shape)


def _sample_scores(page_table, qbd, knew_pad, slope_rows, cache_k_pages, t_q, past_len):
    bsz, rows, _ = qbd.shape
    n_pages = page_table.shape[1]
    n_steps = n_pages // PAGES_PER_STEP
    grid_spec = pltpu.PrefetchScalarGridSpec(
        num_scalar_prefetch=1,
        grid=(bsz, n_steps),
        in_specs=[
            pl.BlockSpec((None, rows, D_ATTN), lambda b, g, pt: (b, 0, 0)),
            pl.BlockSpec((None, LANES, D_ATTN), lambda b, g, pt: (b, 0, 0)),
            pl.BlockSpec((rows, 1), lambda b, g, pt: (0, 0)),
        ] + _page_specs(),
        out_specs=[
            pl.BlockSpec((None, rows, past_len), lambda b, g, pt: (b, 0, 0)),
            pl.BlockSpec((None, rows, LANES), lambda b, g, pt: (b, 0, 0)),
            pl.BlockSpec((None, rows, LANES), lambda b, g, pt: (b, 0, 0)),
        ],
        scratch_shapes=[pltpu.VMEM((LANES, D_ATTN), F32)],
    )
    return pl.pallas_call(
        functools.partial(_sample_scores_kernel, t_q=t_q, past_len=past_len),
        grid_spec=grid_spec,
        out_shape=[
            jax.ShapeDtypeStruct((bsz, rows, past_len), F32),
            jax.ShapeDtypeStruct((bsz, rows, LANES), F32),
            jax.ShapeDtypeStruct((bsz, rows, LANES), F32),
        ],
        compiler_params=_cparams(2),
        name="sample_scores",
    )(page_table, qbd, knew_pad, slope_rows, *([cache_k_pages] * PAGES_PER_STEP))


def _sample_pv_kernel(pt_ref, p_ref, pown_ref, l_ref, vnew_ref, *rest, t_q):
    del pt_ref
    v_refs = rest[:PAGES_PER_STEP]
    o_ref, acc_ref = rest[PAGES_PER_STEP:]
    g = pl.program_id(1)
    n_steps = pl.num_programs(1)

    @pl.when(g == 0)
    def _():
        acc_ref[...] = _dot(pown_ref[...].astype(BF16), vnew_ref[...])

    acc = acc_ref[...]
    for pp in range(PAGES_PER_STEP):
        acc = acc + _dot(p_ref[:, pp * PAGE_SIZE:(pp + 1) * PAGE_SIZE].astype(BF16),
                         _load_page(v_refs[pp]))
    acc_ref[...] = acc

    @pl.when(g == n_steps - 1)
    def _():
        for h in range(N_HEADS):
            rs = slice(h * t_q, (h + 1) * t_q)
            cs = slice(h * HEAD_DIM, (h + 1) * HEAD_DIM)
            o_ref[:, cs] = (acc_ref[rs, cs] / l_ref[rs, 0:1]).astype(BF16)


def _sample_pv(page_table, p, pown, l, vnew_pad, cache_v_pages, t_q):
    bsz, rows, _ = p.shape
    n_pages = page_table.shape[1]
    n_steps = n_pages // PAGES_PER_STEP
    step_keys = PAGES_PER_STEP * PAGE_SIZE
    grid_spec = pltpu.PrefetchScalarGridSpec(
        num_scalar_prefetch=1,
        grid=(bsz, n_steps),
        in_specs=[
            pl.BlockSpec((None, rows, step_keys), lambda b, g, pt: (b, 0, g)),
            pl.BlockSpec((None, rows, LANES), lambda b, g, pt: (b, 0, 0)),
            pl.BlockSpec((None, rows, LANES), lambda b, g, pt: (b, 0, 0)),
            pl.BlockSpec((None, LANES, D_ATTN), lambda b, g, pt: (b, 0, 0)),
        ] + _page_specs(),
        out_specs=pl.BlockSpec((None, t_q, D_ATTN), lambda b, g, pt: (b, 0, 0)),
        scratch_shapes=[pltpu.VMEM((rows, D_ATTN), F32)],
    )
    return pl.pallas_call(
        functools.partial(_sample_pv_kernel, t_q=t_q),
        grid_spec=grid_spec,
        out_shape=jax.ShapeDtypeStruct((bsz, t_q, D_ATTN), BF16),
        compiler_params=_cparams(2),
        name="sample_pv",
    )(page_table, p, pown, l, vnew_pad, *([cache_v_pages] * PAGES_PER_STEP))


def _mix_kernel(u_ref, vz_ref, ob_ref, ga_ref, gb_ref, x_ref, ws_ref, bst_ref, wbr_ref, wout_ref,
                nffn_ref, h_ref, hn_ref, oa_ref):
    tm = u_ref.shape[0]
    chunk = ws_ref.shape[1]
    causal = (lax.broadcasted_iota(jnp.int32, (chunk, chunk), 0)
              >= lax.broadcasted_iota(jnp.int32, (chunk, chunk), 1))
    for g in range(GMLP_GROUPS):
        cs = slice(g * GMLP_GROUP_DIM, (g + 1) * GMLP_GROUP_DIM)
        w_g = jnp.where(causal, ws_ref[g], 0.0).astype(BF16)
        b_g = bst_ref[:, g:g + 1]
        for c in range(tm // chunk):
            rs = slice(c * chunk, (c + 1) * chunk)
            mixed = _dot(w_g, vz_ref[rs, cs].astype(BF16)) + b_g
            oa_ref[rs, cs] = (u_ref[rs, cs].astype(F32) * mixed).astype(BF16)
    merged = (ga_ref[...].astype(F32) * _dot(oa_ref[...], wbr_ref[0])
              + gb_ref[...].astype(F32) * _dot(ob_ref[...], wbr_ref[1]))
    h = x_ref[...] + _dot(merged.astype(BF16), wout_ref[...])
    h_ref[...] = h
    hn_ref[...] = _rms(h, nffn_ref[...]).astype(BF16)


def _mix(u, vz, ob, ga, gb, x, ws_chunk, bs_t, w_br_b, w_out_b, norm_ffn, tm):
    m_rows = x.shape[0]
    chunk = ws_chunk.shape[1]
    row = lambda m: (m, 0)
    const2 = lambda m: (0, 0)
    const3 = lambda m: (0, 0, 0)
    once = pl.Buffered(1)
    return pl.pallas_call(
        _mix_kernel,
        grid=(m_rows // tm,),
        in_specs=[
            pl.BlockSpec((tm, D_GMLP), row),
            pl.BlockSpec((tm, D_GMLP), row),
            pl.BlockSpec((tm, D_ATTN), row),
            pl.BlockSpec((tm, D_MODEL), row),
            pl.BlockSpec((tm, D_MODEL), row),
            pl.BlockSpec((tm, D_MODEL), row),
            pl.BlockSpec((GMLP_GROUPS, chunk, chunk), const3, pipeline_mode=once),
            pl.BlockSpec((chunk, GMLP_GROUPS), const2, pipeline_mode=once),
            pl.BlockSpec((2, D_GMLP, D_MODEL), const3, pipeline_mode=once),
            pl.BlockSpec((D_MODEL, D_MODEL), const2, pipeline_mode=once),
            pl.BlockSpec((1, D_MODEL), const2, pipeline_mode=once),
        ],
        out_specs=[pl.BlockSpec((tm, D_MODEL), row), pl.BlockSpec((tm, D_MODEL), row)],
        out_shape=[jax.ShapeDtypeStruct((m_rows, D_MODEL), F32),
                   jax.ShapeDtypeStruct((m_rows, D_MODEL), BF16)],
        scratch_shapes=[pltpu.VMEM((tm, D_GMLP), BF16)],
        compiler_params=_cparams(1),
        name="mix",
    )(u, vz, ob, ga, gb, x, ws_chunk, bs_t, w_br_b, w_out_b, norm_ffn.reshape(1, D_MODEL))


def _ffn_tile(hn, wa, wb, wd, wc, bc, history):
    a = _dot(hn, wa)
    b = _dot(hn, wb)
    a1, a2 = history(a, pltpu.roll(a, 1, 0), pltpu.roll(a, 2, 0))
    conv = bc + a * wc[2:3, :] + a2 * wc[0:1, :] + a1 * wc[1:2, :]
    act = (jax.nn.gelu(conv) * b).astype(BF16)
    return a, _dot(act, wd)


def _ffn_kernel(hn_ref, wa_ref, wb_ref, wc_ref, bc_ref, wd_ref, h_ref, nfin_ref, init_ref,
                hns_ref, hs_ref, inits_ref, y_ref, tail_ref, ys_ref, tails_ref, halo_ref,
                *, tiles_per_seq, sample_seq):
    m = pl.program_id(0)
    n = pl.program_id(1)
    last_n = pl.num_programs(1) - 1
    tm = hn_ref.shape[0]
    tf = wa_ref.shape[1]

    @pl.when(n == 0)
    def _():
        y_ref[...] = h_ref[...]

    wa = wa_ref[...].astype(BF16)
    wb = wb_ref[...].astype(BF16)
    wd = wd_ref[...].astype(BF16)
    wc = wc_ref[...]
    bc = bc_ref[...]

    def carried_history(a, r1, r2):
        row = lax.broadcasted_iota(jnp.int32, (tm, tf), 0)
        prev = jnp.where(m % tiles_per_seq == 0, init_ref[0], halo_ref[n])
        p1 = prev[SUBLANES - 1:SUBLANES, :]
        p2 = prev[SUBLANES - 2:SUBLANES - 1, :]
        return (jnp.where(row == 0, p1, r1),
                jnp.where(row == 0, p2, jnp.where(row == 1, p1, r2)))

    a, part = _ffn_tile(hn_ref[...], wa, wb, wd, wc, bc, carried_history)
    last = a[tm - SUBLANES:, :]
    halo_ref[n] = last
    tail_ref[...] = last
    y_ref[...] += part

    @pl.when(n == last_n)
    def _():
        y_ref[...] = _rms(y_ref[...], nfin_ref[...])

    @pl.when(m == 0)
    def _():
        @pl.when(n == 0)
        def _():
            ys_ref[...] = hs_ref[...]

        def given_history(a_s, r1, r2):
            pos = lax.broadcasted_iota(jnp.int32, a_s.shape, 0) % sample_seq
            return jnp.where(pos >= 1, r1, inits_ref[0]), jnp.where(pos >= 2, r2, inits_ref[1])

        a_s, part_s = _ffn_tile(hns_ref[...], wa, wb, wd, wc, bc, given_history)
        tails_ref[...] = a_s
        ys_ref[...] += part_s

        @pl.when(n == last_n)
        def _():
            ys_ref[...] = _rms(ys_ref[...], nfin_ref[...])


def _ffn(hn, h, hn_s, h_s, w_up, w_conv, b_conv, w_down, norm_final, init, init_s, tm, seq_len,
         sample_seq):
    m_rows = hn.shape[0]
    s_rows = hn_s.shape[0]
    tf = FF_TILE
    n_ff = D_FF // tf
    tiles_per_seq = seq_len // tm
    row = lambda m, n: (m, 0)
    const = lambda m, n: (0, 0)
    sample_tile = lambda m, n: jnp.where(m == 0, n, n_ff - 1)
    once = pl.Buffered(1)
    return pl.pallas_call(
        functools.partial(_ffn_kernel, tiles_per_seq=tiles_per_seq, sample_seq=sample_seq),
        grid=(m_rows // tm, n_ff),
        in_specs=[
            pl.BlockSpec((tm, D_MODEL), row, pipeline_mode=once),
            pl.BlockSpec((D_MODEL, tf), lambda m, n: (0, n)),
            pl.BlockSpec((D_MODEL, tf), lambda m, n: (0, n_ff + n)),
            pl.BlockSpec((CONV_W, tf), lambda m, n: (0, n)),
            pl.BlockSpec((1, tf), lambda m, n: (0, n)),
            pl.BlockSpec((tf, D_MODEL), lambda m, n: (n, 0)),
            pl.BlockSpec((tm, D_MODEL), row, pipeline_mode=once),
            pl.BlockSpec((1, D_MODEL), const),
            pl.BlockSpec((1, SUBLANES, tf), lambda m, n: (0, m // tiles_per_seq, n)),
            pl.BlockSpec((s_rows, D_MODEL), const),
            pl.BlockSpec((s_rows, D_MODEL), const),
            pl.BlockSpec((2, s_rows, tf), lambda m, n: (0, 0, sample_tile(m, n))),
        ],
        out_specs=[
            pl.BlockSpec((tm, D_MODEL), row, pipeline_mode=once),
            pl.BlockSpec((SUBLANES, tf), lambda m, n: (m, n)),
            pl.BlockSpec((s_rows, D_MODEL), const),
            pl.BlockSpec((s_rows, tf), lambda m, n: (0, sample_tile(m, n))),
        ],
        out_shape=[
            jax.ShapeDtypeStruct((m_rows, D_MODEL), F32),
            jax.ShapeDtypeStruct(((m_rows // tm) * SUBLANES, D_FF), F32),
            jax.ShapeDtypeStruct((s_rows, D_MODEL), F32),
            jax.ShapeDtypeStruct((s_rows, D_FF), F32),
        ],
        scratch_shapes=[pltpu.VMEM((n_ff, SUBLANES, tf), F32)],
        compiler_params=_cparams(2),
        name="ffn",
    )(hn, w_up, w_up, w_conv, b_conv.reshape(1, D_FF), w_down, h,
      norm_final.reshape(1, D_MODEL), init, hn_s, h_s, init_s)


def _alibi_slopes():
    return jnp.exp2(-8.0 * jnp.arange(1, N_HEADS + 1, dtype=F32) / N_HEADS)


def kernel(x_prompt, x_sample, cache_k, cache_v, state_ffn_conv, page_table, norm_mix, w_in,
           norm_gmlp_v, w_spatial, b_spatial, w_branch, w_out, norm_ffn, w_up, w_conv, b_conv,
           w_down, norm_final):
    assert w_in.shape[0] == 1, "single layer"
    bp, seq, _ = x_prompt.shape
    bs, t_q, _ = x_sample.shape
    n_pages = page_table.shape[1]
    past_len = n_pages * PAGE_SIZE
    assert seq % MOBA_BLOCK == 0 and past_len % MOBA_BLOCK == 0 and past_len % CHUNK == 0
    assert t_q <= SUBLANES and past_len // MOBA_BLOCK <= LANES and seq // MOBA_BLOCK <= LANES

    slopes = _alibi_slopes()
    w_br_b = w_branch[0].astype(BF16)
    w_out_b = w_out[0].astype(BF16)
    ws, bsp = w_spatial[0], b_spatial[0]

    m_s = bs * t_q
    xp = x_prompt.reshape(bp * seq, D_MODEL)
    xs = x_sample.reshape(m_s, D_MODEL)
    xn_p = _norm_bf16(xp, norm_mix[0], PROMPT_TM_NORM)
    xn_s = _norm_bf16(xs, norm_mix[0], m_s)
    (u, vz, q, k, kb, v, vb, ga, gb), (us, vzs, qs, ks, kbs, vs, vbs, gas, gbs) = _inproj(
        xn_p, xn_s, w_in[0], norm_gmlp_v[0], PROMPT_TM_INPROJ)

    ob = _moba_prompt(q, kb, vb, slopes, bp, seq)
    h, hn = _mix(u, vz, ob, ga, gb, xp, ws, bsp.T, w_br_b, w_out_b, norm_ffn[0], PROMPT_TM_MIX)

    q4 = qs.reshape(bs, t_q, N_HEADS, HEAD_DIM)
    eye = jnp.eye(N_HEADS, dtype=BF16)
    qbd = (q4.transpose(0, 2, 1, 3)[:, :, :, None, :] * eye[None, :, None, :, None]
           ).reshape(bs, N_HEADS * t_q, D_ATTN)
    pad_rows = ((0, 0), (0, LANES - t_q), (0, 0))
    knew = jnp.pad(kbs.reshape(bs, t_q, D_ATTN), pad_rows)
    vnew = jnp.pad(vbs.reshape(bs, t_q, D_ATTN), pad_rows)
    slope_rows = jnp.repeat(slopes, t_q).reshape(N_HEADS * t_q, 1)
    ck = cache_k[0].reshape(-1, PAGE_SIZE * N_HEADS, HEAD_DIM)
    cv = cache_v[0].reshape(-1, PAGE_SIZE * N_HEADS, HEAD_DIM)
    p, pown, l = _sample_scores(page_table, qbd, knew, slope_rows, ck, t_q, past_len)
    obs = _sample_pv(page_table, p, pown, l, vnew, cv, t_q).reshape(m_s, D_ATTN)
    ws_s = (jnp.eye(bs, dtype=F32)[None, :, None, :, None]
            * ws[:, None, :t_q, None, :t_q]).reshape(GMLP_GROUPS, m_s, m_s)
    bs_t_s = jnp.tile(bsp[:, :t_q], (1, bs)).T
    hs, hns = _mix(us, vzs, obs, gas, gbs, xs, ws_s, bs_t_s, w_br_b, w_out_b, norm_ffn[0], m_s)
    st = state_ffn_conv[0]
    zero_rows = jnp.zeros((bs, t_q - 1, D_FF), F32)
    init1 = jnp.concatenate([st[:, 1:2], zero_rows], axis=1)
    init2 = jnp.concatenate([st[:, 0:1], st[:, 1:2], zero_rows[:, 1:]], axis=1)
    init_s = jnp.stack([init1.reshape(m_s, D_FF), init2.reshape(m_s, D_FF)])

    zero_state = jnp.zeros((1, bp * SUBLANES, D_FF), F32)
    tm_p = PROMPT_TM_FFN
    yp, tail_p, ys, tail_s = _ffn(hn, h, hns, hs, w_up[0], w_conv[0], b_conv[0], w_down[0],
                                  norm_final, zero_state, init_s, tm_p, seq, t_q)

    y_prompt = yp.reshape(bp, seq, D_MODEL)
    gv_p = vz.reshape(bp, seq, D_GMLP)[:, seq - CHUNK:][None]
    k_p = k.reshape(1, bp, seq, N_HEADS, HEAD_DIM)
    v_p = v.reshape(1, bp, seq, N_HEADS, HEAD_DIM)
    c_p = tail_p.reshape(bp, seq // tm_p, SUBLANES, D_FF)[:, -1, SUBLANES - (CONV_W - 1):][None]
    y_sample = ys.reshape(bs, t_q, D_MODEL)
    gv_s = vzs.reshape(1, bs, t_q, D_GMLP)
    k_s = ks.reshape(1, bs, t_q, N_HEADS, HEAD_DIM)
    v_s = vs.reshape(1, bs, t_q, N_HEADS, HEAD_DIM)
    c_s = tail_s.reshape(bs, t_q, D_FF)[:, t_q - (CONV_W - 1):][None]

    return (y_prompt, y_sample, gv_p, gv_s, k_p, v_p, k_s, v_s, c_p, c_s)
```

```python
import functools

import jax
import jax.numpy as jnp
from jax import lax
from jax.experimental import pallas as pl
from jax.experimental.pallas import tpu as pltpu

F32 = jnp.float32
BF16 = jnp.bfloat16

D_MODEL = 2048
D_GMLP = D_MODEL // 2
GMLP_GROUPS = 8
GMLP_GROUP_DIM = D_GMLP // GMLP_GROUPS
CHUNK = 128
D_ATTN = D_MODEL // 2
HEAD_DIM = 128
N_HEADS = D_ATTN // HEAD_DIM
MOBA_BLOCK = 256
MOBA_TOPK = 3
D_FF = 5632
CONV_W = 3
RMS_EPS = 1e-6
PAGE_SIZE = 128
N_SECTIONS = 9

LANES = 128
SUBLANES = 8
VMEM_LIMIT_BYTES = 56 * 1024 * 1024

NEG_BIG = -1e30
LOG2E = 1.4426950408889634
MOBA_HEADS_PER_STEP = 4
MOBA_BLOCKS_PER_TRIP = 4
MOBA_ONES_ROWS = 16
SAMPLE_CHUNK_BLOCKS = 8
FF_TILE = 512
PAGES_PER_STEP = 16
PROMPT_TM_NORM = 1024
PROMPT_TM_INPROJ = 512
PROMPT_TM_MIX = 256
PROMPT_TM_FFN = 1024

_NT = (((1,), (1,)), ((), ()))


def _dot(a, b):
    return jnp.dot(a, b, preferred_element_type=F32)


def _dot_nt(a, b):
    return lax.dot_general(a, b, _NT, preferred_element_type=F32)


def _rms(x, g):
    return x * lax.rsqrt(jnp.mean(x * x, axis=-1, keepdims=True) + RMS_EPS) * g


def _sigmoid(x):
    return 0.5 * (jnp.tanh(0.5 * x) + 1.0)


def _cparams(n_axes):
    return pltpu.CompilerParams(dimension_semantics=("arbitrary",) * n_axes,
                                vmem_limit_bytes=VMEM_LIMIT_BYTES)


def _top3_mask(gate, idx_f, axis=1):
    sel = jnp.zeros_like(gate)
    for _ in range(MOBA_TOPK):
        mx = jnp.max(gate, axis=axis, keepdims=True)
        first = jnp.min(jnp.where(gate == mx, idx_f, float(LANES)), axis=axis, keepdims=True)
        pick = (idx_f == first) & (mx > -jnp.inf)
        sel = jnp.where(pick, 1.0, sel)
        gate = jnp.where(pick, -jnp.inf, gate)
    return sel


def _store_heads(dst_ref, dst_b_ref, acc):
    tm = acc.shape[0]
    dst_b_ref[...] = acc.astype(BF16)
    for h in range(N_HEADS):
        dst_ref[pl.ds(h, tm, stride=N_HEADS), :] = acc[:, h * HEAD_DIM:(h + 1) * HEAD_DIM]


def _norm_kernel(x_ref, g_ref, o_ref):
    o_ref[...] = _rms(x_ref[...], g_ref[...]).astype(BF16)


def _norm_bf16(x, g, tm):
    m_rows, d = x.shape
    return pl.pallas_call(
        _norm_kernel,
        grid=(m_rows // tm,),
        in_specs=[pl.BlockSpec((tm, d), lambda m: (m, 0)), pl.BlockSpec((1, d), lambda m: (0, 0))],
        out_specs=pl.BlockSpec((tm, d), lambda m: (m, 0)),
        out_shape=jax.ShapeDtypeStruct((m_rows, d), BF16),
        compiler_params=_cparams(1),
        name="norm",
    )(x, g.reshape(1, d))


N_INPROJ_OUTS = 9


def _inproj_kernel(xn_ref, xs_ref, w_ref, ngv_ref, *rest):
    prompt = rest[:N_INPROJ_OUTS]
    sample = rest[N_INPROJ_OUTS:2 * N_INPROJ_OUTS]
    wb_ref = rest[2 * N_INPROJ_OUTS]
    n = pl.program_id(0)
    first_tile = pl.program_id(1) == 0

    @pl.when(first_tile)
    def _():
        wb_ref[...] = w_ref[...].astype(BF16)

    def emit(pred, epilogue):
        @pl.when(pred)
        def _():
            epilogue(_dot(xn_ref[...], wb_ref[...]), prompt)

            @pl.when(first_tile)
            def _():
                epilogue(_dot(xs_ref[...], wb_ref[...]), sample)

    def gelu_u(acc, outs):
        outs[0][...] = jax.nn.gelu(acc).astype(BF16)

    def gelu_norm_v(acc, outs):
        outs[1][...] = _rms(jax.nn.gelu(acc), ngv_ref[...])

    def plain_q(acc, outs):
        outs[2][...] = acc.astype(BF16)

    def heads_k(acc, outs):
        _store_heads(outs[3], outs[4], acc)

    def heads_v(acc, outs):
        _store_heads(outs[5], outs[6], acc)

    def gate_a(acc, outs):
        outs[7][...] = _sigmoid(acc).astype(BF16)

    def gate_b(acc, outs):
        outs[8][...] = _sigmoid(acc).astype(BF16)

    emit(n == 0, gelu_u)
    emit(n == 1, gelu_norm_v)
    emit(n == 2, plain_q)
    emit(n == 3, heads_k)
    emit(n == 4, heads_v)
    emit((n == 5) | (n == 6), gate_a)
    emit(n >= 7, gate_b)


def _inproj(xn, xn_s, w_in, norm_gmlp_v, tm):
    m_rows = xn.shape[0]
    s_rows = xn_s.shape[0]
    sec = D_GMLP
    last = m_rows // tm - 1

    def rows(first, count):
        def index(n, m):
            return jnp.where(n < first, 0, jnp.where(n >= first + count, last, m))
        return index

    def col(first, count):
        return lambda n: jnp.clip(n - first, 0, count - 1)

    def out(first, count=1):
        r, c = rows(first, count), col(first, count)
        return pl.BlockSpec((tm, sec), lambda n, m: (r(n, m), c(n)))

    def out_heads(first):
        r = rows(first, 1)
        return pl.BlockSpec((tm * N_HEADS, HEAD_DIM), lambda n, m: (r(n, m), 0))

    def out_s(first, count=1):
        c = col(first, count)
        return pl.BlockSpec((s_rows, sec), lambda n, m: (0, c(n)))

    out_heads_s = pl.BlockSpec((s_rows * N_HEADS, HEAD_DIM), lambda n, m: (0, 0))

    def shapes(r):
        return [
            jax.ShapeDtypeStruct((r, sec), BF16),
            jax.ShapeDtypeStruct((r, sec), F32),
            jax.ShapeDtypeStruct((r, sec), BF16),
            jax.ShapeDtypeStruct((r * N_HEADS, HEAD_DIM), F32),
            jax.ShapeDtypeStruct((r, sec), BF16),
            jax.ShapeDtypeStruct((r * N_HEADS, HEAD_DIM), F32),
            jax.ShapeDtypeStruct((r, sec), BF16),
            jax.ShapeDtypeStruct((r, D_MODEL), BF16),
            jax.ShapeDtypeStruct((r, D_MODEL), BF16),
        ]

    in_specs = [
        pl.BlockSpec((tm, D_MODEL), lambda n, m: (m, 0)),
        pl.BlockSpec((s_rows, D_MODEL), lambda n, m: (0, 0)),
        pl.BlockSpec((D_MODEL, sec), lambda n, m: (0, n)),
        pl.BlockSpec((1, sec), lambda n, m: (0, 0)),
    ]
    out_specs = [out(0), out(1), out(2), out_heads(3), out(3), out_heads(4), out(4), out(5, 2),
                 out(7, 2),
                 out_s(0), out_s(1), out_s(2), out_heads_s, out_s(3), out_heads_s, out_s(4),
                 out_s(5, 2), out_s(7, 2)]
    outs = pl.pallas_call(
        _inproj_kernel,
        grid=(N_SECTIONS, m_rows // tm),
        in_specs=in_specs,
        out_specs=out_specs,
        out_shape=shapes(m_rows) + shapes(s_rows),
        scratch_shapes=[pltpu.VMEM((D_MODEL, sec), BF16)],
        compiler_params=_cparams(2),
        name="inproj",
    )(xn, xn_s, w_in, norm_gmlp_v.reshape(1, sec))
    return outs[:N_INPROJ_OUTS], outs[N_INPROJ_OUTS:]


def _moba_prompt_kernel(pt_ref, slopes_ref, q_ref, kb_ref, vb_ref, p_ref, pown_ref, l_ref, vnew_ref,
                        *rest, n_blocks, steps_per_sample, t_q):
    del pt_ref
    v_refs = rest[:PAGES_PER_STEP]
    (o_ref, os_ref, km_ref, vt_ref, bias_ref, sel_ref, m_ref, t_ref, acc_ref,
     accs_ref) = rest[PAGES_PER_STEP:]
    hg = pl.program_id(1)
    i = pl.program_id(2)
    step = (pl.program_id(0) * pl.num_programs(1) + hg) * n_blocks + i
    sample_g = step % steps_per_sample
    blk = MOBA_BLOCK

    @pl.when(step == 0)
    def _():
        accs_ref[...] = jnp.zeros_like(accs_ref)
    c1 = (HEAD_DIM ** -0.5) * LOG2E
    heads = [(hh, slice(hh * HEAD_DIM, (hh + 1) * HEAD_DIM)) for hh in range(MOBA_HEADS_PER_STEP)]

    def slope2(hh):
        return slopes_ref[hg * MOBA_HEADS_PER_STEP + hh] * LOG2E

    @pl.when(i == 0)
    def _():
        key_f = lax.broadcasted_iota(jnp.int32, (blk, blk), 0).astype(F32)
        for hh, cs in heads:
            bias_ref[hh] = slope2(hh) * key_f
            vt_ref[hh, HEAD_DIM:, :] = jnp.ones((MOBA_ONES_ROWS, n_blocks * blk), BF16)
            for j in range(n_blocks):
                rs = slice(j * blk, (j + 1) * blk)
                km_ref[hh, j:j + 1, :] = jnp.sum(kb_ref[rs, cs].astype(F32), axis=0,
                                                 keepdims=True) * (1.0 / blk)
                vt_ref[hh, :HEAD_DIM, rs] = vb_ref[rs, cs].astype(F32).T.astype(BF16)

    _sample_pv_accumulate(sample_g, p_ref, pown_ref, vnew_ref, v_refs, accs_ref)

    blk_i = lax.broadcasted_iota(jnp.int32, (n_blocks, blk), 0)
    causal = (lax.broadcasted_iota(jnp.int32, (blk, blk), 1)
              >= lax.broadcasted_iota(jnp.int32, (blk, blk), 0))
    start = pl.multiple_of(i * blk, blk)

    for hh, cs in heads:
        q = q_ref[:, cs]
        km = km_ref[hh]
        km_hi = km.astype(BF16)
        km_lo = (km - km_hi.astype(F32)).astype(BF16)
        gate = _dot_nt(km_hi, q) + _dot_nt(km_lo, q)
        gate = jnp.where(blk_i < i, gate, -jnp.inf)
        sel_ref[hh] = _top3_mask(gate, blk_i.astype(F32), axis=0)
        t = _dot_nt(kb_ref[pl.ds(start, blk), cs], q) * c1 + bias_ref[hh]
        t = jnp.where(causal, t, NEG_BIG)
        t_ref[hh, i] = t
        m_ref[hh] = jnp.max(t, axis=0, keepdims=True)

    def shift(hh, j):
        return slope2(hh) * lax.convert_element_type((i - j) * blk, F32)

    def pass1(j0, nb):
        for hh, cs in heads:
            m = m_ref[hh]
            for j in [j0 + d for d in range(nb)]:
                off = pl.multiple_of(j * blk, blk)
                t = _dot_nt(kb_ref[pl.ds(off, blk), cs], q_ref[:, cs]) * c1 + bias_ref[hh]
                t_ref[hh, j] = t
                picked = sel_ref[hh, pl.ds(j, 1), :] > 0.0
                m_blk = jnp.max(t, axis=0, keepdims=True) - shift(hh, j)
                m = jnp.where(picked, jnp.maximum(m, m_blk), m)
            m_ref[hh] = m

    def pass2(j0, nb):
        off = pl.multiple_of(j0 * blk, blk)
        for hh, cs in heads:
            ps = []
            for j in [j0 + d for d in range(nb)]:
                picked = sel_ref[hh, pl.ds(j, 1), :] > 0.0
                sub = jnp.where(picked, m_ref[hh] + shift(hh, j), -NEG_BIG)
                ps.append(jnp.exp2(t_ref[hh, j] - sub).astype(BF16))
            p = ps[0] if nb == 1 else jnp.concatenate(ps, axis=0)
            acc_ref[hh] += _dot(vt_ref[hh, :, pl.ds(off, nb * blk)], p)

    def over_past_blocks(fn):
        done = 0
        un = MOBA_BLOCKS_PER_TRIP
        while un >= 1:
            n_trips = (i - done) // un

            def trip(t, carry, un=un, base=done):
                fn(base + t * un, un)
                return carry

            lax.fori_loop(0, n_trips, trip, 0)
            done = done + n_trips * un
            un //= 2

    over_past_blocks(pass1)
    for hh, cs in heads:
        p = jnp.exp2(t_ref[hh, i] - m_ref[hh])
        acc_ref[hh] = _dot(vt_ref[hh, :, pl.ds(start, blk)], p.astype(BF16))
    over_past_blocks(pass2)
    for hh, cs in heads:
        acc = acc_ref[hh]
        o_ref[:, cs] = (acc[:HEAD_DIM] / acc[HEAD_DIM:HEAD_DIM + 1]).T.astype(BF16)

    @pl.when(sample_g == steps_per_sample - 1)
    def _():
        _sample_pv_finish(l_ref, os_ref, accs_ref, t_q)


def _moba_prompt(q, kb, vb, slopes, bsz, seq, page_table, p, pown, l, vnew_pad, cache_v_pages, t_q):
    n_blocks = seq // MOBA_BLOCK
    blk = MOBA_BLOCK
    hps = MOBA_HEADS_PER_STEP
    n_hg = N_HEADS // hps
    width = hps * HEAD_DIM
    s_bsz, rows, _ = p.shape
    n_pages = page_table.shape[1]
    steps_per_sample = (bsz * n_hg * n_blocks) // s_bsz
    assert steps_per_sample * s_bsz == bsz * n_hg * n_blocks
    assert steps_per_sample * PAGES_PER_STEP == n_pages, "every V page is visited exactly once"
    step_keys = PAGES_PER_STEP * PAGE_SIZE

    def sample_step(b, hg, i):
        step = (b * n_hg + hg) * n_blocks + i
        return step // steps_per_sample, step % steps_per_sample

    def sample_batch(b, hg, i, pt):
        return sample_step(b, hg, i)[0], 0, 0

    def sample_keys(b, hg, i, pt):
        sb, g = sample_step(b, hg, i)
        return sb, 0, g

    qmap = lambda b, hg, i, pt: (b * n_blocks + i, hg)
    kvmap = lambda b, hg, i, pt: (b, hg)
    once = pl.Buffered(1)
    grid_spec = pltpu.PrefetchScalarGridSpec(
        num_scalar_prefetch=1,
        grid=(bsz, n_hg, n_blocks),
        in_specs=[
            pl.BlockSpec(memory_space=pltpu.SMEM),
            pl.BlockSpec((blk, width), qmap),
            pl.BlockSpec((seq, width), kvmap, pipeline_mode=once),
            pl.BlockSpec((seq, width), kvmap, pipeline_mode=once),
            pl.BlockSpec((None, rows, step_keys), sample_keys),
            pl.BlockSpec((None, rows, LANES), sample_batch),
            pl.BlockSpec((None, rows, LANES), sample_batch),
            pl.BlockSpec((None, LANES, D_ATTN), sample_batch),
        ] + _page_specs(sample_step),
        out_specs=[
            pl.BlockSpec((blk, width), qmap),
            pl.BlockSpec((None, t_q, D_ATTN), sample_batch),
        ],
        scratch_shapes=[
            pltpu.VMEM((hps, n_blocks, HEAD_DIM), F32),
            pltpu.VMEM((hps, HEAD_DIM + MOBA_ONES_ROWS, seq), BF16),
            pltpu.VMEM((hps, blk, blk), F32),
            pltpu.VMEM((hps, n_blocks, blk), F32),
            pltpu.VMEM((hps, 1, blk), F32),
            pltpu.VMEM((hps, n_blocks, blk, blk), F32),
            pltpu.VMEM((hps, HEAD_DIM + MOBA_ONES_ROWS, blk), F32),
            pltpu.VMEM((rows, D_ATTN), F32),
        ],
    )
    return pl.pallas_call(
        functools.partial(_moba_prompt_kernel, n_blocks=n_blocks,
                          steps_per_sample=steps_per_sample, t_q=t_q),
        grid_spec=grid_spec,
        out_shape=[jax.ShapeDtypeStruct((bsz * seq, D_ATTN), BF16),
                   jax.ShapeDtypeStruct((s_bsz, t_q, D_ATTN), BF16)],
        compiler_params=_cparams(3),
        name="moba_prompt",
    )(page_table, slopes, q, kb, vb, p, pown, l, vnew_pad, *([cache_v_pages] * PAGES_PER_STEP))


def _page_heads(page_ref):
    return [page_ref[pl.ds(h, PAGE_SIZE, stride=N_HEADS), :] for h in range(N_HEADS)]


def _load_page(page_ref):
    return jnp.concatenate([s.astype(BF16) for s in _page_heads(page_ref)], axis=1)


def _page_specs(batch_and_step=lambda b, g: (b, g)):
    rows = PAGE_SIZE * N_HEADS

    def spec(pp):
        def index(*args):
            *grid_idx, pt = args
            b, g = batch_and_step(*grid_idx)
            return pt[b, g * PAGES_PER_STEP + pp], 0, 0
        return pl.BlockSpec((None, rows, HEAD_DIM), index)

    return [spec(pp) for pp in range(PAGES_PER_STEP)]


def _sample_scores_kernel(pt_ref, qbd_ref, knew_ref, slope_ref, *rest, t_q, past_len):
    del pt_ref
    k_refs = rest[:PAGES_PER_STEP]
    p_ref, pown_ref, l_ref, km_ref = rest[PAGES_PER_STEP:]
    g = pl.program_id(1)
    n_steps = pl.num_programs(1)
    rows = qbd_ref.shape[0]
    blk = MOBA_BLOCK
    pages_per_block = blk // PAGE_SIZE
    n_past_blocks = past_len // blk
    qbd = qbd_ref[...]
    lane = lax.broadcasted_iota(jnp.int32, (rows, LANES), 1)
    lane_f = lane.astype(F32)

    @pl.when(g == 0)
    def _():
        km_ref[n_past_blocks:, :] = jnp.zeros((LANES - n_past_blocks, D_ATTN), F32)

    ksum = None
    km_rows = []
    for pp in range(PAGES_PER_STEP):
        heads = _page_heads(k_refs[pp])
        kpage = jnp.concatenate([s.astype(BF16) for s in heads], axis=1)
        lg = _dot_nt(qbd, kpage)
        off = pl.multiple_of((g * PAGES_PER_STEP + pp) * PAGE_SIZE, PAGE_SIZE)
        p_ref[:, pl.ds(off, PAGE_SIZE)] = lg
        psum = jnp.concatenate(
            [jnp.sum(s.reshape(PAGE_SIZE // SUBLANES, SUBLANES, HEAD_DIM), axis=0) for s in heads],
            axis=1)
        ksum = psum if pp % pages_per_block == 0 else ksum + psum
        if pp % pages_per_block == pages_per_block - 1:
            km_rows.append(jnp.sum(ksum, axis=0, keepdims=True) * (1.0 / blk))
    step_blocks = PAGES_PER_STEP // pages_per_block
    km_ref[pl.ds(pl.multiple_of(g * step_blocks, step_blocks), step_blocks), :] = (
        jnp.concatenate(km_rows, axis=0))

    @pl.when(g == n_steps - 1)
    def _():
        gate = _dot_nt(qbd, km_ref[...].astype(BF16))
        c1 = (HEAD_DIM ** -0.5) * LOG2E
        slope2 = slope_ref[...] * LOG2E
        t_row = (lax.broadcasted_iota(jnp.int32, (rows, 1), 0) % t_q).astype(F32)
        sel = _top3_mask(jnp.where(lane < n_past_blocks, gate, -jnp.inf), lane_f)

        s_own = _dot_nt(qbd, knew_ref[...]) * c1 + slope2 * lane_f
        s_own = jnp.where(lane_f <= t_row, s_own, NEG_BIG)

        cb = SAMPLE_CHUNK_BLOCKS
        ch = cb * blk
        n_chunks = n_past_blocks // cb
        assert cb & (cb - 1) == 0 and blk & (blk - 1) == 0
        widen = jnp.where(
            jnp.bitwise_and(lax.broadcasted_iota(jnp.int32, (LANES, ch), 0), cb - 1)
            == jnp.right_shift(lax.broadcasted_iota(jnp.int32, (LANES, ch), 1), blk.bit_length() - 1),
            1.0, 0.0).astype(BF16)
        key_f = lax.broadcasted_iota(jnp.int32, (1, ch), 1).astype(F32)

        mx = jnp.full((rows, blk), NEG_BIG, F32)
        for c in range(n_chunks):
            sel_c = jnp.where((lane >= c * cb) & (lane < (c + 1) * cb), sel, 0.0).astype(BF16)
            picked = _dot(sel_c, widen)
            s = p_ref[:, c * ch:(c + 1) * ch] * c1 + slope2 * (key_f + float(c * ch - past_len))
            s = jnp.where(picked > 0.0, s, NEG_BIG)
            p_ref[:, c * ch:(c + 1) * ch] = s
            for b in range(cb):
                mx = jnp.maximum(mx, s[:, b * blk:(b + 1) * blk])
        m = jnp.maximum(jnp.max(mx, axis=1, keepdims=True), jnp.max(s_own, axis=1, keepdims=True))

        p_own = jnp.exp2(s_own - m)
        pown_ref[...] = p_own
        acc = jnp.zeros((rows, blk), F32)
        for c in range(n_chunks):
            pc = jnp.exp2(p_ref[:, c * ch:(c + 1) * ch] - m)
            p_ref[:, c * ch:(c + 1) * ch] = pc
            for b in range(cb):
                acc = acc + pc[:, b * blk:(b + 1) * blk]
        l = jnp.sum(acc, axis=1, keepdims=True) + jnp.sum(p_own, axis=1, keepdims=True)
        l_ref[...] = jnp.broadcast_to(l, l_ref.shape)


def _sample_scores(page_table, qbd, knew_pad, slope_rows, cache_k_pages, t_q, past_len):
    bsz, rows, _ = qbd.shape
    n_pages = page_table.shape[1]
    n_steps = n_pages // PAGES_PER_STEP
    grid_spec = pltpu.PrefetchScalarGridSpec(
        num_scalar_prefetch=1,
        grid=(bsz, n_steps),
        in_specs=[
            pl.BlockSpec((None, rows, D_ATTN), lambda b, g, pt: (b, 0, 0)),
            pl.BlockSpec((None, LANES, D_ATTN), lambda b, g, pt: (b, 0, 0)),
            pl.BlockSpec((rows, 1), lambda b, g, pt: (0, 0)),
        ] + _page_specs(),
        out_specs=[
            pl.BlockSpec((None, rows, past_len), lambda b, g, pt: (b, 0, 0)),
            pl.BlockSpec((None, rows, LANES), lambda b, g, pt: (b, 0, 0)),
            pl.BlockSpec((None, rows, LANES), lambda b, g, pt: (b, 0, 0)),
        ],
        scratch_shapes=[pltpu.VMEM((LANES, D_ATTN), F32)],
    )
    return pl.pallas_call(
        functools.partial(_sample_scores_kernel, t_q=t_q, past_len=past_len),
        grid_spec=grid_spec,
        out_shape=[
            jax.ShapeDtypeStruct((bsz, rows, past_len), F32),
            jax.ShapeDtypeStruct((bsz, rows, LANES), F32),
            jax.ShapeDtypeStruct((bsz, rows, LANES), F32),
        ],
        compiler_params=_cparams(2),
        name="sample_scores",
    )(page_table, qbd, knew_pad, slope_rows, *([cache_k_pages] * PAGES_PER_STEP))


def _sample_pv_accumulate(g, p_ref, pown_ref, vnew_ref, v_refs, acc_ref):
    own = _dot(pown_ref[...].astype(BF16), vnew_ref[...])
    acc = jnp.where(g == 0, own, acc_ref[...])
    for pp in range(PAGES_PER_STEP):
        acc = acc + _dot(p_ref[:, pp * PAGE_SIZE:(pp + 1) * PAGE_SIZE].astype(BF16),
                         _load_page(v_refs[pp]))
    acc_ref[...] = acc


def _sample_pv_finish(l_ref, o_ref, acc_ref, t_q):
    for h in range(N_HEADS):
        rs = slice(h * t_q, (h + 1) * t_q)
        cs = slice(h * HEAD_DIM, (h + 1) * HEAD_DIM)
        o_ref[:, cs] = (acc_ref[rs, cs] / l_ref[rs, 0:1]).astype(BF16)


def _mix_kernel(u_ref, vz_ref, ob_ref, ga_ref, gb_ref, x_ref, ws_ref, bst_ref, wbr_ref, wout_ref,
                nffn_ref, h_ref, hn_ref, oa_ref):
    tm = u_ref.shape[0]
    chunk = ws_ref.shape[1]
    causal = (lax.broadcasted_iota(jnp.int32, (chunk, chunk), 0)
              >= lax.broadcasted_iota(jnp.int32, (chunk, chunk), 1))
    for g in range(GMLP_GROUPS):
        cs = slice(g * GMLP_GROUP_DIM, (g + 1) * GMLP_GROUP_DIM)
        w_g = jnp.where(causal, ws_ref[g], 0.0).astype(BF16)
        b_g = bst_ref[:, g:g + 1]
        for c in range(tm // chunk):
            rs = slice(c * chunk, (c + 1) * chunk)
            mixed = _dot(w_g, vz_ref[rs, cs].astype(BF16)) + b_g
            oa_ref[rs, cs] = (u_ref[rs, cs].astype(F32) * mixed).astype(BF16)
    merged = (ga_ref[...].astype(F32) * _dot(oa_ref[...], wbr_ref[0])
              + gb_ref[...].astype(F32) * _dot(ob_ref[...], wbr_ref[1]))
    h = x_ref[...] + _dot(merged.astype(BF16), wout_ref[...])
    h_ref[...] = h
    hn_ref[...] = _rms(h, nffn_ref[...]).astype(BF16)


def _mix(u, vz, ob, ga, gb, x, ws_chunk, bs_t, w_br_b, w_out_b, norm_ffn, tm):
    m_rows = x.shape[0]
    chunk = ws_chunk.shape[1]
    row = lambda m: (m, 0)
    const2 = lambda m: (0, 0)
    const3 = lambda m: (0, 0, 0)
    once = pl.Buffered(1)
    return pl.pallas_call(
        _mix_kernel,
        grid=(m_rows // tm,),
        in_specs=[
            pl.BlockSpec((tm, D_GMLP), row),
            pl.BlockSpec((tm, D_GMLP), row),
            pl.BlockSpec((tm, D_ATTN), row),
            pl.BlockSpec((tm, D_MODEL), row),
            pl.BlockSpec((tm, D_MODEL), row),
            pl.BlockSpec((tm, D_MODEL), row),
            pl.BlockSpec((GMLP_GROUPS, chunk, chunk), const3, pipeline_mode=once),
            pl.BlockSpec((chunk, GMLP_GROUPS), const2, pipeline_mode=once),
            pl.BlockSpec((2, D_GMLP, D_MODEL), const3, pipeline_mode=once),
            pl.BlockSpec((D_MODEL, D_MODEL), const2, pipeline_mode=once),
            pl.BlockSpec((1, D_MODEL), const2, pipeline_mode=once),
        ],
        out_specs=[pl.BlockSpec((tm, D_MODEL), row), pl.BlockSpec((tm, D_MODEL), row)],
        out_shape=[jax.ShapeDtypeStruct((m_rows, D_MODEL), F32),
                   jax.ShapeDtypeStruct((m_rows, D_MODEL), BF16)],
        scratch_shapes=[pltpu.VMEM((tm, D_GMLP), BF16)],
        compiler_params=_cparams(1),
        name="mix",
    )(u, vz, ob, ga, gb, x, ws_chunk, bs_t, w_br_b, w_out_b, norm_ffn.reshape(1, D_MODEL))


def _ffn_tile(hn, wa, wb, wd, wc, bc, history):
    a = _dot(hn, wa)
    b = _dot(hn, wb)
    a1, a2 = history(a, pltpu.roll(a, 1, 0), pltpu.roll(a, 2, 0))
    conv = bc + a * wc[2:3, :] + a2 * wc[0:1, :] + a1 * wc[1:2, :]
    act = (jax.nn.gelu(conv) * b).astype(BF16)
    return a, _dot(act, wd)


def _ffn_kernel(hn_ref, wa_ref, wb_ref, wc_ref, bc_ref, wd_ref, h_ref, nfin_ref, init_ref,
                hns_ref, hs_ref, inits_ref, y_ref, tail_ref, ys_ref, tails_ref, halo_ref,
                *, tiles_per_seq, sample_seq):
    m = pl.program_id(0)
    n = pl.program_id(1)
    last_n = pl.num_programs(1) - 1
    tm = hn_ref.shape[0]
    tf = wa_ref.shape[1]

    @pl.when(n == 0)
    def _():
        y_ref[...] = h_ref[...]

    wa = wa_ref[...].astype(BF16)
    wb = wb_ref[...].astype(BF16)
    wd = wd_ref[...].astype(BF16)
    wc = wc_ref[...]
    bc = bc_ref[...]

    def carried_history(a, r1, r2):
        row = lax.broadcasted_iota(jnp.int32, (tm, tf), 0)
        prev = jnp.where(m % tiles_per_seq == 0, init_ref[0], halo_ref[n])
        p1 = prev[SUBLANES - 1:SUBLANES, :]
        p2 = prev[SUBLANES - 2:SUBLANES - 1, :]
        return (jnp.where(row == 0, p1, r1),
                jnp.where(row == 0, p2, jnp.where(row == 1, p1, r2)))

    a, part = _ffn_tile(hn_ref[...], wa, wb, wd, wc, bc, carried_history)
    last = a[tm - SUBLANES:, :]
    halo_ref[n] = last
    tail_ref[...] = last
    y_ref[...] += part

    @pl.when(n == last_n)
    def _():
        y_ref[...] = _rms(y_ref[...], nfin_ref[...])

    @pl.when(m == 0)
    def _():
        @pl.when(n == 0)
        def _():
            ys_ref[...] = hs_ref[...]

        def given_history(a_s, r1, r2):
            pos = lax.broadcasted_iota(jnp.int32, a_s.shape, 0) % sample_seq
            return jnp.where(pos >= 1, r1, inits_ref[0]), jnp.where(pos >= 2, r2, inits_ref[1])

        a_s, part_s = _ffn_tile(hns_ref[...], wa, wb, wd, wc, bc, given_history)
        tails_ref[...] = a_s
        ys_ref[...] += part_s

        @pl.when(n == last_n)
        def _():
            ys_ref[...] = _rms(ys_ref[...], nfin_ref[...])


def _ffn(hn, h, hn_s, h_s, w_up, w_conv, b_conv, w_down, norm_final, init, init_s, tm, seq_len,
         sample_seq):
    m_rows = hn.shape[0]
    s_rows = hn_s.shape[0]
    tf = FF_TILE
    n_ff = D_FF // tf
    tiles_per_seq = seq_len // tm
    row = lambda m, n: (m, 0)
    const = lambda m, n: (0, 0)
    sample_tile = lambda m, n: jnp.where(m == 0, n, n_ff - 1)
    once = pl.Buffered(1)
    return pl.pallas_call(
        functools.partial(_ffn_kernel, tiles_per_seq=tiles_per_seq, sample_seq=sample_seq),
        grid=(m_rows // tm, n_ff),
        in_specs=[
            pl.BlockSpec((tm, D_MODEL), row, pipeline_mode=once),
            pl.BlockSpec((D_MODEL, tf), lambda m, n: (0, n)),
            pl.BlockSpec((D_MODEL, tf), lambda m, n: (0, n_ff + n)),
            pl.BlockSpec((CONV_W, tf), lambda m, n: (0, n)),
            pl.BlockSpec((1, tf), lambda m, n: (0, n)),
            pl.BlockSpec((tf, D_MODEL), lambda m, n: (n, 0)),
            pl.BlockSpec((tm, D_MODEL), row, pipeline_mode=once),
            pl.BlockSpec((1, D_MODEL), const),
            pl.BlockSpec((1, SUBLANES, tf), lambda m, n: (0, m // tiles_per_seq, n)),
            pl.BlockSpec((s_rows, D_MODEL), const),
            pl.BlockSpec((s_rows, D_MODEL), const),
            pl.BlockSpec((2, s_rows, tf), lambda m, n: (0, 0, sample_tile(m, n))),
        ],
        out_specs=[
            pl.BlockSpec((tm, D_MODEL), row, pipeline_mode=once),
            pl.BlockSpec((SUBLANES, tf), lambda m, n: (m, n)),
            pl.BlockSpec((s_rows, D_MODEL), const),
            pl.BlockSpec((s_rows, tf), lambda m, n: (0, sample_tile(m, n))),
        ],
        out_shape=[
            jax.ShapeDtypeStruct((m_rows, D_MODEL), F32),
            jax.ShapeDtypeStruct(((m_rows // tm) * SUBLANES, D_FF), F32),
            jax.ShapeDtypeStruct((s_rows, D_MODEL), F32),
            jax.ShapeDtypeStruct((s_rows, D_FF), F32),
        ],
        scratch_shapes=[pltpu.VMEM((n_ff, SUBLANES, tf), F32)],
        compiler_params=_cparams(2),
        name="ffn",
    )(hn, w_up, w_up, w_conv, b_conv.reshape(1, D_FF), w_down, h,
      norm_final.reshape(1, D_MODEL), init, hn_s, h_s, init_s)


def _alibi_slopes():
    return jnp.exp2(-8.0 * jnp.arange(1, N_HEADS + 1, dtype=F32) / N_HEADS)


def kernel(x_prompt, x_sample, cache_k, cache_v, state_ffn_conv, page_table, norm_mix, w_in,
           norm_gmlp_v, w_spatial, b_spatial, w_branch, w_out, norm_ffn, w_up, w_conv, b_conv,
           w_down, norm_final):
    assert w_in.shape[0] == 1, "single layer"
    bp, seq, _ = x_prompt.shape
    bs, t_q, _ = x_sample.shape
    n_pages = page_table.shape[1]
    past_len = n_pages * PAGE_SIZE
    assert seq % MOBA_BLOCK == 0 and past_len % MOBA_BLOCK == 0 and past_len % CHUNK == 0
    assert t_q <= SUBLANES and past_len // MOBA_BLOCK <= LANES and seq // MOBA_BLOCK <= LANES

    slopes = _alibi_slopes()
    w_br_b = w_branch[0].astype(BF16)
    w_out_b = w_out[0].astype(BF16)
    ws, bsp = w_spatial[0], b_spatial[0]

    m_s = bs * t_q
    xp = x_prompt.reshape(bp * seq, D_MODEL)
    xs = x_sample.reshape(m_s, D_MODEL)
    xn_p = _norm_bf16(xp, norm_mix[0], PROMPT_TM_NORM)
    xn_s = _norm_bf16(xs, norm_mix[0], m_s)
    (u, vz, q, k, kb, v, vb, ga, gb), (us, vzs, qs, ks, kbs, vs, vbs, gas, gbs) = _inproj(
        xn_p, xn_s, w_in[0], norm_gmlp_v[0], PROMPT_TM_INPROJ)

    q4 = qs.reshape(bs, t_q, N_HEADS, HEAD_DIM)
    eye = jnp.eye(N_HEADS, dtype=BF16)
    qbd = (q4.transpose(0, 2, 1, 3)[:, :, :, None, :] * eye[None, :, None, :, None]
           ).reshape(bs, N_HEADS * t_q, D_ATTN)
    pad_rows = ((0, 0), (0, LANES - t_q), (0, 0))
    knew = jnp.pad(kbs.reshape(bs, t_q, D_ATTN), pad_rows)
    vnew = jnp.pad(vbs.reshape(bs, t_q, D_ATTN), pad_rows)
    slope_rows = jnp.repeat(slopes, t_q).reshape(N_HEADS * t_q, 1)
    ck = cache_k[0].reshape(-1, PAGE_SIZE * N_HEADS, HEAD_DIM)
    cv = cache_v[0].reshape(-1, PAGE_SIZE * N_HEADS, HEAD_DIM)
    p, pown, l = _sample_scores(page_table, qbd, knew, slope_rows, ck, t_q, past_len)

    ob, obs = _moba_prompt(q, kb, vb, slopes, bp, seq, page_table, p, pown, l, vnew, cv, t_q)
    obs = obs.reshape(m_s, D_ATTN)
    h, hn = _mix(u, vz, ob, ga, gb, xp, ws, bsp.T, w_br_b, w_out_b, norm_ffn[0], PROMPT_TM_MIX)
    ws_s = (jnp.eye(bs, dtype=F32)[None, :, None, :, None]
            * ws[:, None, :t_q, None, :t_q]).reshape(GMLP_GROUPS, m_s, m_s)
    bs_t_s = jnp.tile(bsp[:, :t_q], (1, bs)).T
    hs, hns = _mix(us, vzs, obs, gas, gbs, xs, ws_s, bs_t_s, w_br_b, w_out_b, norm_ffn[0], m_s)
    st = state_ffn_conv[0]
    zero_rows = jnp.zeros((bs, t_q - 1, D_FF), F32)
    init1 = jnp.concatenate([st[:, 1:2], zero_rows], axis=1)
    init2 = jnp.concatenate([st[:, 0:1], st[:, 1:2], zero_rows[:, 1:]], axis=1)
    init_s = jnp.stack([init1.reshape(m_s, D_FF), init2.reshape(m_s, D_FF)])

    zero_state = jnp.zeros((1, bp * SUBLANES, D_FF), F32)
    tm_p = PROMPT_TM_FFN
    yp, tail_p, ys, tail_s = _ffn(hn, h, hns, hs, w_up[0], w_conv[0], b_conv[0], w_down[0],
                                  norm_final, zero_state, init_s, tm_p, seq, t_q)

    y_prompt = yp.reshape(bp, seq, D_MODEL)
    gv_p = vz.reshape(bp, seq, D_GMLP)[:, seq - CHUNK:][None]
    k_p = k.reshape(1, bp, seq, N_HEADS, HEAD_DIM)
    v_p = v.reshape(1, bp, seq, N_HEADS, HEAD_DIM)
    c_p = tail_p.reshape(bp, seq // tm_p, SUBLANES, D_FF)[:, -1, SUBLANES - (CONV_W - 1):][None]
    y_sample = ys.reshape(bs, t_q, D_MODEL)
    gv_s = vzs.reshape(1, bs, t_q, D_GMLP)
    k_s = ks.reshape(1, bs, t_q, N_HEADS, HEAD_DIM)
    v_s = vs.reshape(1, bs, t_q, N_HEADS, HEAD_DIM)
    c_s = tail_s.reshape(bs, t_q, D_FF)[:, t_q - (CONV_W - 1):][None]

    return (y_prompt, y_sample, gv_p, gv_s, k_p, v_p, k_s, v_s, c_p, c_s)
```

```python
import functools

import jax
import jax.numpy as jnp
from jax import lax
from jax.experimental import pallas as pl
from jax.experimental.pallas import tpu as pltpu

F32 = jnp.float32
BF16 = jnp.bfloat16

D_MODEL = 2048
D_GMLP = D_MODEL // 2
GMLP_GROUPS = 8
GMLP_GROUP_DIM = D_GMLP // GMLP_GROUPS
CHUNK = 128
D_ATTN = D_MODEL // 2
HEAD_DIM = 128
N_HEADS = D_ATTN // HEAD_DIM
MOBA_BLOCK = 256
MOBA_TOPK = 3
D_FF = 5632
CONV_W = 3
RMS_EPS = 1e-6
PAGE_SIZE = 128
N_SECTIONS = 9

LANES = 128
SUBLANES = 8
VMEM_LIMIT_BYTES = 56 * 1024 * 1024

NEG_BIG = -1e30
LOG2E = 1.4426950408889634
MOBA_HEADS_PER_STEP = 4
MOBA_BLOCKS_PER_TRIP = 8
MOBA_ONES_ROWS = 16
SAMPLE_CHUNK_BLOCKS = 8
FF_TILE = 512
PAGES_PER_STEP = 16
PROMPT_TM_NORM = 1024
PROMPT_TM_INPROJ = 512
PROMPT_TM_MIX = 256
PROMPT_TM_FFN = 1024

_NT = (((1,), (1,)), ((), ()))


def _dot(a, b):
    return jnp.dot(a, b, preferred_element_type=F32)


def _dot_nt(a, b):
    return lax.dot_general(a, b, _NT, preferred_element_type=F32)


def _rms(x, g):
    return x * lax.rsqrt(jnp.mean(x * x, axis=-1, keepdims=True) + RMS_EPS) * g


def _sigmoid(x):
    return 0.5 * (jnp.tanh(0.5 * x) + 1.0)


def _cparams(n_axes):
    return pltpu.CompilerParams(dimension_semantics=("arbitrary",) * n_axes,
                                vmem_limit_bytes=VMEM_LIMIT_BYTES)


def _top3_mask(gate, idx_f, axis=1):
    sel = jnp.zeros_like(gate)
    for _ in range(MOBA_TOPK):
        mx = jnp.max(gate, axis=axis, keepdims=True)
        first = jnp.min(jnp.where(gate == mx, idx_f, float(LANES)), axis=axis, keepdims=True)
        pick = (idx_f == first) & (mx > -jnp.inf)
        sel = jnp.where(pick, 1.0, sel)
        gate = jnp.where(pick, -jnp.inf, gate)
    return sel


def _store_heads(dst_ref, dst_b_ref, acc):
    tm = acc.shape[0]
    dst_b_ref[...] = acc.astype(BF16)
    for h in range(N_HEADS):
        dst_ref[pl.ds(h, tm, stride=N_HEADS), :] = acc[:, h * HEAD_DIM:(h + 1) * HEAD_DIM]


def _norm_kernel(x_ref, g_ref, o_ref):
    o_ref[...] = _rms(x_ref[...], g_ref[...]).astype(BF16)


def _norm_bf16(x, g, tm):
    m_rows, d = x.shape
    return pl.pallas_call(
        _norm_kernel,
        grid=(m_rows // tm,),
        in_specs=[pl.BlockSpec((tm, d), lambda m: (m, 0)), pl.BlockSpec((1, d), lambda m: (0, 0))],
        out_specs=pl.BlockSpec((tm, d), lambda m: (m, 0)),
        out_shape=jax.ShapeDtypeStruct((m_rows, d), BF16),
        compiler_params=_cparams(1),
        name="norm",
    )(x, g.reshape(1, d))


N_INPROJ_OUTS = 9


def _inproj_kernel(xn_ref, xs_ref, w_ref, ngv_ref, *rest):
    prompt = rest[:N_INPROJ_OUTS]
    sample = rest[N_INPROJ_OUTS:2 * N_INPROJ_OUTS]
    wb_ref = rest[2 * N_INPROJ_OUTS]
    n = pl.program_id(0)
    first_tile = pl.program_id(1) == 0

    @pl.when(first_tile)
    def _():
        wb_ref[...] = w_ref[...].astype(BF16)

    def emit(pred, epilogue):
        @pl.when(pred)
        def _():
            epilogue(_dot(xn_ref[...], wb_ref[...]), prompt)

            @pl.when(first_tile)
            def _():
                epilogue(_dot(xs_ref[...], wb_ref[...]), sample)

    def gelu_u(acc, outs):
        outs[0][...] = jax.nn.gelu(acc).astype(BF16)

    def gelu_norm_v(acc, outs):
        outs[1][...] = _rms(jax.nn.gelu(acc), ngv_ref[...])

    def plain_q(acc, outs):
        outs[2][...] = acc.astype(BF16)

    def heads_k(acc, outs):
        _store_heads(outs[3], outs[4], acc)

    def heads_v(acc, outs):
        _store_heads(outs[5], outs[6], acc)

    def gate_a(acc, outs):
        outs[7][...] = _sigmoid(acc).astype(BF16)

    def gate_b(acc, outs):
        outs[8][...] = _sigmoid(acc).astype(BF16)

    emit(n == 0, gelu_u)
    emit(n == 1, gelu_norm_v)
    emit(n == 2, plain_q)
    emit(n == 3, heads_k)
    emit(n == 4, heads_v)
    emit((n == 5) | (n == 6), gate_a)
    emit(n >= 7, gate_b)


def _inproj(xn, xn_s, w_in, norm_gmlp_v, tm):
    m_rows = xn.shape[0]
    s_rows = xn_s.shape[0]
    sec = D_GMLP
    last = m_rows // tm - 1

    def rows(first, count):
        def index(n, m):
            return jnp.where(n < first, 0, jnp.where(n >= first + count, last, m))
        return index

    def col(first, count):
        return lambda n: jnp.clip(n - first, 0, count - 1)

    def out(first, count=1):
        r, c = rows(first, count), col(first, count)
        return pl.BlockSpec((tm, sec), lambda n, m: (r(n, m), c(n)))

    def out_heads(first):
        r = rows(first, 1)
        return pl.BlockSpec((tm * N_HEADS, HEAD_DIM), lambda n, m: (r(n, m), 0))

    def out_s(first, count=1):
        c = col(first, count)
        return pl.BlockSpec((s_rows, sec), lambda n, m: (0, c(n)))

    out_heads_s = pl.BlockSpec((s_rows * N_HEADS, HEAD_DIM), lambda n, m: (0, 0))

    def shapes(r):
        return [
            jax.ShapeDtypeStruct((r, sec), BF16),
            jax.ShapeDtypeStruct((r, sec), F32),
            jax.ShapeDtypeStruct((r, sec), BF16),
            jax.ShapeDtypeStruct((r * N_HEADS, HEAD_DIM), F32),
            jax.ShapeDtypeStruct((r, sec), BF16),
            jax.ShapeDtypeStruct((r * N_HEADS, HEAD_DIM), F32),
            jax.ShapeDtypeStruct((r, sec), BF16),
            jax.ShapeDtypeStruct((r, D_MODEL), BF16),
            jax.ShapeDtypeStruct((r, D_MODEL), BF16),
        ]

    in_specs = [
        pl.BlockSpec((tm, D_MODEL), lambda n, m: (m, 0)),
        pl.BlockSpec((s_rows, D_MODEL), lambda n, m: (0, 0)),
        pl.BlockSpec((D_MODEL, sec), lambda n, m: (0, n)),
        pl.BlockSpec((1, sec), lambda n, m: (0, 0)),
    ]
    out_specs = [out(0), out(1), out(2), out_heads(3), out(3), out_heads(4), out(4), out(5, 2),
                 out(7, 2),
                 out_s(0), out_s(1), out_s(2), out_heads_s, out_s(3), out_heads_s, out_s(4),
                 out_s(5, 2), out_s(7, 2)]
    outs = pl.pallas_call(
        _inproj_kernel,
        grid=(N_SECTIONS, m_rows // tm),
        in_specs=in_specs,
        out_specs=out_specs,
        out_shape=shapes(m_rows) + shapes(s_rows),
        scratch_shapes=[pltpu.VMEM((D_MODEL, sec), BF16)],
        compiler_params=_cparams(2),
        name="inproj",
    )(xn, xn_s, w_in, norm_gmlp_v.reshape(1, sec))
    return outs[:N_INPROJ_OUTS], outs[N_INPROJ_OUTS:]


def _moba_prompt_kernel(pt_ref, slopes_ref, q_ref, kb_ref, vb_ref, p_ref, pown_ref, l_ref, vnew_ref,
                        *rest, n_blocks, steps_per_sample, t_q):
    del pt_ref
    v_refs = rest[:PAGES_PER_STEP]
    (o_ref, os_ref, km_ref, vt_ref, bias_ref, sel_ref, m_ref, t_ref, acc_ref,
     accs_ref) = rest[PAGES_PER_STEP:]
    hg = pl.program_id(1)
    i = pl.program_id(2)
    step = (pl.program_id(0) * pl.num_programs(1) + hg) * n_blocks + i
    sample_g = step % steps_per_sample
    blk = MOBA_BLOCK

    @pl.when(step == 0)
    def _():
        accs_ref[...] = jnp.zeros_like(accs_ref)
    c1 = (HEAD_DIM ** -0.5) * LOG2E
    heads = [(hh, slice(hh * HEAD_DIM, (hh + 1) * HEAD_DIM)) for hh in range(MOBA_HEADS_PER_STEP)]

    def slope2(hh):
        return slopes_ref[hg * MOBA_HEADS_PER_STEP + hh] * LOG2E

    @pl.when(i == 0)
    def _():
        key_f = lax.broadcasted_iota(jnp.int32, (blk, blk), 0).astype(F32)
        for hh, cs in heads:
            bias_ref[hh] = slope2(hh) * key_f
            vt_ref[hh, HEAD_DIM:, :] = jnp.ones((MOBA_ONES_ROWS, n_blocks * blk), BF16)
            for j in range(n_blocks):
                rs = slice(j * blk, (j + 1) * blk)
                km_ref[hh, j:j + 1, :] = jnp.sum(kb_ref[rs, cs].astype(F32), axis=0,
                                                 keepdims=True) * (1.0 / blk)
                vt_ref[hh, :HEAD_DIM, rs] = vb_ref[rs, cs].astype(F32).T.astype(BF16)

    _sample_pv_accumulate(sample_g, p_ref, pown_ref, vnew_ref, v_refs, accs_ref)

    blk_i = lax.broadcasted_iota(jnp.int32, (n_blocks, blk), 0)
    causal = (lax.broadcasted_iota(jnp.int32, (blk, blk), 1)
              >= lax.broadcasted_iota(jnp.int32, (blk, blk), 0))
    start = pl.multiple_of(i * blk, blk)

    for hh, cs in heads:
        q = q_ref[:, cs]
        km = km_ref[hh]
        km_hi = km.astype(BF16)
        km_lo = (km - km_hi.astype(F32)).astype(BF16)
        gate = _dot_nt(km_hi, q) + _dot_nt(km_lo, q)
        gate = jnp.where(blk_i < i, gate, -jnp.inf)
        sel_ref[hh] = _top3_mask(gate, blk_i.astype(F32), axis=0)
        t = _dot_nt(kb_ref[pl.ds(start, blk), cs], q) * c1 + bias_ref[hh]
        t = jnp.where(causal, t, NEG_BIG)
        t_ref[hh, i] = t
        m_ref[hh] = jnp.max(t, axis=0, keepdims=True)

    def shift(hh, j):
        return slope2(hh) * lax.convert_element_type((i - j) * blk, F32)

    def pass1(j0, nb):
        for hh, cs in heads:
            m = m_ref[hh]
            for j in [j0 + d for d in range(nb)]:
                off = pl.multiple_of(j * blk, blk)
                t = _dot_nt(kb_ref[pl.ds(off, blk), cs], q_ref[:, cs]) * c1 + bias_ref[hh]
                t_ref[hh, j] = t
                picked = sel_ref[hh, pl.ds(j, 1), :] > 0.0
                m_blk = jnp.max(t, axis=0, keepdims=True) - shift(hh, j)
                m = jnp.where(picked, jnp.maximum(m, m_blk), m)
            m_ref[hh] = m

    def pass2(j0, nb):
        off = pl.multiple_of(j0 * blk, blk)
        for hh, cs in heads:
            ps = []
            for j in [j0 + d for d in range(nb)]:
                picked = sel_ref[hh, pl.ds(j, 1), :] > 0.0
                sub = jnp.where(picked, m_ref[hh] + shift(hh, j), -NEG_BIG)
                ps.append(jnp.exp2(t_ref[hh, j] - sub).astype(BF16))
            p = ps[0] if nb == 1 else jnp.concatenate(ps, axis=0)
            acc_ref[hh] += _dot(vt_ref[hh, :, pl.ds(off, nb * blk)], p)

    def over_past_blocks(fn):
        done = 0
        un = MOBA_BLOCKS_PER_TRIP
        while un >= 1:
            n_trips = (i - done) // un

            def trip(t, carry, un=un, base=done):
                fn(base + t * un, un)
                return carry

            lax.fori_loop(0, n_trips, trip, 0)
            done = done + n_trips * un
            un //= 2

    over_past_blocks(pass1)
    for hh, cs in heads:
        p = jnp.exp2(t_ref[hh, i] - m_ref[hh])
        acc_ref[hh] = _dot(vt_ref[hh, :, pl.ds(start, blk)], p.astype(BF16))
    over_past_blocks(pass2)
    for hh, cs in heads:
        acc = acc_ref[hh]
        o_ref[:, cs] = (acc[:HEAD_DIM] / acc[HEAD_DIM:HEAD_DIM + 1]).T.astype(BF16)

    @pl.when(sample_g == steps_per_sample - 1)
    def _():
        _sample_pv_finish(l_ref, os_ref, accs_ref, t_q)


def _moba_prompt(q, kb, vb, slopes, bsz, seq, page_table, p, pown, l, vnew_pad, cache_v_pages, t_q):
    n_blocks = seq // MOBA_BLOCK
    blk = MOBA_BLOCK
    hps = MOBA_HEADS_PER_STEP
    n_hg = N_HEADS // hps
    width = hps * HEAD_DIM
    s_bsz, rows, _ = p.shape
    n_pages = page_table.shape[1]
    steps_per_sample = (bsz * n_hg * n_blocks) // s_bsz
    assert steps_per_sample * s_bsz == bsz * n_hg * n_blocks
    assert steps_per_sample * PAGES_PER_STEP == n_pages, "every V page is visited exactly once"
    step_keys = PAGES_PER_STEP * PAGE_SIZE

    def sample_step(b, hg, i):
        step = (b * n_hg + hg) * n_blocks + i
        return step // steps_per_sample, step % steps_per_sample

    def sample_batch(b, hg, i, pt):
        return sample_step(b, hg, i)[0], 0, 0

    def sample_keys(b, hg, i, pt):
        sb, g = sample_step(b, hg, i)
        return sb, 0, g

    qmap = lambda b, hg, i, pt: (b * n_blocks + i, hg)
    kvmap = lambda b, hg, i, pt: (b, hg)
    once = pl.Buffered(1)
    grid_spec = pltpu.PrefetchScalarGridSpec(
        num_scalar_prefetch=1,
        grid=(bsz, n_hg, n_blocks),
        in_specs=[
            pl.BlockSpec(memory_space=pltpu.SMEM),
            pl.BlockSpec((blk, width), qmap),
            pl.BlockSpec((seq, width), kvmap, pipeline_mode=once),
            pl.BlockSpec((seq, width), kvmap, pipeline_mode=once),
            pl.BlockSpec((None, rows, step_keys), sample_keys),
            pl.BlockSpec((None, rows, LANES), sample_batch),
            pl.BlockSpec((None, rows, LANES), sample_batch),
            pl.BlockSpec((None, LANES, D_ATTN), sample_batch),
        ] + _page_specs(sample_step),
        out_specs=[
            pl.BlockSpec((blk, width), qmap),
            pl.BlockSpec((None, t_q, D_ATTN), sample_batch),
        ],
        scratch_shapes=[
            pltpu.VMEM((hps, n_blocks, HEAD_DIM), F32),
            pltpu.VMEM((hps, HEAD_DIM + MOBA_ONES_ROWS, seq), BF16),
            pltpu.VMEM((hps, blk, blk), F32),
            pltpu.VMEM((hps, n_blocks, blk), F32),
            pltpu.VMEM((hps, 1, blk), F32),
            pltpu.VMEM((hps, n_blocks, blk, blk), F32),
            pltpu.VMEM((hps, HEAD_DIM + MOBA_ONES_ROWS, blk), F32),
            pltpu.VMEM((rows, D_ATTN), F32),
        ],
    )
    return pl.pallas_call(
        functools.partial(_moba_prompt_kernel, n_blocks=n_blocks,
                          steps_per_sample=steps_per_sample, t_q=t_q),
        grid_spec=grid_spec,
        out_shape=[jax.ShapeDtypeStruct((bsz * seq, D_ATTN), BF16),
                   jax.ShapeDtypeStruct((s_bsz, t_q, D_ATTN), BF16)],
        compiler_params=_cparams(3),
        name="moba_prompt",
    )(page_table, slopes, q, kb, vb, p, pown, l, vnew_pad, *([cache_v_pages] * PAGES_PER_STEP))


def _page_heads(page_ref):
    return [page_ref[pl.ds(h, PAGE_SIZE, stride=N_HEADS), :] for h in range(N_HEADS)]


def _load_page(page_ref):
    return jnp.concatenate([s.astype(BF16) for s in _page_heads(page_ref)], axis=1)


def _page_specs(batch_and_step=lambda b, g: (b, g)):
    rows = PAGE_SIZE * N_HEADS

    def spec(pp):
        def index(*args):
            *grid_idx, pt = args
            b, g = batch_and_step(*grid_idx)
            return pt[b, g * PAGES_PER_STEP + pp], 0, 0
        return pl.BlockSpec((None, rows, HEAD_DIM), index)

    return [spec(pp) for pp in range(PAGES_PER_STEP)]


def _sample_scores_kernel(pt_ref, qbd_ref, knew_ref, slope_ref, *rest, t_q, past_len):
    del pt_ref
    k_refs = rest[:PAGES_PER_STEP]
    p_ref, pown_ref, l_ref, km_ref = rest[PAGES_PER_STEP:]
    g = pl.program_id(1)
    n_steps = pl.num_programs(1)
    rows = qbd_ref.shape[0]
    blk = MOBA_BLOCK
    pages_per_block = blk // PAGE_SIZE
    n_past_blocks = past_len // blk
    qbd = qbd_ref[...]
    lane = lax.broadcasted_iota(jnp.int32, (rows, LANES), 1)
    lane_f = lane.astype(F32)

    @pl.when(g == 0)
    def _():
        km_ref[n_past_blocks:, :] = jnp.zeros((LANES - n_past_blocks, D_ATTN), F32)

    ksum = None
    km_rows = []
    for pp in range(PAGES_PER_STEP):
        heads = _page_heads(k_refs[pp])
        kpage = jnp.concatenate([s.astype(BF16) for s in heads], axis=1)
        lg = _dot_nt(qbd, kpage)
        off = pl.multiple_of((g * PAGES_PER_STEP + pp) * PAGE_SIZE, PAGE_SIZE)
        p_ref[:, pl.ds(off, PAGE_SIZE)] = lg
        psum = jnp.concatenate(
            [jnp.sum(s.reshape(PAGE_SIZE // SUBLANES, SUBLANES, HEAD_DIM), axis=0) for s in heads],
            axis=1)
        ksum = psum if pp % pages_per_block == 0 else ksum + psum
        if pp % pages_per_block == pages_per_block - 1:
            km_rows.append(jnp.sum(ksum, axis=0, keepdims=True) * (1.0 / blk))
    step_blocks = PAGES_PER_STEP // pages_per_block
    km_ref[pl.ds(pl.multiple_of(g * step_blocks, step_blocks), step_blocks), :] = (
        jnp.concatenate(km_rows, axis=0))

    @pl.when(g == n_steps - 1)
    def _():
        gate = _dot_nt(qbd, km_ref[...].astype(BF16))
        c1 = (HEAD_DIM ** -0.5) * LOG2E
        slope2 = slope_ref[...] * LOG2E
        t_row = (lax.broadcasted_iota(jnp.int32, (rows, 1), 0) % t_q).astype(F32)
        sel = _top3_mask(jnp.where(lane < n_past_blocks, gate, -jnp.inf), lane_f)

        s_own = _dot_nt(qbd, knew_ref[...]) * c1 + slope2 * lane_f
        s_own = jnp.where(lane_f <= t_row, s_own, NEG_BIG)

        cb = SAMPLE_CHUNK_BLOCKS
        ch = cb * blk
        n_chunks = n_past_blocks // cb
        assert cb & (cb - 1) == 0 and blk & (blk - 1) == 0
        widen = jnp.where(
            jnp.bitwise_and(lax.broadcasted_iota(jnp.int32, (LANES, ch), 0), cb - 1)
            == jnp.right_shift(lax.broadcasted_iota(jnp.int32, (LANES, ch), 1), blk.bit_length() - 1),
            1.0, 0.0).astype(BF16)
        key_f = lax.broadcasted_iota(jnp.int32, (1, ch), 1).astype(F32)

        mx = jnp.full((rows, blk), NEG_BIG, F32)
        for c in range(n_chunks):
            sel_c = jnp.where((lane >= c * cb) & (lane < (c + 1) * cb), sel, 0.0).astype(BF16)
            picked = _dot(sel_c, widen)
            s = p_ref[:, c * ch:(c + 1) * ch] * c1 + slope2 * (key_f + float(c * ch - past_len))
            s = jnp.where(picked > 0.0, s, NEG_BIG)
            p_ref[:, c * ch:(c + 1) * ch] = s
            for b in range(cb):
                mx = jnp.maximum(mx, s[:, b * blk:(b + 1) * blk])
        m = jnp.maximum(jnp.max(mx, axis=1, keepdims=True), jnp.max(s_own, axis=1, keepdims=True))

        p_own = jnp.exp2(s_own - m)
        pown_ref[...] = p_own
        acc = jnp.zeros((rows, blk), F32)
        for c in range(n_chunks):
            pc = jnp.exp2(p_ref[:, c * ch:(c + 1) * ch] - m)
            p_ref[:, c * ch:(c + 1) * ch] = pc
            for b in range(cb):
                acc = acc + pc[:, b * blk:(b + 1) * blk]
        l = jnp.sum(acc, axis=1, keepdims=True) + jnp.sum(p_own, axis=1, keepdims=True)
        l_ref[...] = jnp.broadcast_to(l, l_ref.shape)


def _sample_scores(page_table, qbd, knew_pad, slope_rows, cache_k_pages, t_q, past_len):
    bsz, rows, _ = qbd.shape
    n_pages = page_table.shape[1]
    n_steps = n_pages // PAGES_PER_STEP
    grid_spec = pltpu.PrefetchScalarGridSpec(
        num_scalar_prefetch=1,
        grid=(bsz, n_steps),
        in_specs=[
            pl.BlockSpec((None, rows, D_ATTN), lambda b, g, pt: (b, 0, 0)),
            pl.BlockSpec((None, LANES, D_ATTN), lambda b, g, pt: (b, 0, 0)),
            pl.BlockSpec((rows, 1), lambda b, g, pt: (0, 0)),
        ] + _page_specs(),
        out_specs=[
            pl.BlockSpec((None, rows, past_len), lambda b, g, pt: (b, 0, 0)),
            pl.BlockSpec((None, rows, LANES), lambda b, g, pt: (b, 0, 0)),
            pl.BlockSpec((None, rows, LANES), lambda b, g, pt: (b, 0, 0)),
        ],
        scratch_shapes=[pltpu.VMEM((LANES, D_ATTN), F32)],
    )
    return pl.pallas_call(
        functools.partial(_sample_scores_kernel, t_q=t_q, past_len=past_len),
        grid_spec=grid_spec,
        out_shape=[
            jax.ShapeDtypeStruct((bsz, rows, past_len), F32),
            jax.ShapeDtypeStruct((bsz, rows, LANES), F32),
            jax.ShapeDtypeStruct((bsz, rows, LANES), F32),
        ],
        compiler_params=_cparams(2),
        name="sample_scores",
    )(page_table, qbd, knew_pad, slope_rows, *([cache_k_pages] * PAGES_PER_STEP))


def _sample_pv_accumulate(g, p_ref, pown_ref, vnew_ref, v_refs, acc_ref):
    own = _dot(pown_ref[...].astype(BF16), vnew_ref[...])
    acc = jnp.where(g == 0, own, acc_ref[...])
    for pp in range(PAGES_PER_STEP):
        acc = acc + _dot(p_ref[:, pp * PAGE_SIZE:(pp + 1) * PAGE_SIZE].astype(BF16),
                         _load_page(v_refs[pp]))
    acc_ref[...] = acc


def _sample_pv_finish(l_ref, o_ref, acc_ref, t_q):
    for h in range(N_HEADS):
        rs = slice(h * t_q, (h + 1) * t_q)
        cs = slice(h * HEAD_DIM, (h + 1) * HEAD_DIM)
        o_ref[:, cs] = (acc_ref[rs, cs] / l_ref[rs, 0:1]).astype(BF16)


def _mix_rows(u_ref, vz_ref, ob_ref, ga_ref, gb_ref, x_ref, ws_ref, bst_ref, wbr_ref, wout_ref,
              nffn_ref, h_ref, hn_ref, oa_ref):
    tm = u_ref.shape[0]
    chunk = ws_ref.shape[1]
    causal = (lax.broadcasted_iota(jnp.int32, (chunk, chunk), 0)
              >= lax.broadcasted_iota(jnp.int32, (chunk, chunk), 1))
    for g in range(GMLP_GROUPS):
        cs = slice(g * GMLP_GROUP_DIM, (g + 1) * GMLP_GROUP_DIM)
        w_g = jnp.where(causal, ws_ref[g], 0.0).astype(BF16)
        b_g = bst_ref[:, g:g + 1]
        for c in range(tm // chunk):
            rs = slice(c * chunk, (c + 1) * chunk)
            mixed = _dot(w_g, vz_ref[rs, cs].astype(BF16)) + b_g
            oa_ref[rs, cs] = (u_ref[rs, cs].astype(F32) * mixed).astype(BF16)
    merged = (ga_ref[...].astype(F32) * _dot(oa_ref[...], wbr_ref[0])
              + gb_ref[...].astype(F32) * _dot(ob_ref[...], wbr_ref[1]))
    h = x_ref[...] + _dot(merged.astype(BF16), wout_ref[...])
    h_ref[...] = h
    hn_ref[...] = _rms(h, nffn_ref[...]).astype(BF16)


N_MIX_ROW_OPERANDS = 6


def _mix_kernel(*refs):
    k = N_MIX_ROW_OPERANDS
    prompt_rows, sample_rows = refs[:k], refs[k:2 * k]
    ws_ref, bst_ref, wss_ref, bsts_ref, wbr_ref, wout_ref, nffn_ref = refs[2 * k:2 * k + 7]
    h_ref, hn_ref, hs_ref, hns_ref, oa_ref, oas_ref = refs[2 * k + 7:]
    _mix_rows(*prompt_rows, ws_ref, bst_ref, wbr_ref, wout_ref, nffn_ref, h_ref, hn_ref, oa_ref)

    @pl.when(pl.program_id(0) == 0)
    def _():
        _mix_rows(*sample_rows, wss_ref, bsts_ref, wbr_ref, wout_ref, nffn_ref, hs_ref, hns_ref,
                  oas_ref)


def _mix(prompt_rows, sample_rows, ws_chunk, bs_t, ws_chunk_s, bs_t_s, w_br_b, w_out_b, norm_ffn, tm):
    m_rows = prompt_rows[-1].shape[0]
    s_rows = sample_rows[-1].shape[0]
    widths = (D_GMLP, D_GMLP, D_ATTN, D_MODEL, D_MODEL, D_MODEL)
    row = lambda m: (m, 0)
    const2 = lambda m: (0, 0)
    const3 = lambda m: (0, 0, 0)
    once = pl.Buffered(1)

    def chunk_specs(ws):
        chunk = ws.shape[1]
        return [pl.BlockSpec((GMLP_GROUPS, chunk, chunk), const3, pipeline_mode=once),
                pl.BlockSpec((chunk, GMLP_GROUPS), const2, pipeline_mode=once)]

    return pl.pallas_call(
        _mix_kernel,
        grid=(m_rows // tm,),
        in_specs=(
            [pl.BlockSpec((tm, w), row) for w in widths]
            + [pl.BlockSpec((s_rows, w), const2) for w in widths]
            + chunk_specs(ws_chunk) + chunk_specs(ws_chunk_s)
            + [pl.BlockSpec((2, D_GMLP, D_MODEL), const3, pipeline_mode=once),
               pl.BlockSpec((D_MODEL, D_MODEL), const2, pipeline_mode=once),
               pl.BlockSpec((1, D_MODEL), const2, pipeline_mode=once)]),
        out_specs=[pl.BlockSpec((tm, D_MODEL), row), pl.BlockSpec((tm, D_MODEL), row),
                   pl.BlockSpec((s_rows, D_MODEL), const2), pl.BlockSpec((s_rows, D_MODEL), const2)],
        out_shape=[jax.ShapeDtypeStruct((m_rows, D_MODEL), F32),
                   jax.ShapeDtypeStruct((m_rows, D_MODEL), BF16),
                   jax.ShapeDtypeStruct((s_rows, D_MODEL), F32),
                   jax.ShapeDtypeStruct((s_rows, D_MODEL), BF16)],
        scratch_shapes=[pltpu.VMEM((tm, D_GMLP), BF16), pltpu.VMEM((s_rows, D_GMLP), BF16)],
        compiler_params=_cparams(1),
        name="mix",
    )(*prompt_rows, *sample_rows, ws_chunk, bs_t, ws_chunk_s, bs_t_s, w_br_b, w_out_b,
      norm_ffn.reshape(1, D_MODEL))


def _ffn_tile(hn, wa, wb, wd, wc, bc, history):
    a = _dot(hn, wa)
    b = _dot(hn, wb)
    a1, a2 = history(a, pltpu.roll(a, 1, 0), pltpu.roll(a, 2, 0))
    conv = bc + a * wc[2:3, :] + a2 * wc[0:1, :] + a1 * wc[1:2, :]
    act = (jax.nn.gelu(conv) * b).astype(BF16)
    return a, _dot(act, wd)


def _ffn_kernel(hn_ref, wa_ref, wb_ref, wc_ref, bc_ref, wd_ref, h_ref, nfin_ref, init_ref,
                hns_ref, hs_ref, inits_ref, y_ref, tail_ref, ys_ref, tails_ref, halo_ref,
                *, tiles_per_seq, sample_seq):
    m = pl.program_id(0)
    n = pl.program_id(1)
    last_n = pl.num_programs(1) - 1
    tm = hn_ref.shape[0]
    tf = wa_ref.shape[1]

    @pl.when(n == 0)
    def _():
        y_ref[...] = h_ref[...]

    wa = wa_ref[...].astype(BF16)
    wb = wb_ref[...].astype(BF16)
    wd = wd_ref[...].astype(BF16)
    wc = wc_ref[...]
    bc = bc_ref[...]

    def carried_history(a, r1, r2):
        row = lax.broadcasted_iota(jnp.int32, (tm, tf), 0)
        prev = jnp.where(m % tiles_per_seq == 0, init_ref[0], halo_ref[n])
        p1 = prev[SUBLANES - 1:SUBLANES, :]
        p2 = prev[SUBLANES - 2:SUBLANES - 1, :]
        return (jnp.where(row == 0, p1, r1),
                jnp.where(row == 0, p2, jnp.where(row == 1, p1, r2)))

    a, part = _ffn_tile(hn_ref[...], wa, wb, wd, wc, bc, carried_history)
    last = a[tm - SUBLANES:, :]
    halo_ref[n] = last
    tail_ref[...] = last
    y_ref[...] += part

    @pl.when(n == last_n)
    def _():
        y_ref[...] = _rms(y_ref[...], nfin_ref[...])

    @pl.when(m == 0)
    def _():
        @pl.when(n == 0)
        def _():
            ys_ref[...] = hs_ref[...]

        def given_history(a_s, r1, r2):
            pos = lax.broadcasted_iota(jnp.int32, a_s.shape, 0) % sample_seq
            return jnp.where(pos >= 1, r1, inits_ref[0]), jnp.where(pos >= 2, r2, inits_ref[1])

        a_s, part_s = _ffn_tile(hns_ref[...], wa, wb, wd, wc, bc, given_history)
        tails_ref[...] = a_s
        ys_ref[...] += part_s

        @pl.when(n == last_n)
        def _():
            ys_ref[...] = _rms(ys_ref[...], nfin_ref[...])


def _ffn(hn, h, hn_s, h_s, w_up, w_conv, b_conv, w_down, norm_final, init, init_s, tm, seq_len,
         sample_seq):
    m_rows = hn.shape[0]
    s_rows = hn_s.shape[0]
    tf = FF_TILE
    n_ff = D_FF // tf
    tiles_per_seq = seq_len // tm
    row = lambda m, n: (m, 0)
    const = lambda m, n: (0, 0)
    sample_tile = lambda m, n: jnp.where(m == 0, n, n_ff - 1)
    once = pl.Buffered(1)
    return pl.pallas_call(
        functools.partial(_ffn_kernel, tiles_per_seq=tiles_per_seq, sample_seq=sample_seq),
        grid=(m_rows // tm, n_ff),
        in_specs=[
            pl.BlockSpec((tm, D_MODEL), row, pipeline_mode=once),
            pl.BlockSpec((D_MODEL, tf), lambda m, n: (0, n)),
            pl.BlockSpec((D_MODEL, tf), lambda m, n: (0, n_ff + n)),
            pl.BlockSpec((CONV_W, tf), lambda m, n: (0, n)),
            pl.BlockSpec((1, tf), lambda m, n: (0, n)),
            pl.BlockSpec((tf, D_MODEL), lambda m, n: (n, 0)),
            pl.BlockSpec((tm, D_MODEL), row, pipeline_mode=once),
            pl.BlockSpec((1, D_MODEL), const),
            pl.BlockSpec((1, SUBLANES, tf), lambda m, n: (0, m // tiles_per_seq, n)),
            pl.BlockSpec((s_rows, D_MODEL), const),
            pl.BlockSpec((s_rows, D_MODEL), const),
            pl.BlockSpec((2, s_rows, tf), lambda m, n: (0, 0, sample_tile(m, n))),
        ],
        out_specs=[
            pl.BlockSpec((tm, D_MODEL), row, pipeline_mode=once),
            pl.BlockSpec((SUBLANES, tf), lambda m, n: (m, n)),
            pl.BlockSpec((s_rows, D_MODEL), const),
            pl.BlockSpec((s_rows, tf), lambda m, n: (0, sample_tile(m, n))),
        ],
        out_shape=[
            jax.ShapeDtypeStruct((m_rows, D_MODEL), F32),
            jax.ShapeDtypeStruct(((m_rows // tm) * SUBLANES, D_FF), F32),
            jax.ShapeDtypeStruct((s_rows, D_MODEL), F32),
            jax.ShapeDtypeStruct((s_rows, D_FF), F32),
        ],
        scratch_shapes=[pltpu.VMEM((n_ff, SUBLANES, tf), F32)],
        compiler_params=_cparams(2),
        name="ffn",
    )(hn, w_up, w_up, w_conv, b_conv.reshape(1, D_FF), w_down, h,
      norm_final.reshape(1, D_MODEL), init, hn_s, h_s, init_s)


def _alibi_slopes():
    return jnp.exp2(-8.0 * jnp.arange(1, N_HEADS + 1, dtype=F32) / N_HEADS)


def kernel(x_prompt, x_sample, cache_k, cache_v, state_ffn_conv, page_table, norm_mix, w_in,
           norm_gmlp_v, w_spatial, b_spatial, w_branch, w_out, norm_ffn, w_up, w_conv, b_conv,
           w_down, norm_final):
    assert w_in.shape[0] == 1, "single layer"
    bp, seq, _ = x_prompt.shape
    bs, t_q, _ = x_sample.shape
    n_pages = page_table.shape[1]
    past_len = n_pages * PAGE_SIZE
    assert seq % MOBA_BLOCK == 0 and past_len % MOBA_BLOCK == 0 and past_len % CHUNK == 0
    assert t_q <= SUBLANES and past_len // MOBA_BLOCK <= LANES and seq // MOBA_BLOCK <= LANES

    slopes = _alibi_slopes()
    w_br_b = w_branch[0].astype(BF16)
    w_out_b = w_out[0].astype(BF16)
    ws, bsp = w_spatial[0], b_spatial[0]

    m_s = bs * t_q
    xp = x_prompt.reshape(bp * seq, D_MODEL)
    xs = x_sample.reshape(m_s, D_MODEL)
    xn_p = _norm_bf16(xp, norm_mix[0], PROMPT_TM_NORM)
    xn_s = _norm_bf16(xs, norm_mix[0], m_s)
    (u, vz, q, k, kb, v, vb, ga, gb), (us, vzs, qs, ks, kbs, vs, vbs, gas, gbs) = _inproj(
        xn_p, xn_s, w_in[0], norm_gmlp_v[0], PROMPT_TM_INPROJ)

    q4 = qs.reshape(bs, t_q, N_HEADS, HEAD_DIM)
    eye = jnp.eye(N_HEADS, dtype=BF16)
    qbd = (q4.transpose(0, 2, 1, 3)[:, :, :, None, :] * eye[None, :, None, :, None]
           ).reshape(bs, N_HEADS * t_q, D_ATTN)
    pad_rows = ((0, 0), (0, LANES - t_q), (0, 0))
    knew = jnp.pad(kbs.reshape(bs, t_q, D_ATTN), pad_rows)
    vnew = jnp.pad(vbs.reshape(bs, t_q, D_ATTN), pad_rows)
    slope_rows = jnp.repeat(slopes, t_q).reshape(N_HEADS * t_q, 1)
    ck = cache_k[0].reshape(-1, PAGE_SIZE * N_HEADS, HEAD_DIM)
    cv = cache_v[0].reshape(-1, PAGE_SIZE * N_HEADS, HEAD_DIM)
    p, pown, l = _sample_scores(page_table, qbd, knew, slope_rows, ck, t_q, past_len)

    ob, obs = _moba_prompt(q, kb, vb, slopes, bp, seq, page_table, p, pown, l, vnew, cv, t_q)
    obs = obs.reshape(m_s, D_ATTN)
    ws_s = (jnp.eye(bs, dtype=F32)[None, :, None, :, None]
            * ws[:, None, :t_q, None, :t_q]).reshape(GMLP_GROUPS, m_s, m_s)
    bs_t_s = jnp.tile(bsp[:, :t_q], (1, bs)).T
    h, hn, hs, hns = _mix((u, vz, ob, ga, gb, xp), (us, vzs, obs, gas, gbs, xs), ws, bsp.T, ws_s,
                          bs_t_s, w_br_b, w_out_b, norm_ffn[0], PROMPT_TM_MIX)
    st = state_ffn_conv[0]
    zero_rows = jnp.zeros((bs, t_q - 1, D_FF), F32)
    init1 = jnp.concatenate([st[:, 1:2], zero_rows], axis=1)
    init2 = jnp.concatenate([st[:, 0:1], st[:, 1:2], zero_rows[:, 1:]], axis=1)
    init_s = jnp.stack([init1.reshape(m_s, D_FF), init2.reshape(m_s, D_FF)])

    zero_state = jnp.zeros((1, bp * SUBLANES, D_FF), F32)
    tm_p = PROMPT_TM_FFN
    yp, tail_p, ys, tail_s = _ffn(hn, h, hns, hs, w_up[0], w_conv[0], b_conv[0], w_down[0],
                                  norm_final, zero_state, init_s, tm_p, seq, t_q)

    y_prompt = yp.reshape(bp, seq, D_MODEL)
    gv_p = vz.reshape(bp, seq, D_GMLP)[:, seq - CHUNK:][None]
    k_p = k.reshape(1, bp, seq, N_HEADS, HEAD_DIM)
    v_p = v.reshape(1, bp, seq, N_HEADS, HEAD_DIM)
    c_p = tail_p.reshape(bp, seq // tm_p, SUBLANES, D_FF)[:, -1, SUBLANES - (CONV_W - 1):][None]
    y_sample = ys.reshape(bs, t_q, D_MODEL)
    gv_s = vzs.reshape(1, bs, t_q, D_GMLP)
    k_s = ks.reshape(1, bs, t_q, N_HEADS, HEAD_DIM)
    v_s = vs.reshape(1, bs, t_q, N_HEADS, HEAD_DIM)
    c_s = tail_s.reshape(bs, t_q, D_FF)[:, t_q - (CONV_W - 1):][None]

    return (y_prompt, y_sample, gv_p, gv_s, k_p, v_p, k_s, v_s, c_p, c_s)
```

```python
import functools

import jax
import jax.numpy as jnp
from jax import lax
from jax.experimental import pallas as pl
from jax.experimental.pallas import tpu as pltpu

F32 = jnp.float32
BF16 = jnp.bfloat16

D_MODEL = 2048
D_GMLP = D_MODEL // 2
GMLP_GROUPS = 8
GMLP_GROUP_DIM = D_GMLP // GMLP_GROUPS
CHUNK = 128
D_ATTN = D_MODEL // 2
HEAD_DIM = 128
N_HEADS = D_ATTN // HEAD_DIM
MOBA_BLOCK = 256
MOBA_TOPK = 3
D_FF = 5632
CONV_W = 3
RMS_EPS = 1e-6
PAGE_SIZE = 128
N_SECTIONS = 9

LANES = 128
SUBLANES = 8
VMEM_LIMIT_BYTES = 56 * 1024 * 1024

NEG_BIG = -1e30
LOG2E = 1.4426950408889634
MOBA_HEADS_PER_STEP = 4
MOBA_BLOCKS_PER_TRIP = 8
MOBA_ONES_ROWS = 16
SAMPLE_CHUNK_BLOCKS = 8
FF_TILE = 512
PAGES_PER_STEP = 16
PROMPT_TM_NORM = 1024
PROMPT_TM_INPROJ = 512
PROMPT_TM_MIX = 256
PROMPT_TM_FFN = 1024

_NT = (((1,), (1,)), ((), ()))


def _dot(a, b):
    return jnp.dot(a, b, preferred_element_type=F32)


def _dot_nt(a, b):
    return lax.dot_general(a, b, _NT, preferred_element_type=F32)


def _rms(x, g):
    return x * lax.rsqrt(jnp.mean(x * x, axis=-1, keepdims=True) + RMS_EPS) * g


def _sigmoid(x):
    return 0.5 * (jnp.tanh(0.5 * x) + 1.0)


def _cparams(n_axes):
    return pltpu.CompilerParams(dimension_semantics=("arbitrary",) * n_axes,
                                vmem_limit_bytes=VMEM_LIMIT_BYTES)


def _top3_mask(gate, idx_f, axis=1):
    sel = jnp.zeros_like(gate)
    for _ in range(MOBA_TOPK):
        mx = jnp.max(gate, axis=axis, keepdims=True)
        first = jnp.min(jnp.where(gate == mx, idx_f, float(LANES)), axis=axis, keepdims=True)
        pick = (idx_f == first) & (mx > -jnp.inf)
        sel = jnp.where(pick, 1.0, sel)
        gate = jnp.where(pick, -jnp.inf, gate)
    return sel


def _store_heads(dst_ref, dst_b_ref, acc):
    tm = acc.shape[0]
    dst_b_ref[...] = acc.astype(BF16)
    for h in range(N_HEADS):
        dst_ref[pl.ds(h, tm, stride=N_HEADS), :] = acc[:, h * HEAD_DIM:(h + 1) * HEAD_DIM]


def _norm_kernel(x_ref, g_ref, o_ref):
    o_ref[...] = _rms(x_ref[...], g_ref[...]).astype(BF16)


def _norm_bf16(x, g, tm):
    m_rows, d = x.shape
    return pl.pallas_call(
        _norm_kernel,
        grid=(m_rows // tm,),
        in_specs=[pl.BlockSpec((tm, d), lambda m: (m, 0)), pl.BlockSpec((1, d), lambda m: (0, 0))],
        out_specs=pl.BlockSpec((tm, d), lambda m: (m, 0)),
        out_shape=jax.ShapeDtypeStruct((m_rows, d), BF16),
        compiler_params=_cparams(1),
        name="norm",
    )(x, g.reshape(1, d))


N_INPROJ_OUTS = 9


def _inproj_kernel(xn_ref, xs_ref, w_ref, ngv_ref, *rest):
    prompt = rest[:N_INPROJ_OUTS]
    sample = rest[N_INPROJ_OUTS:2 * N_INPROJ_OUTS]
    wb_ref = rest[2 * N_INPROJ_OUTS]
    n = pl.program_id(0)
    first_tile = pl.program_id(1) == 0

    @pl.when(first_tile)
    def _():
        wb_ref[...] = w_ref[...].astype(BF16)

    def emit(pred, epilogue):
        @pl.when(pred)
        def _():
            epilogue(_dot(xn_ref[...], wb_ref[...]), prompt)

            @pl.when(first_tile)
            def _():
                epilogue(_dot(xs_ref[...], wb_ref[...]), sample)

    def gelu_u(acc, outs):
        outs[0][...] = jax.nn.gelu(acc).astype(BF16)

    def gelu_norm_v(acc, outs):
        outs[1][...] = _rms(jax.nn.gelu(acc), ngv_ref[...])

    def plain_q(acc, outs):
        outs[2][...] = acc.astype(BF16)

    def heads_k(acc, outs):
        _store_heads(outs[3], outs[4], acc)

    def heads_v(acc, outs):
        _store_heads(outs[5], outs[6], acc)

    def gate_a(acc, outs):
        outs[7][...] = _sigmoid(acc).astype(BF16)

    def gate_b(acc, outs):
        outs[8][...] = _sigmoid(acc).astype(BF16)

    emit(n == 0, gelu_u)
    emit(n == 1, gelu_norm_v)
    emit(n == 2, plain_q)
    emit(n == 3, heads_k)
    emit(n == 4, heads_v)
    emit((n == 5) | (n == 6), gate_a)
    emit(n >= 7, gate_b)


def _inproj(xn, xn_s, w_in, norm_gmlp_v, tm):
    m_rows = xn.shape[0]
    s_rows = xn_s.shape[0]
    sec = D_GMLP
    last = m_rows // tm - 1

    def rows(first, count):
        def index(n, m):
            return jnp.where(n < first, 0, jnp.where(n >= first + count, last, m))
        return index

    def col(first, count):
        return lambda n: jnp.clip(n - first, 0, count - 1)

    def out(first, count=1):
        r, c = rows(first, count), col(first, count)
        return pl.BlockSpec((tm, sec), lambda n, m: (r(n, m), c(n)))

    def out_heads(first):
        r = rows(first, 1)
        return pl.BlockSpec((tm * N_HEADS, HEAD_DIM), lambda n, m: (r(n, m), 0))

    def out_s(first, count=1):
        c = col(first, count)
        return pl.BlockSpec((s_rows, sec), lambda n, m: (0, c(n)))

    out_heads_s = pl.BlockSpec((s_rows * N_HEADS, HEAD_DIM), lambda n, m: (0, 0))

    def shapes(r):
        return [
            jax.ShapeDtypeStruct((r, sec), BF16),
            jax.ShapeDtypeStruct((r, sec), F32),
            jax.ShapeDtypeStruct((r, sec), BF16),
            jax.ShapeDtypeStruct((r * N_HEADS, HEAD_DIM), F32),
            jax.ShapeDtypeStruct((r, sec), BF16),
            jax.ShapeDtypeStruct((r * N_HEADS, HEAD_DIM), F32),
            jax.ShapeDtypeStruct((r, sec), BF16),
            jax.ShapeDtypeStruct((r, D_MODEL), BF16),
            jax.ShapeDtypeStruct((r, D_MODEL), BF16),
        ]

    in_specs = [
        pl.BlockSpec((tm, D_MODEL), lambda n, m: (m, 0)),
        pl.BlockSpec((s_rows, D_MODEL), lambda n, m: (0, 0)),
        pl.BlockSpec((D_MODEL, sec), lambda n, m: (0, n)),
        pl.BlockSpec((1, sec), lambda n, m: (0, 0)),
    ]
    out_specs = [out(0), out(1), out(2), out_heads(3), out(3), out_heads(4), out(4), out(5, 2),
                 out(7, 2),
                 out_s(0), out_s(1), out_s(2), out_heads_s, out_s(3), out_heads_s, out_s(4),
                 out_s(5, 2), out_s(7, 2)]
    outs = pl.pallas_call(
        _inproj_kernel,
        grid=(N_SECTIONS, m_rows // tm),
        in_specs=in_specs,
        out_specs=out_specs,
        out_shape=shapes(m_rows) + shapes(s_rows),
        scratch_shapes=[pltpu.VMEM((D_MODEL, sec), BF16)],
        compiler_params=_cparams(2),
        name="inproj",
    )(xn, xn_s, w_in, norm_gmlp_v.reshape(1, sec))
    return outs[:N_INPROJ_OUTS], outs[N_INPROJ_OUTS:]


def _moba_prompt_kernel(pt_ref, slopes_ref, q_ref, kb_ref, vb_ref, p_ref, pown_ref, l_ref, vnew_ref,
                        *rest, n_blocks, steps_per_sample, t_q):
    del pt_ref
    v_refs = rest[:PAGES_PER_STEP]
    (o_ref, os_ref, km_ref, vt_ref, bias_ref, sel_ref, m_ref, t_ref, acc_ref,
     accs_ref) = rest[PAGES_PER_STEP:]
    hg = pl.program_id(1)
    i = pl.program_id(2)
    step = (pl.program_id(0) * pl.num_programs(1) + hg) * n_blocks + i
    sample_g = step % steps_per_sample
    blk = MOBA_BLOCK

    @pl.when(step == 0)
    def _():
        accs_ref[...] = jnp.zeros_like(accs_ref)
    c1 = (HEAD_DIM ** -0.5) * LOG2E
    heads = [(hh, slice(hh * HEAD_DIM, (hh + 1) * HEAD_DIM)) for hh in range(MOBA_HEADS_PER_STEP)]

    def slope2(hh):
        return slopes_ref[hg * MOBA_HEADS_PER_STEP + hh] * LOG2E

    @pl.when(i == 0)
    def _():
        key_f = lax.broadcasted_iota(jnp.int32, (blk, blk), 0).astype(F32)
        for hh, cs in heads:
            bias_ref[hh] = slope2(hh) * key_f
            vt_ref[hh, HEAD_DIM:, :] = jnp.ones((MOBA_ONES_ROWS, n_blocks * blk), BF16)
            for j in range(n_blocks):
                rs = slice(j * blk, (j + 1) * blk)
                km_ref[hh, j:j + 1, :] = jnp.sum(kb_ref[rs, cs].astype(F32), axis=0,
                                                 keepdims=True) * (1.0 / blk)
                vt_ref[hh, :HEAD_DIM, rs] = vb_ref[rs, cs].astype(F32).T.astype(BF16)

    assert n_blocks <= 2 * MOBA_BLOCKS_PER_TRIP
    trips = []
    done = 0
    un = MOBA_BLOCKS_PER_TRIP
    while un >= 1:
        n_trips = (i - done) // un
        trips.append((un, done, n_trips))
        done = done + n_trips * un
        un //= 2

    assert 2 * MOBA_BLOCKS_PER_TRIP - 1 <= PAGES_PER_STEP
    trip_pages = {}
    first_page = 0
    for size, _, _ in trips:
        trip_pages[size] = range(first_page, first_page + size)
        first_page += size
    _sample_pv_start(sample_g, p_ref, pown_ref, vnew_ref, v_refs, accs_ref,
                     range(first_page, PAGES_PER_STEP))

    blk_i = lax.broadcasted_iota(jnp.int32, (n_blocks, blk), 0)
    causal = (lax.broadcasted_iota(jnp.int32, (blk, blk), 1)
              >= lax.broadcasted_iota(jnp.int32, (blk, blk), 0))
    start = pl.multiple_of(i * blk, blk)

    for hh, cs in heads:
        q = q_ref[:, cs]
        km = km_ref[hh]
        km_hi = km.astype(BF16)
        km_lo = (km - km_hi.astype(F32)).astype(BF16)
        gate = _dot_nt(km_hi, q) + _dot_nt(km_lo, q)
        gate = jnp.where(blk_i < i, gate, -jnp.inf)
        sel_ref[hh] = _top3_mask(gate, blk_i.astype(F32), axis=0)
        t = _dot_nt(kb_ref[pl.ds(start, blk), cs], q) * c1 + bias_ref[hh]
        t = jnp.where(causal, t, NEG_BIG)
        t_ref[hh, i] = t
        m_ref[hh] = jnp.max(t, axis=0, keepdims=True)

    def shift(hh, j):
        return slope2(hh) * lax.convert_element_type((i - j) * blk, F32)

    def pass1(j0, nb):
        for hh, cs in heads:
            m = m_ref[hh]
            for j in [j0 + d for d in range(nb)]:
                off = pl.multiple_of(j * blk, blk)
                t = _dot_nt(kb_ref[pl.ds(off, blk), cs], q_ref[:, cs]) * c1 + bias_ref[hh]
                t_ref[hh, j] = t
                picked = sel_ref[hh, pl.ds(j, 1), :] > 0.0
                m_blk = jnp.max(t, axis=0, keepdims=True) - shift(hh, j)
                m = jnp.where(picked, jnp.maximum(m, m_blk), m)
            m_ref[hh] = m

    def pass2(j0, nb):
        off = pl.multiple_of(j0 * blk, blk)
        for hh, cs in heads:
            ps = []
            for j in [j0 + d for d in range(nb)]:
                picked = sel_ref[hh, pl.ds(j, 1), :] > 0.0
                sub = jnp.where(picked, m_ref[hh] + shift(hh, j), -NEG_BIG)
                ps.append(jnp.exp2(t_ref[hh, j] - sub).astype(BF16))
            p = ps[0] if nb == 1 else jnp.concatenate(ps, axis=0)
            acc_ref[hh] += _dot(vt_ref[hh, :, pl.ds(off, nb * blk)], p)
        accs_ref[...] = _sample_pv_pages(accs_ref[...], p_ref, v_refs, trip_pages[nb])

    def over_past_blocks(fn):
        for size, first, n_trips in trips:
            @pl.when(n_trips > 0)
            def _(size=size, first=first):
                fn(first, size)

    for size, _, n_trips in trips:
        @pl.when(n_trips == 0)
        def _(size=size):
            accs_ref[...] = _sample_pv_pages(accs_ref[...], p_ref, v_refs, trip_pages[size])

    over_past_blocks(pass1)
    for hh, cs in heads:
        p = jnp.exp2(t_ref[hh, i] - m_ref[hh])
        acc_ref[hh] = _dot(vt_ref[hh, :, pl.ds(start, blk)], p.astype(BF16))
    over_past_blocks(pass2)
    for hh, cs in heads:
        acc = acc_ref[hh]
        o_ref[:, cs] = (acc[:HEAD_DIM] / acc[HEAD_DIM:HEAD_DIM + 1]).T.astype(BF16)

    @pl.when(sample_g == steps_per_sample - 1)
    def _():
        _sample_pv_finish(l_ref, os_ref, accs_ref, t_q)


def _moba_prompt(q, kb, vb, slopes, bsz, seq, page_table, p, pown, l, vnew_pad, cache_v_pages, t_q):
    n_blocks = seq // MOBA_BLOCK
    blk = MOBA_BLOCK
    hps = MOBA_HEADS_PER_STEP
    n_hg = N_HEADS // hps
    width = hps * HEAD_DIM
    s_bsz, rows, _ = p.shape
    n_pages = page_table.shape[1]
    steps_per_sample = (bsz * n_hg * n_blocks) // s_bsz
    assert steps_per_sample * s_bsz == bsz * n_hg * n_blocks
    assert steps_per_sample * PAGES_PER_STEP == n_pages, "every V page is visited exactly once"
    step_keys = PAGES_PER_STEP * PAGE_SIZE

    def sample_step(b, hg, i):
        step = (b * n_hg + hg) * n_blocks + i
        return step // steps_per_sample, step % steps_per_sample

    def sample_batch(b, hg, i, pt):
        return sample_step(b, hg, i)[0], 0, 0

    def sample_keys(b, hg, i, pt):
        sb, g = sample_step(b, hg, i)
        return sb, 0, g

    qmap = lambda b, hg, i, pt: (b * n_blocks + i, hg)
    kvmap = lambda b, hg, i, pt: (b, hg)
    once = pl.Buffered(1)
    grid_spec = pltpu.PrefetchScalarGridSpec(
        num_scalar_prefetch=1,
        grid=(bsz, n_hg, n_blocks),
        in_specs=[
            pl.BlockSpec(memory_space=pltpu.SMEM),
            pl.BlockSpec((blk, width), qmap),
            pl.BlockSpec((seq, width), kvmap, pipeline_mode=once),
            pl.BlockSpec((seq, width), kvmap, pipeline_mode=once),
            pl.BlockSpec((None, rows, step_keys), sample_keys),
            pl.BlockSpec((None, rows, LANES), sample_batch),
            pl.BlockSpec((None, rows, LANES), sample_batch),
            pl.BlockSpec((None, LANES, D_ATTN), sample_batch),
        ] + _page_specs(sample_step),
        out_specs=[
            pl.BlockSpec((blk, width), qmap),
            pl.BlockSpec((None, t_q, D_ATTN), sample_batch),
        ],
        scratch_shapes=[
            pltpu.VMEM((hps, n_blocks, HEAD_DIM), F32),
            pltpu.VMEM((hps, HEAD_DIM + MOBA_ONES_ROWS, seq), BF16),
            pltpu.VMEM((hps, blk, blk), F32),
            pltpu.VMEM((hps, n_blocks, blk), F32),
            pltpu.VMEM((hps, 1, blk), F32),
            pltpu.VMEM((hps, n_blocks, blk, blk), F32),
            pltpu.VMEM((hps, HEAD_DIM + MOBA_ONES_ROWS, blk), F32),
            pltpu.VMEM((rows, D_ATTN), F32),
        ],
    )
    return pl.pallas_call(
        functools.partial(_moba_prompt_kernel, n_blocks=n_blocks,
                          steps_per_sample=steps_per_sample, t_q=t_q),
        grid_spec=grid_spec,
        out_shape=[jax.ShapeDtypeStruct((bsz * seq, D_ATTN), BF16),
                   jax.ShapeDtypeStruct((s_bsz, t_q, D_ATTN), BF16)],
        compiler_params=_cparams(3),
        name="moba_prompt",
    )(page_table, slopes, q, kb, vb, p, pown, l, vnew_pad, *([cache_v_pages] * PAGES_PER_STEP))


def _page_heads(page_ref):
    return [page_ref[pl.ds(h, PAGE_SIZE, stride=N_HEADS), :] for h in range(N_HEADS)]


def _load_page(page_ref):
    return jnp.concatenate([s.astype(BF16) for s in _page_heads(page_ref)], axis=1)


def _page_specs(batch_and_step=lambda b, g: (b, g)):
    rows = PAGE_SIZE * N_HEADS

    def spec(pp):
        def index(*args):
            *grid_idx, pt = args
            b, g = batch_and_step(*grid_idx)
            return pt[b, g * PAGES_PER_STEP + pp], 0, 0
        return pl.BlockSpec((None, rows, HEAD_DIM), index)

    return [spec(pp) for pp in range(PAGES_PER_STEP)]


def _sample_scores_kernel(pt_ref, qbd_ref, knew_ref, slope_ref, *rest, t_q, past_len):
    del pt_ref
    k_refs = rest[:PAGES_PER_STEP]
    p_ref, pown_ref, l_ref, km_ref = rest[PAGES_PER_STEP:]
    g = pl.program_id(1)
    n_steps = pl.num_programs(1)
    rows = qbd_ref.shape[0]
    blk = MOBA_BLOCK
    pages_per_block = blk // PAGE_SIZE
    n_past_blocks = past_len // blk
    qbd = qbd_ref[...]
    lane = lax.broadcasted_iota(jnp.int32, (rows, LANES), 1)
    lane_f = lane.astype(F32)

    @pl.when(g == 0)
    def _():
        km_ref[n_past_blocks:, :] = jnp.zeros((LANES - n_past_blocks, D_ATTN), F32)

    ksum = None
    km_rows = []
    for pp in range(PAGES_PER_STEP):
        heads = _page_heads(k_refs[pp])
        kpage = jnp.concatenate([s.astype(BF16) for s in heads], axis=1)
        lg = _dot_nt(qbd, kpage)
        off = pl.multiple_of((g * PAGES_PER_STEP + pp) * PAGE_SIZE, PAGE_SIZE)
        p_ref[:, pl.ds(off, PAGE_SIZE)] = lg
        psum = jnp.concatenate(
            [jnp.sum(s.reshape(PAGE_SIZE // SUBLANES, SUBLANES, HEAD_DIM), axis=0) for s in heads],
            axis=1)
        ksum = psum if pp % pages_per_block == 0 else ksum + psum
        if pp % pages_per_block == pages_per_block - 1:
            km_rows.append(jnp.sum(ksum, axis=0, keepdims=True) * (1.0 / blk))
    step_blocks = PAGES_PER_STEP // pages_per_block
    km_ref[pl.ds(pl.multiple_of(g * step_blocks, step_blocks), step_blocks), :] = (
        jnp.concatenate(km_rows, axis=0))

    @pl.when(g == n_steps - 1)
    def _():
        gate = _dot_nt(qbd, km_ref[...].astype(BF16))
        c1 = (HEAD_DIM ** -0.5) * LOG2E
        slope2 = slope_ref[...] * LOG2E
        t_row = (lax.broadcasted_iota(jnp.int32, (rows, 1), 0) % t_q).astype(F32)
        sel = _top3_mask(jnp.where(lane < n_past_blocks, gate, -jnp.inf), lane_f)

        s_own = _dot_nt(qbd, knew_ref[...]) * c1 + slope2 * lane_f
        s_own = jnp.where(lane_f <= t_row, s_own, NEG_BIG)

        cb = SAMPLE_CHUNK_BLOCKS
        ch = cb * blk
        n_chunks = n_past_blocks // cb
        assert cb & (cb - 1) == 0 and blk & (blk - 1) == 0
        widen = jnp.where(
            jnp.bitwise_and(lax.broadcasted_iota(jnp.int32, (LANES, ch), 0), cb - 1)
            == jnp.right_shift(lax.broadcasted_iota(jnp.int32, (LANES, ch), 1), blk.bit_length() - 1),
            1.0, 0.0).astype(BF16)
        key_f = lax.broadcasted_iota(jnp.int32, (1, ch), 1).astype(F32)

        mx = jnp.full((rows, blk), NEG_BIG, F32)
        for c in range(n_chunks):
            sel_c = jnp.where((lane >= c * cb) & (lane < (c + 1) * cb), sel, 0.0).astype(BF16)
            picked = _dot(sel_c, widen)
            s = p_ref[:, c * ch:(c + 1) * ch] * c1 + slope2 * (key_f + float(c * ch - past_len))
            s = jnp.where(picked > 0.0, s, NEG_BIG)
            p_ref[:, c * ch:(c + 1) * ch] = s
            for b in range(cb):
                mx = jnp.maximum(mx, s[:, b * blk:(b + 1) * blk])
        m = jnp.maximum(jnp.max(mx, axis=1, keepdims=True), jnp.max(s_own, axis=1, keepdims=True))

        p_own = jnp.exp2(s_own - m)
        pown_ref[...] = p_own
        acc = jnp.zeros((rows, blk), F32)
        for c in range(n_chunks):
            pc = jnp.exp2(p_ref[:, c * ch:(c + 1) * ch] - m)
            p_ref[:, c * ch:(c + 1) * ch] = pc
            for b in range(cb):
                acc = acc + pc[:, b * blk:(b + 1) * blk]
        l = jnp.sum(acc, axis=1, keepdims=True) + jnp.sum(p_own, axis=1, keepdims=True)
        l_ref[...] = jnp.broadcast_to(l, l_ref.shape)


def _sample_scores(page_table, qbd, knew_pad, slope_rows, cache_k_pages, t_q, past_len):
    bsz, rows, _ = qbd.shape
    n_pages = page_table.shape[1]
    n_steps = n_pages // PAGES_PER_STEP
    grid_spec = pltpu.PrefetchScalarGridSpec(
        num_scalar_prefetch=1,
        grid=(bsz, n_steps),
        in_specs=[
            pl.BlockSpec((None, rows, D_ATTN), lambda b, g, pt: (b, 0, 0)),
            pl.BlockSpec((None, LANES, D_ATTN), lambda b, g, pt: (b, 0, 0)),
            pl.BlockSpec((rows, 1), lambda b, g, pt: (0, 0)),
        ] + _page_specs(),
        out_specs=[
            pl.BlockSpec((None, rows, past_len), lambda b, g, pt: (b, 0, 0)),
            pl.BlockSpec((None, rows, LANES), lambda b, g, pt: (b, 0, 0)),
            pl.BlockSpec((None, rows, LANES), lambda b, g, pt: (b, 0, 0)),
        ],
        scratch_shapes=[pltpu.VMEM((LANES, D_ATTN), F32)],
    )
    return pl.pallas_call(
        functools.partial(_sample_scores_kernel, t_q=t_q, past_len=past_len),
        grid_spec=grid_spec,
        out_shape=[
            jax.ShapeDtypeStruct((bsz, rows, past_len), F32),
            jax.ShapeDtypeStruct((bsz, rows, LANES), F32),
            jax.ShapeDtypeStruct((bsz, rows, LANES), F32),
        ],
        compiler_params=_cparams(2),
        name="sample_scores",
    )(page_table, qbd, knew_pad, slope_rows, *([cache_k_pages] * PAGES_PER_STEP))


def _sample_pv_pages(acc, p_ref, v_refs, pages):
    for pp in pages:
        acc = acc + _dot(p_ref[:, pp * PAGE_SIZE:(pp + 1) * PAGE_SIZE].astype(BF16),
                         _load_page(v_refs[pp]))
    return acc


def _sample_pv_start(g, p_ref, pown_ref, vnew_ref, v_refs, acc_ref, pages):
    own = _dot(pown_ref[...].astype(BF16), vnew_ref[...])
    acc = jnp.where(g == 0, own, acc_ref[...])
    acc_ref[...] = _sample_pv_pages(acc, p_ref, v_refs, pages)


def _sample_pv_finish(l_ref, o_ref, acc_ref, t_q):
    for h in range(N_HEADS):
        rs = slice(h * t_q, (h + 1) * t_q)
        cs = slice(h * HEAD_DIM, (h + 1) * HEAD_DIM)
        o_ref[:, cs] = (acc_ref[rs, cs] / l_ref[rs, 0:1]).astype(BF16)


def _mix_rows(u_ref, vz_ref, ob_ref, ga_ref, gb_ref, x_ref, ws_ref, bst_ref, wbr_ref, wout_ref,
              nffn_ref, h_ref, hn_ref, oa_ref):
    tm = u_ref.shape[0]
    chunk = ws_ref.shape[1]
    causal = (lax.broadcasted_iota(jnp.int32, (chunk, chunk), 0)
              >= lax.broadcasted_iota(jnp.int32, (chunk, chunk), 1))
    for g in range(GMLP_GROUPS):
        cs = slice(g * GMLP_GROUP_DIM, (g + 1) * GMLP_GROUP_DIM)
        w_g = jnp.where(causal, ws_ref[g], 0.0).astype(BF16)
        b_g = bst_ref[:, g:g + 1]
        for c in range(tm // chunk):
            rs = slice(c * chunk, (c + 1) * chunk)
            mixed = _dot(w_g, vz_ref[rs, cs].astype(BF16)) + b_g
            oa_ref[rs, cs] = (u_ref[rs, cs].astype(F32) * mixed).astype(BF16)
    merged = (ga_ref[...].astype(F32) * _dot(oa_ref[...], wbr_ref[0])
              + gb_ref[...].astype(F32) * _dot(ob_ref[...], wbr_ref[1]))
    h = x_ref[...] + _dot(merged.astype(BF16), wout_ref[...])
    h_ref[...] = h
    hn_ref[...] = _rms(h, nffn_ref[...]).astype(BF16)


N_MIX_ROW_OPERANDS = 6


def _mix_kernel(*refs):
    k = N_MIX_ROW_OPERANDS
    prompt_rows, sample_rows = refs[:k], refs[k:2 * k]
    ws_ref, bst_ref, wss_ref, bsts_ref, wbr_ref, wout_ref, nffn_ref = refs[2 * k:2 * k + 7]
    h_ref, hn_ref, hs_ref, hns_ref, oa_ref, oas_ref = refs[2 * k + 7:]
    _mix_rows(*prompt_rows, ws_ref, bst_ref, wbr_ref, wout_ref, nffn_ref, h_ref, hn_ref, oa_ref)

    @pl.when(pl.program_id(0) == 0)
    def _():
        _mix_rows(*sample_rows, wss_ref, bsts_ref, wbr_ref, wout_ref, nffn_ref, hs_ref, hns_ref,
                  oas_ref)


def _mix(prompt_rows, sample_rows, ws_chunk, bs_t, ws_chunk_s, bs_t_s, w_br_b, w_out_b, norm_ffn, tm):
    m_rows = prompt_rows[-1].shape[0]
    s_rows = sample_rows[-1].shape[0]
    widths = (D_GMLP, D_GMLP, D_ATTN, D_MODEL, D_MODEL, D_MODEL)
    row = lambda m: (m, 0)
    const2 = lambda m: (0, 0)
    const3 = lambda m: (0, 0, 0)
    once = pl.Buffered(1)

    def chunk_specs(ws):
        chunk = ws.shape[1]
        return [pl.BlockSpec((GMLP_GROUPS, chunk, chunk), const3, pipeline_mode=once),
                pl.BlockSpec((chunk, GMLP_GROUPS), const2, pipeline_mode=once)]

    return pl.pallas_call(
        _mix_kernel,
        grid=(m_rows // tm,),
        in_specs=(
            [pl.BlockSpec((tm, w), row) for w in widths]
            + [pl.BlockSpec((s_rows, w), const2) for w in widths]
            + chunk_specs(ws_chunk) + chunk_specs(ws_chunk_s)
            + [pl.BlockSpec((2, D_GMLP, D_MODEL), const3, pipeline_mode=once),
               pl.BlockSpec((D_MODEL, D_MODEL), const2, pipeline_mode=once),
               pl.BlockSpec((1, D_MODEL), const2, pipeline_mode=once)]),
        out_specs=[pl.BlockSpec((tm, D_MODEL), row), pl.BlockSpec((tm, D_MODEL), row),
                   pl.BlockSpec((s_rows, D_MODEL), const2), pl.BlockSpec((s_rows, D_MODEL), const2)],
        out_shape=[jax.ShapeDtypeStruct((m_rows, D_MODEL), F32),
                   jax.ShapeDtypeStruct((m_rows, D_MODEL), BF16),
                   jax.ShapeDtypeStruct((s_rows, D_MODEL), F32),
                   jax.ShapeDtypeStruct((s_rows, D_MODEL), BF16)],
        scratch_shapes=[pltpu.VMEM((tm, D_GMLP), BF16), pltpu.VMEM((s_rows, D_GMLP), BF16)],
        compiler_params=_cparams(1),
        name="mix",
    )(*prompt_rows, *sample_rows, ws_chunk, bs_t, ws_chunk_s, bs_t_s, w_br_b, w_out_b,
      norm_ffn.reshape(1, D_MODEL))


def _ffn_tile(hn, wa, wb, wd, wc, bc, history):
    a = _dot(hn, wa)
    b = _dot(hn, wb)
    a1, a2 = history(a, pltpu.roll(a, 1, 0), pltpu.roll(a, 2, 0))
    conv = bc + a * wc[2:3, :] + a2 * wc[0:1, :] + a1 * wc[1:2, :]
    act = (jax.nn.gelu(conv) * b).astype(BF16)
    return a, _dot(act, wd)


def _ffn_kernel(hn_ref, wa_ref, wb_ref, wc_ref, bc_ref, wd_ref, h_ref, nfin_ref, init_ref,
                hns_ref, hs_ref, inits_ref, y_ref, tail_ref, ys_ref, tails_ref, halo_ref,
                *, tiles_per_seq, sample_seq):
    m = pl.program_id(0)
    n = pl.program_id(1)
    last_n = pl.num_programs(1) - 1
    tm = hn_ref.shape[0]
    tf = wa_ref.shape[1]

    @pl.when(n == 0)
    def _():
        y_ref[...] = h_ref[...]

    wa = wa_ref[...].astype(BF16)
    wb = wb_ref[...].astype(BF16)
    wd = wd_ref[...].astype(BF16)
    wc = wc_ref[...]
    bc = bc_ref[...]

    def carried_history(a, r1, r2):
        row = lax.broadcasted_iota(jnp.int32, (tm, tf), 0)
        prev = jnp.where(m % tiles_per_seq == 0, init_ref[0], halo_ref[n])
        p1 = prev[SUBLANES - 1:SUBLANES, :]
        p2 = prev[SUBLANES - 2:SUBLANES - 1, :]
        return (jnp.where(row == 0, p1, r1),
                jnp.where(row == 0, p2, jnp.where(row == 1, p1, r2)))

    a, part = _ffn_tile(hn_ref[...], wa, wb, wd, wc, bc, carried_history)
    last = a[tm - SUBLANES:, :]
    halo_ref[n] = last
    tail_ref[...] = last
    y_ref[...] += part

    @pl.when(n == last_n)
    def _():
        y_ref[...] = _rms(y_ref[...], nfin_ref[...])

    @pl.when(m == 0)
    def _():
        @pl.when(n == 0)
        def _():
            ys_ref[...] = hs_ref[...]

        def given_history(a_s, r1, r2):
            pos = lax.broadcasted_iota(jnp.int32, a_s.shape, 0) % sample_seq
            return jnp.where(pos >= 1, r1, inits_ref[0]), jnp.where(pos >= 2, r2, inits_ref[1])

        a_s, part_s = _ffn_tile(hns_ref[...], wa, wb, wd, wc, bc, given_history)
        tails_ref[...] = a_s
        ys_ref[...] += part_s

        @pl.when(n == last_n)
        def _():
            ys_ref[...] = _rms(ys_ref[...], nfin_ref[...])


def _ffn(hn, h, hn_s, h_s, w_up, w_conv, b_conv, w_down, norm_final, init, init_s, tm, seq_len,
         sample_seq):
    m_rows = hn.shape[0]
    s_rows = hn_s.shape[0]
    tf = FF_TILE
    n_ff = D_FF // tf
    tiles_per_seq = seq_len // tm
    row = lambda m, n: (m, 0)
    const = lambda m, n: (0, 0)
    sample_tile = lambda m, n: jnp.where(m == 0, n, n_ff - 1)
    once = pl.Buffered(1)
    return pl.pallas_call(
        functools.partial(_ffn_kernel, tiles_per_seq=tiles_per_seq, sample_seq=sample_seq),
        grid=(m_rows // tm, n_ff),
        in_specs=[
            pl.BlockSpec((tm, D_MODEL), row, pipeline_mode=once),
            pl.BlockSpec((D_MODEL, tf), lambda m, n: (0, n)),
            pl.BlockSpec((D_MODEL, tf), lambda m, n: (0, n_ff + n)),
            pl.BlockSpec((CONV_W, tf), lambda m, n: (0, n)),
            pl.BlockSpec((1, tf), lambda m, n: (0, n)),
            pl.BlockSpec((tf, D_MODEL), lambda m, n: (n, 0)),
            pl.BlockSpec((tm, D_MODEL), row, pipeline_mode=once),
            pl.BlockSpec((1, D_MODEL), const),
            pl.BlockSpec((1, SUBLANES, tf), lambda m, n: (0, m // tiles_per_seq, n)),
            pl.BlockSpec((s_rows, D_MODEL), const),
            pl.BlockSpec((s_rows, D_MODEL), const),
            pl.BlockSpec((2, s_rows, tf), lambda m, n: (0, 0, sample_tile(m, n))),
        ],
        out_specs=[
            pl.BlockSpec((tm, D_MODEL), row, pipeline_mode=once),
            pl.BlockSpec((SUBLANES, tf), lambda m, n: (m, n)),
            pl.BlockSpec((s_rows, D_MODEL), const),
            pl.BlockSpec((s_rows, tf), lambda m, n: (0, sample_tile(m, n))),
        ],
        out_shape=[
            jax.ShapeDtypeStruct((m_rows, D_MODEL), F32),
            jax.ShapeDtypeStruct(((m_rows // tm) * SUBLANES, D_FF), F32),
            jax.ShapeDtypeStruct((s_rows, D_MODEL), F32),
            jax.ShapeDtypeStruct((s_rows, D_FF), F32),
        ],
        scratch_shapes=[pltpu.VMEM((n_ff, SUBLANES, tf), F32)],
        compiler_params=_cparams(2),
        name="ffn",
    )(hn, w_up, w_up, w_conv, b_conv.reshape(1, D_FF), w_down, h,
      norm_final.reshape(1, D_MODEL), init, hn_s, h_s, init_s)


def _alibi_slopes():
    return jnp.exp2(-8.0 * jnp.arange(1, N_HEADS + 1, dtype=F32) / N_HEADS)


def kernel(x_prompt, x_sample, cache_k, cache_v, state_ffn_conv, page_table, norm_mix, w_in,
           norm_gmlp_v, w_spatial, b_spatial, w_branch, w_out, norm_ffn, w_up, w_conv, b_conv,
           w_down, norm_final):
    assert w_in.shape[0] == 1, "single layer"
    bp, seq, _ = x_prompt.shape
    bs, t_q, _ = x_sample.shape
    n_pages = page_table.shape[1]
    past_len = n_pages * PAGE_SIZE
    assert seq % MOBA_BLOCK == 0 and past_len % MOBA_BLOCK == 0 and past_len % CHUNK == 0
    assert t_q <= SUBLANES and past_len // MOBA_BLOCK <= LANES and seq // MOBA_BLOCK <= LANES

    slopes = _alibi_slopes()
    w_br_b = w_branch[0].astype(BF16)
    w_out_b = w_out[0].astype(BF16)
    ws, bsp = w_spatial[0], b_spatial[0]

    m_s = bs * t_q
    xp = x_prompt.reshape(bp * seq, D_MODEL)
    xs = x_sample.reshape(m_s, D_MODEL)
    xn_p = _norm_bf16(xp, norm_mix[0], PROMPT_TM_NORM)
    xn_s = _norm_bf16(xs, norm_mix[0], m_s)
    (u, vz, q, k, kb, v, vb, ga, gb), (us, vzs, qs, ks, kbs, vs, vbs, gas, gbs) = _inproj(
        xn_p, xn_s, w_in[0], norm_gmlp_v[0], PROMPT_TM_INPROJ)

    q4 = qs.reshape(bs, t_q, N_HEADS, HEAD_DIM)
    eye = jnp.eye(N_HEADS, dtype=BF16)
    qbd = (q4.transpose(0, 2, 1, 3)[:, :, :, None, :] * eye[None, :, None, :, None]
           ).reshape(bs, N_HEADS * t_q, D_ATTN)
    pad_rows = ((0, 0), (0, LANES - t_q), (0, 0))
    knew = jnp.pad(kbs.reshape(bs, t_q, D_ATTN), pad_rows)
    vnew = jnp.pad(vbs.reshape(bs, t_q, D_ATTN), pad_rows)
    slope_rows = jnp.repeat(slopes, t_q).reshape(N_HEADS * t_q, 1)
    ck = cache_k[0].reshape(-1, PAGE_SIZE * N_HEADS, HEAD_DIM)
    cv = cache_v[0].reshape(-1, PAGE_SIZE * N_HEADS, HEAD_DIM)
    p, pown, l = _sample_scores(page_table, qbd, knew, slope_rows, ck, t_q, past_len)

    ob, obs = _moba_prompt(q, kb, vb, slopes, bp, seq, page_table, p, pown, l, vnew, cv, t_q)
    obs = obs.reshape(m_s, D_ATTN)
    ws_s = (jnp.eye(bs, dtype=F32)[None, :, None, :, None]
            * ws[:, None, :t_q, None, :t_q]).reshape(GMLP_GROUPS, m_s, m_s)
    bs_t_s = jnp.tile(bsp[:, :t_q], (1, bs)).T
    h, hn, hs, hns = _mix((u, vz, ob, ga, gb, xp), (us, vzs, obs, gas, gbs, xs), ws, bsp.T, ws_s,
                          bs_t_s, w_br_b, w_out_b, norm_ffn[0], PROMPT_TM_MIX)
    st = state_ffn_conv[0]
    zero_rows = jnp.zeros((bs, t_q - 1, D_FF), F32)
    init1 = jnp.concatenate([st[:, 1:2], zero_rows], axis=1)
    init2 = jnp.concatenate([st[:, 0:1], st[:, 1:2], zero_rows[:, 1:]], axis=1)
    init_s = jnp.stack([init1.reshape(m_s, D_FF), init2.reshape(m_s, D_FF)])

    zero_state = jnp.zeros((1, bp * SUBLANES, D_FF), F32)
    tm_p = PROMPT_TM_FFN
    yp, tail_p, ys, tail_s = _ffn(hn, h, hns, hs, w_up[0], w_conv[0], b_conv[0], w_down[0],
                                  norm_final, zero_state, init_s, tm_p, seq, t_q)

    y_prompt = yp.reshape(bp, seq, D_MODEL)
    gv_p = vz.reshape(bp, seq, D_GMLP)[:, seq - CHUNK:][None]
    k_p = k.reshape(1, bp, seq, N_HEADS, HEAD_DIM)
    v_p = v.reshape(1, bp, seq, N_HEADS, HEAD_DIM)
    c_p = tail_p.reshape(bp, seq // tm_p, SUBLANES, D_FF)[:, -1, SUBLANES - (CONV_W - 1):][None]
    y_sample = ys.reshape(bs, t_q, D_MODEL)
    gv_s = vzs.reshape(1, bs, t_q, D_GMLP)
    k_s = ks.reshape(1, bs, t_q, N_HEADS, HEAD_DIM)
    v_s = vs.reshape(1, bs, t_q, N_HEADS, HEAD_DIM)
    c_s = tail_s.reshape(bs, t_q, D_FF)[:, t_q - (CONV_W - 1):][None]

    return (y_prompt, y_sample, gv_p, gv_s, k_p, v_p, k_s, v_s, c_p, c_s)
```

```python
import functools

import jax
import jax.numpy as jnp
from jax import lax
from jax.experimental import pallas as pl
from jax.experimental.pallas import tpu as pltpu

F32 = jnp.float32
BF16 = jnp.bfloat16

D_MODEL = 2048
D_GMLP = D_MODEL // 2
GMLP_GROUPS = 8
GMLP_GROUP_DIM = D_GMLP // GMLP_GROUPS
CHUNK = 128
D_ATTN = D_MODEL // 2
HEAD_DIM = 128
N_HEADS = D_ATTN // HEAD_DIM
MOBA_BLOCK = 256
MOBA_TOPK = 3
D_FF = 5632
CONV_W = 3
RMS_EPS = 1e-6
PAGE_SIZE = 128
N_SECTIONS = 9

LANES = 128
SUBLANES = 8
V7X_VMEM_BYTES = 64 * 1024 * 1024
VMEM_LIMIT_BYTES = V7X_VMEM_BYTES * 7 // 8

NEG_BIG = -1e30
LOG2E = 1.4426950408889634

MOBA_HEADS_PER_STEP = 4
MOBA_BLOCKS_PER_TRIP = 8
MOBA_ONES_ROWS = 16
SAMPLE_CHUNK_BLOCKS = 8
PAGES_PER_STEP = 16
FF_TILE = 512
PROMPT_TM_NORM = 1024
PROMPT_TM_INPROJ = 512
PROMPT_TM_MIX = 256
PROMPT_TM_FFN = 1024

_NT = (((1,), (1,)), ((), ()))


def _dot(a, b):
    return jnp.dot(a, b, preferred_element_type=F32)


def _dot_nt(a, b):
    return lax.dot_general(a, b, _NT, preferred_element_type=F32)


def _rms(x, g):
    return x * lax.rsqrt(jnp.mean(x * x, axis=-1, keepdims=True) + RMS_EPS) * g


def _sigmoid(x):
    return 0.5 * (jnp.tanh(0.5 * x) + 1.0)


def _cparams(n_axes):
    return pltpu.CompilerParams(dimension_semantics=("arbitrary",) * n_axes,
                                vmem_limit_bytes=VMEM_LIMIT_BYTES)


def _top3_mask(gate, idx_f, axis=1):
    sel = jnp.zeros_like(gate)
    for _ in range(MOBA_TOPK):
        mx = jnp.max(gate, axis=axis, keepdims=True)
        first = jnp.min(jnp.where(gate == mx, idx_f, float(LANES)), axis=axis, keepdims=True)
        pick = (idx_f == first) & (mx > -jnp.inf)
        sel = jnp.where(pick, 1.0, sel)
        gate = jnp.where(pick, -jnp.inf, gate)
    return sel


def _store_heads(dst_ref, dst_b_ref, acc):
    tm = acc.shape[0]
    dst_b_ref[...] = acc.astype(BF16)
    for h in range(N_HEADS):
        dst_ref[pl.ds(h, tm, stride=N_HEADS), :] = acc[:, h * HEAD_DIM:(h + 1) * HEAD_DIM]


def _norm_kernel(x_ref, g_ref, o_ref):
    o_ref[...] = _rms(x_ref[...], g_ref[...]).astype(BF16)


def _norm_bf16(x, g, tm):
    m_rows, d = x.shape
    return pl.pallas_call(
        _norm_kernel,
        grid=(m_rows // tm,),
        in_specs=[pl.BlockSpec((tm, d), lambda m: (m, 0)), pl.BlockSpec((1, d), lambda m: (0, 0))],
        out_specs=pl.BlockSpec((tm, d), lambda m: (m, 0)),
        out_shape=jax.ShapeDtypeStruct((m_rows, d), BF16),
        compiler_params=_cparams(1),
        name="norm",
    )(x, g.reshape(1, d))


N_INPROJ_OUTS = 9


def _inproj_kernel(xn_ref, xs_ref, w_ref, ngv_ref, *rest):
    prompt = rest[:N_INPROJ_OUTS]
    sample = rest[N_INPROJ_OUTS:2 * N_INPROJ_OUTS]
    wb_ref = rest[2 * N_INPROJ_OUTS]
    n = pl.program_id(0)
    first_tile = pl.program_id(1) == 0

    @pl.when(first_tile)
    def _():
        wb_ref[...] = w_ref[...].astype(BF16)

    def emit(pred, epilogue):
        @pl.when(pred)
        def _():
            epilogue(_dot(xn_ref[...], wb_ref[...]), prompt)

            @pl.when(first_tile)
            def _():
                epilogue(_dot(xs_ref[...], wb_ref[...]), sample)

    def gelu_u(acc, outs):
        outs[0][...] = jax.nn.gelu(acc).astype(BF16)

    def gelu_norm_v(acc, outs):
        outs[1][...] = _rms(jax.nn.gelu(acc), ngv_ref[...])

    def plain_q(acc, outs):
        outs[2][...] = acc.astype(BF16)

    def heads_k(acc, outs):
        _store_heads(outs[3], outs[4], acc)

    def heads_v(acc, outs):
        _store_heads(outs[5], outs[6], acc)

    def gate_a(acc, outs):
        outs[7][...] = _sigmoid(acc).astype(BF16)

    def gate_b(acc, outs):
        outs[8][...] = _sigmoid(acc).astype(BF16)

    emit(n == 0, gelu_u)
    emit(n == 1, gelu_norm_v)
    emit(n == 2, plain_q)
    emit(n == 3, heads_k)
    emit(n == 4, heads_v)
    emit((n == 5) | (n == 6), gate_a)
    emit(n >= 7, gate_b)


def _inproj(xn, xn_s, w_in, norm_gmlp_v, tm):
    m_rows = xn.shape[0]
    s_rows = xn_s.shape[0]
    sec = D_GMLP
    last = m_rows // tm - 1

    def rows(first, count):
        def index(n, m):
            return jnp.where(n < first, 0, jnp.where(n >= first + count, last, m))
        return index

    def col(first, count):
        return lambda n: jnp.clip(n - first, 0, count - 1)

    def out(first, count=1):
        r, c = rows(first, count), col(first, count)
        return pl.BlockSpec((tm, sec), lambda n, m: (r(n, m), c(n)))

    def out_heads(first):
        r = rows(first, 1)
        return pl.BlockSpec((tm * N_HEADS, HEAD_DIM), lambda n, m: (r(n, m), 0))

    def out_s(first, count=1):
        c = col(first, count)
        return pl.BlockSpec((s_rows, sec), lambda n, m: (0, c(n)))

    out_heads_s = pl.BlockSpec((s_rows * N_HEADS, HEAD_DIM), lambda n, m: (0, 0))

    def shapes(r):
        return [
            jax.ShapeDtypeStruct((r, sec), BF16),
            jax.ShapeDtypeStruct((r, sec), F32),
            jax.ShapeDtypeStruct((r, sec), BF16),
            jax.ShapeDtypeStruct((r * N_HEADS, HEAD_DIM), F32),
            jax.ShapeDtypeStruct((r, sec), BF16),
            jax.ShapeDtypeStruct((r * N_HEADS, HEAD_DIM), F32),
            jax.ShapeDtypeStruct((r, sec), BF16),
            jax.ShapeDtypeStruct((r, D_MODEL), BF16),
            jax.ShapeDtypeStruct((r, D_MODEL), BF16),
        ]

    in_specs = [
        pl.BlockSpec((tm, D_MODEL), lambda n, m: (m, 0)),
        pl.BlockSpec((s_rows, D_MODEL), lambda n, m: (0, 0)),
        pl.BlockSpec((D_MODEL, sec), lambda n, m: (0, n)),
        pl.BlockSpec((1, sec), lambda n, m: (0, 0)),
    ]
    out_specs = [out(0), out(1), out(2), out_heads(3), out(3), out_heads(4), out(4), out(5, 2),
                 out(7, 2),
                 out_s(0), out_s(1), out_s(2), out_heads_s, out_s(3), out_heads_s, out_s(4),
                 out_s(5, 2), out_s(7, 2)]
    outs = pl.pallas_call(
        _inproj_kernel,
        grid=(N_SECTIONS, m_rows // tm),
        in_specs=in_specs,
        out_specs=out_specs,
        out_shape=shapes(m_rows) + shapes(s_rows),
        scratch_shapes=[pltpu.VMEM((D_MODEL, sec), BF16)],
        compiler_params=_cparams(2),
        name="inproj",
    )(xn, xn_s, w_in, norm_gmlp_v.reshape(1, sec))
    return outs[:N_INPROJ_OUTS], outs[N_INPROJ_OUTS:]


def _moba_prompt_kernel(pt_ref, slopes_ref, q_ref, kb_ref, vb_ref, p_ref, pown_ref, l_ref, vnew_ref,
                        *rest, n_blocks, steps_per_sample, t_q):
    del pt_ref
    v_refs = rest[:PAGES_PER_STEP]
    (o_ref, os_ref, km_ref, vt_ref, bias_ref, sel_ref, m_ref, t_ref, acc_ref,
     accs_ref) = rest[PAGES_PER_STEP:]
    hg = pl.program_id(1)
    i = pl.program_id(2)
    step = (pl.program_id(0) * pl.num_programs(1) + hg) * n_blocks + i
    sample_g = step % steps_per_sample
    blk = MOBA_BLOCK

    @pl.when(step == 0)
    def _():
        accs_ref[...] = jnp.zeros_like(accs_ref)
    c1 = (HEAD_DIM ** -0.5) * LOG2E
    heads = [(hh, slice(hh * HEAD_DIM, (hh + 1) * HEAD_DIM)) for hh in range(MOBA_HEADS_PER_STEP)]

    def slope2(hh):
        return slopes_ref[hg * MOBA_HEADS_PER_STEP + hh] * LOG2E

    @pl.when(i == 0)
    def _():
        key_f = lax.broadcasted_iota(jnp.int32, (blk, blk), 0).astype(F32)
        for hh, cs in heads:
            bias_ref[hh] = slope2(hh) * key_f
            vt_ref[hh, HEAD_DIM:, :] = jnp.ones((MOBA_ONES_ROWS, n_blocks * blk), BF16)
            for j in range(n_blocks):
                rs = slice(j * blk, (j + 1) * blk)
                km_ref[hh, j:j + 1, :] = jnp.sum(kb_ref[rs, cs].astype(F32), axis=0,
                                                 keepdims=True) * (1.0 / blk)
                vt_ref[hh, :HEAD_DIM, rs] = vb_ref[rs, cs].astype(F32).T.astype(BF16)

    _sample_pv_accumulate(sample_g, p_ref, pown_ref, vnew_ref, v_refs, accs_ref)

    blk_i = lax.broadcasted_iota(jnp.int32, (n_blocks, blk), 0)
    causal = (lax.broadcasted_iota(jnp.int32, (blk, blk), 1)
              >= lax.broadcasted_iota(jnp.int32, (blk, blk), 0))
    start = pl.multiple_of(i * blk, blk)

    for hh, cs in heads:
        q = q_ref[:, cs]
        km = km_ref[hh]
        km_hi = km.astype(BF16)
        km_lo = (km - km_hi.astype(F32)).astype(BF16)
        gate = _dot_nt(km_hi, q) + _dot_nt(km_lo, q)
        gate = jnp.where(blk_i < i, gate, -jnp.inf)
        sel_ref[hh] = _top3_mask(gate, blk_i.astype(F32), axis=0)
        t = _dot_nt(kb_ref[pl.ds(start, blk), cs], q) * c1 + bias_ref[hh]
        t = jnp.where(causal, t, NEG_BIG)
        t_ref[hh, i] = t
        m_ref[hh] = jnp.max(t, axis=0, keepdims=True)

    def shift(hh, j):
        return slope2(hh) * lax.convert_element_type((i - j) * blk, F32)

    def pass1(j0, nb):
        for hh, cs in heads:
            m = m_ref[hh]
            for j in [j0 + d for d in range(nb)]:
                off = pl.multiple_of(j * blk, blk)
                t = _dot_nt(kb_ref[pl.ds(off, blk), cs], q_ref[:, cs]) * c1 + bias_ref[hh]
                t_ref[hh, j] = t
                picked = sel_ref[hh, pl.ds(j, 1), :] > 0.0
                m_blk = jnp.max(t, axis=0, keepdims=True) - shift(hh, j)
                m = jnp.where(picked, jnp.maximum(m, m_blk), m)
            m_ref[hh] = m

    def pass2(j0, nb):
        off = pl.multiple_of(j0 * blk, blk)
        for hh, cs in heads:
            ps = []
            for j in [j0 + d for d in range(nb)]:
                picked = sel_ref[hh, pl.ds(j, 1), :] > 0.0
                sub = jnp.where(picked, m_ref[hh] + shift(hh, j), -NEG_BIG)
                ps.append(jnp.exp2(t_ref[hh, j] - sub).astype(BF16))
            p = ps[0] if nb == 1 else jnp.concatenate(ps, axis=0)
            acc_ref[hh] += _dot(vt_ref[hh, :, pl.ds(off, nb * blk)], p)

    def over_past_blocks(fn):
        done = 0
        un = MOBA_BLOCKS_PER_TRIP
        while un >= 1:
            n_trips = (i - done) // un

            def trip(t, carry, un=un, base=done):
                fn(base + t * un, un)
                return carry

            lax.fori_loop(0, n_trips, trip, 0)
            done = done + n_trips * un
            un //= 2

    over_past_blocks(pass1)
    for hh, cs in heads:
        p = jnp.exp2(t_ref[hh, i] - m_ref[hh])
        acc_ref[hh] = _dot(vt_ref[hh, :, pl.ds(start, blk)], p.astype(BF16))
    over_past_blocks(pass2)
    for hh, cs in heads:
        acc = acc_ref[hh]
        o_ref[:, cs] = (acc[:HEAD_DIM] / acc[HEAD_DIM:HEAD_DIM + 1]).T.astype(BF16)

    @pl.when(sample_g == steps_per_sample - 1)
    def _():
        _sample_pv_finish(l_ref, os_ref, accs_ref, t_q)


def _moba_prompt(q, kb, vb, slopes, bsz, seq, page_table, p, pown, l, vnew_pad, cache_v_pages, t_q):
    n_blocks = seq // MOBA_BLOCK
    blk = MOBA_BLOCK
    hps = MOBA_HEADS_PER_STEP
    n_hg = N_HEADS // hps
    width = hps * HEAD_DIM
    s_bsz, rows, _ = p.shape
    n_pages = page_table.shape[1]
    steps_per_sample = (bsz * n_hg * n_blocks) // s_bsz
    assert steps_per_sample * s_bsz == bsz * n_hg * n_blocks
    assert steps_per_sample * PAGES_PER_STEP == n_pages, "every V page is visited exactly once"
    step_keys = PAGES_PER_STEP * PAGE_SIZE

    def sample_step(b, hg, i):
        step = (b * n_hg + hg) * n_blocks + i
        return step // steps_per_sample, step % steps_per_sample

    def sample_batch(b, hg, i, pt):
        return sample_step(b, hg, i)[0], 0, 0

    def sample_keys(b, hg, i, pt):
        sb, g = sample_step(b, hg, i)
        return sb, 0, g

    qmap = lambda b, hg, i, pt: (b * n_blocks + i, hg)
    kvmap = lambda b, hg, i, pt: (b, hg)
    once = pl.Buffered(1)
    grid_spec = pltpu.PrefetchScalarGridSpec(
        num_scalar_prefetch=1,
        grid=(bsz, n_hg, n_blocks),
        in_specs=[
            pl.BlockSpec(memory_space=pltpu.SMEM),
            pl.BlockSpec((blk, width), qmap),
            pl.BlockSpec((seq, width), kvmap, pipeline_mode=once),
            pl.BlockSpec((seq, width), kvmap, pipeline_mode=once),
            pl.BlockSpec((None, rows, step_keys), sample_keys),
            pl.BlockSpec((None, rows, LANES), sample_batch),
            pl.BlockSpec((None, rows, LANES), sample_batch),
            pl.BlockSpec((None, LANES, D_ATTN), sample_batch),
        ] + _page_specs(sample_step),
        out_specs=[
            pl.BlockSpec((blk, width), qmap),
            pl.BlockSpec((None, t_q, D_ATTN), sample_batch),
        ],
        scratch_shapes=[
            pltpu.VMEM((hps, n_blocks, HEAD_DIM), F32),
            pltpu.VMEM((hps, HEAD_DIM + MOBA_ONES_ROWS, seq), BF16),
            pltpu.VMEM((hps, blk, blk), F32),
            pltpu.VMEM((hps, n_blocks, blk), F32),
            pltpu.VMEM((hps, 1, blk), F32),
            pltpu.VMEM((hps, n_blocks, blk, blk), F32),
            pltpu.VMEM((hps, HEAD_DIM + MOBA_ONES_ROWS, blk), F32),
            pltpu.VMEM((rows, D_ATTN), F32),
        ],
    )
    return pl.pallas_call(
        functools.partial(_moba_prompt_kernel, n_blocks=n_blocks,
                          steps_per_sample=steps_per_sample, t_q=t_q),
        grid_spec=grid_spec,
        out_shape=[jax.ShapeDtypeStruct((bsz * seq, D_ATTN), BF16),
                   jax.ShapeDtypeStruct((s_bsz, t_q, D_ATTN), BF16)],
        compiler_params=_cparams(3),
        name="moba_prompt",
    )(page_table, slopes, q, kb, vb, p, pown, l, vnew_pad, *([cache_v_pages] * PAGES_PER_STEP))


def _page_heads(page_ref):
    return [page_ref[pl.ds(h, PAGE_SIZE, stride=N_HEADS), :] for h in range(N_HEADS)]


def _load_page(page_ref):
    return jnp.concatenate([s.astype(BF16) for s in _page_heads(page_ref)], axis=1)


def _page_specs(batch_and_step=lambda b, g: (b, g)):
    rows = PAGE_SIZE * N_HEADS

    def spec(pp):
        def index(*args):
            *grid_idx, pt = args
            b, g = batch_and_step(*grid_idx)
            return pt[b, g * PAGES_PER_STEP + pp], 0, 0
        return pl.BlockSpec((None, rows, HEAD_DIM), index)

    return [spec(pp) for pp in range(PAGES_PER_STEP)]


def _sample_scores_kernel(pt_ref, qbd_ref, knew_ref, slope_ref, *rest, t_q, past_len):
    del pt_ref
    k_refs = rest[:PAGES_PER_STEP]
    p_ref, pown_ref, l_ref, km_ref = rest[PAGES_PER_STEP:]
    g = pl.program_id(1)
    n_steps = pl.num_programs(1)
    rows = qbd_ref.shape[0]
    blk = MOBA_BLOCK
    pages_per_block = blk // PAGE_SIZE
    n_past_blocks = past_len // blk
    qbd = qbd_ref[...]
    lane = lax.broadcasted_iota(jnp.int32, (rows, LANES), 1)
    lane_f = lane.astype(F32)

    @pl.when(g == 0)
    def _():
        km_ref[n_past_blocks:, :] = jnp.zeros((LANES - n_past_blocks, D_ATTN), F32)

    ksum = None
    km_rows = []
    for pp in range(PAGES_PER_STEP):
        heads = _page_heads(k_refs[pp])
        kpage = jnp.concatenate([s.astype(BF16) for s in heads], axis=1)
        lg = _dot_nt(qbd, kpage)
        off = pl.multiple_of((g * PAGES_PER_STEP + pp) * PAGE_SIZE, PAGE_SIZE)
        p_ref[:, pl.ds(off, PAGE_SIZE)] = lg
        psum = jnp.concatenate(
            [jnp.sum(s.reshape(PAGE_SIZE // SUBLANES, SUBLANES, HEAD_DIM), axis=0) for s in heads],
            axis=1)
        ksum = psum if pp % pages_per_block == 0 else ksum + psum
        if pp % pages_per_block == pages_per_block - 1:
            km_rows.append(jnp.sum(ksum, axis=0, keepdims=True) * (1.0 / blk))
    step_blocks = PAGES_PER_STEP // pages_per_block
    km_ref[pl.ds(pl.multiple_of(g * step_blocks, step_blocks), step_blocks), :] = (
        jnp.concatenate(km_rows, axis=0))

    @pl.when(g == n_steps - 1)
    def _():
        gate = _dot_nt(qbd, km_ref[...].astype(BF16))
        c1 = (HEAD_DIM ** -0.5) * LOG2E
        slope2 = slope_ref[...] * LOG2E
        t_row = (lax.broadcasted_iota(jnp.int32, (rows, 1), 0) % t_q).astype(F32)
        sel = _top3_mask(jnp.where(lane < n_past_blocks, gate, -jnp.inf), lane_f)

        s_own = _dot_nt(qbd, knew_ref[...]) * c1 + slope2 * lane_f
        s_own = jnp.where(lane_f <= t_row, s_own, NEG_BIG)

        cb = SAMPLE_CHUNK_BLOCKS
        ch = cb * blk
        n_chunks = n_past_blocks // cb
        assert cb & (cb - 1) == 0 and blk & (blk - 1) == 0
        widen = jnp.where(
            jnp.bitwise_and(lax.broadcasted_iota(jnp.int32, (LANES, ch), 0), cb - 1)
            == jnp.right_shift(lax.broadcasted_iota(jnp.int32, (LANES, ch), 1), blk.bit_length() - 1),
            1.0, 0.0).astype(BF16)
        key_f = lax.broadcasted_iota(jnp.int32, (1, ch), 1).astype(F32)

        mx = jnp.full((rows, blk), NEG_BIG, F32)
        for c in range(n_chunks):
            sel_c = jnp.where((lane >= c * cb) & (lane < (c + 1) * cb), sel, 0.0).astype(BF16)
            picked = _dot(sel_c, widen)
            s = p_ref[:, c * ch:(c + 1) * ch] * c1 + slope2 * (key_f + float(c * ch - past_len))
            s = jnp.where(picked > 0.0, s, NEG_BIG)
            p_ref[:, c * ch:(c + 1) * ch] = s
            for b in range(cb):
                mx = jnp.maximum(mx, s[:, b * blk:(b + 1) * blk])
        m = jnp.maximum(jnp.max(mx, axis=1, keepdims=True), jnp.max(s_own, axis=1, keepdims=True))

        p_own = jnp.exp2(s_own - m)
        pown_ref[...] = p_own
        acc = jnp.zeros((rows, blk), F32)
        for c in range(n_chunks):
            pc = jnp.exp2(p_ref[:, c * ch:(c + 1) * ch] - m)
            p_ref[:, c * ch:(c + 1) * ch] = pc
            for b in range(cb):
                acc = acc + pc[:, b * blk:(b + 1) * blk]
        l = jnp.sum(acc, axis=1, keepdims=True) + jnp.sum(p_own, axis=1, keepdims=True)
        l_ref[...] = jnp.broadcast_to(l, l_ref.shape)


def _sample_scores(page_table, qbd, knew_pad, slope_rows, cache_k_pages, t_q, past_len):
    bsz, rows, _ = qbd.shape
    n_pages = page_table.shape[1]
    n_steps = n_pages // PAGES_PER_STEP
    grid_spec = pltpu.PrefetchScalarGridSpec(
        num_scalar_prefetch=1,
        grid=(bsz, n_steps),
        in_specs=[
            pl.BlockSpec((None, rows, D_ATTN), lambda b, g, pt: (b, 0, 0)),
            pl.BlockSpec((None, LANES, D_ATTN), lambda b, g, pt: (b, 0, 0)),
            pl.BlockSpec((rows, 1), lambda b, g, pt: (0, 0)),
        ] + _page_specs(),
        out_specs=[
            pl.BlockSpec((None, rows, past_len), lambda b, g, pt: (b, 0, 0)),
            pl.BlockSpec((None, rows, LANES), lambda b, g, pt: (b, 0, 0)),
            pl.BlockSpec((None, rows, LANES), lambda b, g, pt: (b, 0, 0)),
        ],
        scratch_shapes=[pltpu.VMEM((LANES, D_ATTN), F32)],
    )
    return pl.pallas_call(
        functools.partial(_sample_scores_kernel, t_q=t_q, past_len=past_len),
        grid_spec=grid_spec,
        out_shape=[
            jax.ShapeDtypeStruct((bsz, rows, past_len), F32),
            jax.ShapeDtypeStruct((bsz, rows, LANES), F32),
            jax.ShapeDtypeStruct((bsz, rows, LANES), F32),
        ],
        compiler_params=_cparams(2),
        name="sample_scores",
    )(page_table, qbd, knew_pad, slope_rows, *([cache_k_pages] * PAGES_PER_STEP))


def _sample_pv_accumulate(g, p_ref, pown_ref, vnew_ref, v_refs, acc_ref):
    own = _dot(pown_ref[...].astype(BF16), vnew_ref[...])
    acc = jnp.where(g == 0, own, acc_ref[...])
    for pp in range(PAGES_PER_STEP):
        acc = acc + _dot(p_ref[:, pp * PAGE_SIZE:(pp + 1) * PAGE_SIZE].astype(BF16),
                         _load_page(v_refs[pp]))
    acc_ref[...] = acc


def _sample_pv_finish(l_ref, o_ref, acc_ref, t_q):
    for h in range(N_HEADS):
        rs = slice(h * t_q, (h + 1) * t_q)
        cs = slice(h * HEAD_DIM, (h + 1) * HEAD_DIM)
        o_ref[:, cs] = (acc_ref[rs, cs] / l_ref[rs, 0:1]).astype(BF16)


def _mix_rows(u_ref, vz_ref, ob_ref, ga_ref, gb_ref, x_ref, ws_ref, bst_ref, wbr_ref, wout_ref,
              nffn_ref, h_ref, hn_ref, oa_ref):
    tm = u_ref.shape[0]
    chunk = ws_ref.shape[1]
    causal = (lax.broadcasted_iota(jnp.int32, (chunk, chunk), 0)
              >= lax.broadcasted_iota(jnp.int32, (chunk, chunk), 1))
    for g in range(GMLP_GROUPS):
        cs = slice(g * GMLP_GROUP_DIM, (g + 1) * GMLP_GROUP_DIM)
        w_g = jnp.where(causal, ws_ref[g], 0.0).astype(BF16)
        b_g = bst_ref[:, g:g + 1]
        for c in range(tm // chunk):
            rs = slice(c * chunk, (c + 1) * chunk)
            mixed = _dot(w_g, vz_ref[rs, cs].astype(BF16)) + b_g
            oa_ref[rs, cs] = (u_ref[rs, cs].astype(F32) * mixed).astype(BF16)
    merged = (ga_ref[...].astype(F32) * _dot(oa_ref[...], wbr_ref[0])
              + gb_ref[...].astype(F32) * _dot(ob_ref[...], wbr_ref[1]))
    h = x_ref[...] + _dot(merged.astype(BF16), wout_ref[...])
    h_ref[...] = h
    hn_ref[...] = _rms(h, nffn_ref[...]).astype(BF16)


N_MIX_ROW_OPERANDS = 6


def _mix_kernel(*refs):
    k = N_MIX_ROW_OPERANDS
    prompt_rows, sample_rows = refs[:k], refs[k:2 * k]
    ws_ref, bst_ref, wss_ref, bsts_ref, wbr_ref, wout_ref, nffn_ref = refs[2 * k:2 * k + 7]
    h_ref, hn_ref, hs_ref, hns_ref, oa_ref, oas_ref = refs[2 * k + 7:]
    _mix_rows(*prompt_rows, ws_ref, bst_ref, wbr_ref, wout_ref, nffn_ref, h_ref, hn_ref, oa_ref)

    @pl.when(pl.program_id(0) == 0)
    def _():
        _mix_rows(*sample_rows, wss_ref, bsts_ref, wbr_ref, wout_ref, nffn_ref, hs_ref, hns_ref,
                  oas_ref)


def _mix(prompt_rows, sample_rows, ws_chunk, bs_t, ws_chunk_s, bs_t_s, w_br_b, w_out_b, norm_ffn, tm):
    m_rows = prompt_rows[-1].shape[0]
    s_rows = sample_rows[-1].shape[0]
    widths = (D_GMLP, D_GMLP, D_ATTN, D_MODEL, D_MODEL, D_MODEL)
    row = lambda m: (m, 0)
    const2 = lambda m: (0, 0)
    const3 = lambda m: (0, 0, 0)
    once = pl.Buffered(1)

    def chunk_specs(ws):
        chunk = ws.shape[1]
        return [pl.BlockSpec((GMLP_GROUPS, chunk, chunk), const3, pipeline_mode=once),
                pl.BlockSpec((chunk, GMLP_GROUPS), const2, pipeline_mode=once)]

    return pl.pallas_call(
        _mix_kernel,
        grid=(m_rows // tm,),
        in_specs=(
            [pl.BlockSpec((tm, w), row) for w in widths]
            + [pl.BlockSpec((s_rows, w), const2) for w in widths]
            + chunk_specs(ws_chunk) + chunk_specs(ws_chunk_s)
            + [pl.BlockSpec((2, D_GMLP, D_MODEL), const3, pipeline_mode=once),
               pl.BlockSpec((D_MODEL, D_MODEL), const2, pipeline_mode=once),
               pl.BlockSpec((1, D_MODEL), const2, pipeline_mode=once)]),
        out_specs=[pl.BlockSpec((tm, D_MODEL), row), pl.BlockSpec((tm, D_MODEL), row),
                   pl.BlockSpec((s_rows, D_MODEL), const2), pl.BlockSpec((s_rows, D_MODEL), const2)],
        out_shape=[jax.ShapeDtypeStruct((m_rows, D_MODEL), F32),
                   jax.ShapeDtypeStruct((m_rows, D_MODEL), BF16),
                   jax.ShapeDtypeStruct((s_rows, D_MODEL), F32),
                   jax.ShapeDtypeStruct((s_rows, D_MODEL), BF16)],
        scratch_shapes=[pltpu.VMEM((tm, D_GMLP), BF16), pltpu.VMEM((s_rows, D_GMLP), BF16)],
        compiler_params=_cparams(1),
        name="mix",
    )(*prompt_rows, *sample_rows, ws_chunk, bs_t, ws_chunk_s, bs_t_s, w_br_b, w_out_b,
      norm_ffn.reshape(1, D_MODEL))


def _ffn_tile(hn, wa, wb, wd, wc, bc, history):
    a = _dot(hn, wa)
    b = _dot(hn, wb)
    a1, a2 = history(a, pltpu.roll(a, 1, 0), pltpu.roll(a, 2, 0))
    conv = bc + a * wc[2:3, :] + a2 * wc[0:1, :] + a1 * wc[1:2, :]
    act = (jax.nn.gelu(conv) * b).astype(BF16)
    return a, _dot(act, wd)


def _ffn_kernel(hn_ref, wa_ref, wb_ref, wc_ref, bc_ref, wd_ref, h_ref, nfin_ref, init_ref,
                hns_ref, hs_ref, inits_ref, y_ref, tail_ref, ys_ref, tails_ref, halo_ref,
                *, tiles_per_seq, sample_seq):
    m = pl.program_id(0)
    n = pl.program_id(1)
    last_n = pl.num_programs(1) - 1
    tm = hn_ref.shape[0]
    tf = wa_ref.shape[1]

    @pl.when((m == 0) & (n == 0))
    def _():
        halo_ref[...] = jnp.zeros_like(halo_ref)

    @pl.when(n == 0)
    def _():
        y_ref[...] = h_ref[...]

    wa = wa_ref[...].astype(BF16)
    wb = wb_ref[...].astype(BF16)
    wd = wd_ref[...].astype(BF16)
    wc = wc_ref[...]
    bc = bc_ref[...]

    def carried_history(a, r1, r2):
        row = lax.broadcasted_iota(jnp.int32, (tm, tf), 0)
        prev = jnp.where(m % tiles_per_seq == 0, init_ref[0], halo_ref[n])
        p1 = prev[SUBLANES - 1:SUBLANES, :]
        p2 = prev[SUBLANES - 2:SUBLANES - 1, :]
        return (jnp.where(row == 0, p1, r1),
                jnp.where(row == 0, p2, jnp.where(row == 1, p1, r2)))

    a, part = _ffn_tile(hn_ref[...], wa, wb, wd, wc, bc, carried_history)
    last = a[tm - SUBLANES:, :]
    halo_ref[n] = last
    tail_ref[...] = last
    y_ref[...] += part

    @pl.when(n == last_n)
    def _():
        y_ref[...] = _rms(y_ref[...], nfin_ref[...])

    @pl.when(m == 0)
    def _():
        @pl.when(n == 0)
        def _():
            ys_ref[...] = hs_ref[...]

        def given_history(a_s, r1, r2):
            pos = lax.broadcasted_iota(jnp.int32, a_s.shape, 0) % sample_seq
            return jnp.where(pos >= 1, r1, inits_ref[0]), jnp.where(pos >= 2, r2, inits_ref[1])

        a_s, part_s = _ffn_tile(hns_ref[...], wa, wb, wd, wc, bc, given_history)
        tails_ref[...] = a_s
        ys_ref[...] += part_s

        @pl.when(n == last_n)
        def _():
            ys_ref[...] = _rms(ys_ref[...], nfin_ref[...])


def _ffn(hn, h, hn_s, h_s, w_up, w_conv, b_conv, w_down, norm_final, init, init_s, tm, seq_len,
         sample_seq):
    m_rows = hn.shape[0]
    s_rows = hn_s.shape[0]
    tf = FF_TILE
    n_ff = D_FF // tf
    tiles_per_seq = seq_len // tm
    row = lambda m, n: (m, 0)
    const = lambda m, n: (0, 0)
    sample_tile = lambda m, n: jnp.where(m == 0, n, n_ff - 1)
    once = pl.Buffered(1)
    return pl.pallas_call(
        functools.partial(_ffn_kernel, tiles_per_seq=tiles_per_seq, sample_seq=sample_seq),
        grid=(m_rows // tm, n_ff),
        in_specs=[
            pl.BlockSpec((tm, D_MODEL), row, pipeline_mode=once),
            pl.BlockSpec((D_MODEL, tf), lambda m, n: (0, n)),
            pl.BlockSpec((D_MODEL, tf), lambda m, n: (0, n_ff + n)),
            pl.BlockSpec((CONV_W, tf), lambda m, n: (0, n)),
            pl.BlockSpec((1, tf), lambda m, n: (0, n)),
            pl.BlockSpec((tf, D_MODEL), lambda m, n: (n, 0)),
            pl.BlockSpec((tm, D_MODEL), row, pipeline_mode=once),
            pl.BlockSpec((1, D_MODEL), const),
            pl.BlockSpec((1, SUBLANES, tf), lambda m, n: (0, m // tiles_per_seq, n)),
            pl.BlockSpec((s_rows, D_MODEL), const),
            pl.BlockSpec((s_rows, D_MODEL), const),
            pl.BlockSpec((2, s_rows, tf), lambda m, n: (0, 0, sample_tile(m, n))),
        ],
        out_specs=[
            pl.BlockSpec((tm, D_MODEL), row, pipeline_mode=once),
            pl.BlockSpec((SUBLANES, tf), lambda m, n: (m, n)),
            pl.BlockSpec((s_rows, D_MODEL), const),
            pl.BlockSpec((s_rows, tf), lambda m, n: (0, sample_tile(m, n))),
        ],
        out_shape=[
            jax.ShapeDtypeStruct((m_rows, D_MODEL), F32),
            jax.ShapeDtypeStruct(((m_rows // tm) * SUBLANES, D_FF), F32),
            jax.ShapeDtypeStruct((s_rows, D_MODEL), F32),
            jax.ShapeDtypeStruct((s_rows, D_FF), F32),
        ],
        scratch_shapes=[pltpu.VMEM((n_ff, SUBLANES, tf), F32)],
        compiler_params=_cparams(2),
        name="ffn",
    )(hn, w_up, w_up, w_conv, b_conv.reshape(1, D_FF), w_down, h,
      norm_final.reshape(1, D_MODEL), init, hn_s, h_s, init_s)


def _alibi_slopes():
    return jnp.exp2(-8.0 * jnp.arange(1, N_HEADS + 1, dtype=F32) / N_HEADS)


def kernel(x_prompt, x_sample, cache_k, cache_v, state_ffn_conv, page_table, norm_mix, w_in,
           norm_gmlp_v, w_spatial, b_spatial, w_branch, w_out, norm_ffn, w_up, w_conv, b_conv,
           w_down, norm_final):
    assert w_in.shape[0] == 1, "single layer"
    bp, seq, _ = x_prompt.shape
    bs, t_q, _ = x_sample.shape
    n_pages = page_table.shape[1]
    past_len = n_pages * PAGE_SIZE
    assert seq % MOBA_BLOCK == 0 and past_len % MOBA_BLOCK == 0 and past_len % CHUNK == 0
    assert t_q <= SUBLANES and past_len // MOBA_BLOCK <= LANES and seq // MOBA_BLOCK <= LANES

    slopes = _alibi_slopes()
    w_br_b = w_branch[0].astype(BF16)
    w_out_b = w_out[0].astype(BF16)
    ws, bsp = w_spatial[0], b_spatial[0]

    m_s = bs * t_q
    xp = x_prompt.reshape(bp * seq, D_MODEL)
    xs = x_sample.reshape(m_s, D_MODEL)
    xn_p = _norm_bf16(xp, norm_mix[0], PROMPT_TM_NORM)
    xn_s = _norm_bf16(xs, norm_mix[0], m_s)
    (u, vz, q, k, kb, v, vb, ga, gb), (us, vzs, qs, ks, kbs, vs, vbs, gas, gbs) = _inproj(
        xn_p, xn_s, w_in[0], norm_gmlp_v[0], PROMPT_TM_INPROJ)

    q4 = qs.reshape(bs, t_q, N_HEADS, HEAD_DIM)
    eye = jnp.eye(N_HEADS, dtype=BF16)
    qbd = (q4.transpose(0, 2, 1, 3)[:, :, :, None, :] * eye[None, :, None, :, None]
           ).reshape(bs, N_HEADS * t_q, D_ATTN)
    pad_rows = ((0, 0), (0, LANES - t_q), (0, 0))
    knew = jnp.pad(kbs.reshape(bs, t_q, D_ATTN), pad_rows)
    vnew = jnp.pad(vbs.reshape(bs, t_q, D_ATTN), pad_rows)
    slope_rows = jnp.repeat(slopes, t_q).reshape(N_HEADS * t_q, 1)
    ck = cache_k[0].reshape(-1, PAGE_SIZE * N_HEADS, HEAD_DIM)
    cv = cache_v[0].reshape(-1, PAGE_SIZE * N_HEADS, HEAD_DIM)
    p, pown, l = _sample_scores(page_table, qbd, knew, slope_rows, ck, t_q, past_len)

    ob, obs = _moba_prompt(q, kb, vb, slopes, bp, seq, page_table, p, pown, l, vnew, cv, t_q)
    obs = obs.reshape(m_s, D_ATTN)
    ws_s = (jnp.eye(bs, dtype=F32)[None, :, None, :, None]
            * ws[:, None, :t_q, None, :t_q]).reshape(GMLP_GROUPS, m_s, m_s)
    bs_t_s = jnp.tile(bsp[:, :t_q], (1, bs)).T
    h, hn, hs, hns = _mix((u, vz, ob, ga, gb, xp), (us, vzs, obs, gas, gbs, xs), ws, bsp.T, ws_s,
                          bs_t_s, w_br_b, w_out_b, norm_ffn[0], PROMPT_TM_MIX)
    st = state_ffn_conv[0]
    zero_rows = jnp.zeros((bs, t_q - 1, D_FF), F32)
    init1 = jnp.concatenate([st[:, 1:2], zero_rows], axis=1)
    init2 = jnp.concatenate([st[:, 0:1], st[:, 1:2], zero_rows[:, 1:]], axis=1)
    init_s = jnp.stack([init1.reshape(m_s, D_FF), init2.reshape(m_s, D_FF)])

    zero_state = jnp.zeros((1, bp * SUBLANES, D_FF), F32)
    tm_p = PROMPT_TM_FFN
    yp, tail_p, ys, tail_s = _ffn(hn, h, hns, hs, w_up[0], w_conv[0], b_conv[0], w_down[0],
                                  norm_final, zero_state, init_s, tm_p, seq, t_q)

    y_prompt = yp.reshape(bp, seq, D_MODEL)
    gv_p = vz.reshape(bp, seq, D_GMLP)[:, seq - CHUNK:][None]
    k_p = k.reshape(1, bp, seq, N_HEADS, HEAD_DIM)
    v_p = v.reshape(1, bp, seq, N_HEADS, HEAD_DIM)
    c_p = tail_p.reshape(bp, seq // tm_p, SUBLANES, D_FF)[:, -1, SUBLANES - (CONV_W - 1):][None]
    y_sample = ys.reshape(bs, t_q, D_MODEL)
    gv_s = vzs.reshape(1, bs, t_q, D_GMLP)
    k_s = ks.reshape(1, bs, t_q, N_HEADS, HEAD_DIM)
    v_s = vs.reshape(1, bs, t_q, N_HEADS, HEAD_DIM)
    c_s = tail_s.reshape(bs, t_q, D_FF)[:, t_q - (CONV_W - 1):][None]

    return (y_prompt, y_sample, gv_p, gv_s, k_p, v_p, k_s, v_s, c_p, c_s)
```

```python
import functools

import jax
import jax.numpy as jnp
from jax import lax
from jax.experimental import pallas as pl
from jax.experimental.pallas import tpu as pltpu

F32 = jnp.float32
BF16 = jnp.bfloat16

D_MODEL = 2048
D_GMLP = D_MODEL // 2
GMLP_GROUPS = 8
GMLP_GROUP_DIM = D_GMLP // GMLP_GROUPS
CHUNK = 128
D_ATTN = D_MODEL // 2
HEAD_DIM = 128
N_HEADS = D_ATTN // HEAD_DIM
MOBA_BLOCK = 256
MOBA_TOPK = 3
D_FF = 5632
CONV_W = 3
RMS_EPS = 1e-6
PAGE_SIZE = 128
N_SECTIONS = 9

LANES = 128
SUBLANES = 8
V7X_VMEM_BYTES = 64 * 1024 * 1024
VMEM_LIMIT_BYTES = V7X_VMEM_BYTES * 7 // 8

NEG_BIG = -1e30
LOG2E = 1.4426950408889634

MOBA_HEADS_PER_STEP = 4
MOBA_BLOCKS_PER_TRIP = 8
MOBA_ONES_ROWS = 16
SAMPLE_CHUNK_BLOCKS = 8
PAGES_PER_STEP = 16
FF_TILE = 512
PROMPT_TM_NORM = 1024
PROMPT_TM_INPROJ = 512
PROMPT_TM_MIX = 256
PROMPT_TM_FFN = 1024

_NT = (((1,), (1,)), ((), ()))


def _dot(a, b):
    return jnp.dot(a, b, preferred_element_type=F32)


def _dot_nt(a, b):
    return lax.dot_general(a, b, _NT, preferred_element_type=F32)


def _rms(x, g):
    return x * lax.rsqrt(jnp.mean(x * x, axis=-1, keepdims=True) + RMS_EPS) * g


def _sigmoid(x):
    return 0.5 * (jnp.tanh(0.5 * x) + 1.0)


def _cparams(n_axes):
    return pltpu.CompilerParams(dimension_semantics=("arbitrary",) * n_axes,
                                vmem_limit_bytes=VMEM_LIMIT_BYTES)


def _top3_mask(gate, idx_f, axis=1):
    sel = jnp.zeros_like(gate)
    for _ in range(MOBA_TOPK):
        mx = jnp.max(gate, axis=axis, keepdims=True)
        first = jnp.min(jnp.where(gate == mx, idx_f, float(LANES)), axis=axis, keepdims=True)
        pick = (idx_f == first) & (mx > -jnp.inf)
        sel = jnp.where(pick, 1.0, sel)
        gate = jnp.where(pick, -jnp.inf, gate)
    return sel


def _store_heads(dst_ref, dst_b_ref, acc):
    tm = acc.shape[0]
    dst_b_ref[...] = acc.astype(BF16)
    for h in range(N_HEADS):
        dst_ref[pl.ds(h, tm, stride=N_HEADS), :] = acc[:, h * HEAD_DIM:(h + 1) * HEAD_DIM]


def _norm_kernel(x_ref, g_ref, o_ref):
    o_ref[...] = _rms(x_ref[...], g_ref[...]).astype(BF16)


def _norm_bf16(x, g, tm):
    m_rows, d = x.shape
    return pl.pallas_call(
        _norm_kernel,
        grid=(m_rows // tm,),
        in_specs=[pl.BlockSpec((tm, d), lambda m: (m, 0)), pl.BlockSpec((1, d), lambda m: (0, 0))],
        out_specs=pl.BlockSpec((tm, d), lambda m: (m, 0)),
        out_shape=jax.ShapeDtypeStruct((m_rows, d), BF16),
        compiler_params=_cparams(1),
        name="norm",
    )(x, g.reshape(1, d))


N_INPROJ_OUTS = 9


def _inproj_kernel(xn_ref, xs_ref, w_ref, ngv_ref, *rest):
    prompt = rest[:N_INPROJ_OUTS]
    sample = rest[N_INPROJ_OUTS:2 * N_INPROJ_OUTS]
    wb_ref = rest[2 * N_INPROJ_OUTS]
    n = pl.program_id(0)
    first_tile = pl.program_id(1) == 0

    @pl.when(first_tile)
    def _():
        wb_ref[...] = w_ref[...].astype(BF16)

    def emit(pred, epilogue):
        @pl.when(pred)
        def _():
            epilogue(_dot(xn_ref[...], wb_ref[...]), prompt)

            @pl.when(first_tile)
            def _():
                epilogue(_dot(xs_ref[...], wb_ref[...]), sample)

    def gelu_u(acc, outs):
        outs[0][...] = jax.nn.gelu(acc).astype(BF16)

    def gelu_norm_v(acc, outs):
        outs[1][...] = _rms(jax.nn.gelu(acc), ngv_ref[...])

    def plain_q(acc, outs):
        outs[2][...] = acc.astype(BF16)

    def heads_k(acc, outs):
        _store_heads(outs[3], outs[4], acc)

    def heads_v(acc, outs):
        _store_heads(outs[5], outs[6], acc)

    def gate_a(acc, outs):
        outs[7][...] = _sigmoid(acc).astype(BF16)

    def gate_b(acc, outs):
        outs[8][...] = _sigmoid(acc).astype(BF16)

    emit(n == 0, gelu_u)
    emit(n == 1, gelu_norm_v)
    emit(n == 2, plain_q)
    emit(n == 3, heads_k)
    emit(n == 4, heads_v)
    emit((n == 5) | (n == 6), gate_a)
    emit(n >= 7, gate_b)


def _inproj(xn, xn_s, w_in, norm_gmlp_v, tm):
    m_rows = xn.shape[0]
    s_rows = xn_s.shape[0]
    sec = D_GMLP
    last = m_rows // tm - 1

    def rows(first, count):
        def index(n, m):
            return jnp.where(n < first, 0, jnp.where(n >= first + count, last, m))
        return index

    def col(first, count):
        return lambda n: jnp.clip(n - first, 0, count - 1)

    def out(first, count=1):
        r, c = rows(first, count), col(first, count)
        return pl.BlockSpec((tm, sec), lambda n, m: (r(n, m), c(n)))

    def out_heads(first):
        r = rows(first, 1)
        return pl.BlockSpec((tm * N_HEADS, HEAD_DIM), lambda n, m: (r(n, m), 0))

    def out_s(first, count=1):
        c = col(first, count)
        return pl.BlockSpec((s_rows, sec), lambda n, m: (0, c(n)))

    out_heads_s = pl.BlockSpec((s_rows * N_HEADS, HEAD_DIM), lambda n, m: (0, 0))

    def shapes(r):
        return [
            jax.ShapeDtypeStruct((r, sec), BF16),
            jax.ShapeDtypeStruct((r, sec), F32),
            jax.ShapeDtypeStruct((r, sec), BF16),
            jax.ShapeDtypeStruct((r * N_HEADS, HEAD_DIM), F32),
            jax.ShapeDtypeStruct((r, sec), BF16),
            jax.ShapeDtypeStruct((r * N_HEADS, HEAD_DIM), F32),
            jax.ShapeDtypeStruct((r, sec), BF16),
            jax.ShapeDtypeStruct((r, D_MODEL), BF16),
            jax.ShapeDtypeStruct((r, D_MODEL), BF16),
        ]

    in_specs = [
        pl.BlockSpec((tm, D_MODEL), lambda n, m: (m, 0)),
        pl.BlockSpec((s_rows, D_MODEL), lambda n, m: (0, 0)),
        pl.BlockSpec((D_MODEL, sec), lambda n, m: (0, n)),
        pl.BlockSpec((1, sec), lambda n, m: (0, 0)),
    ]
    out_specs = [out(0), out(1), out(2), out_heads(3), out(3), out_heads(4), out(4), out(5, 2),
                 out(7, 2),
                 out_s(0), out_s(1), out_s(2), out_heads_s, out_s(3), out_heads_s, out_s(4),
                 out_s(5, 2), out_s(7, 2)]
    outs = pl.pallas_call(
        _inproj_kernel,
        grid=(N_SECTIONS, m_rows // tm),
        in_specs=in_specs,
        out_specs=out_specs,
        out_shape=shapes(m_rows) + shapes(s_rows),
        scratch_shapes=[pltpu.VMEM((D_MODEL, sec), BF16)],
        compiler_params=_cparams(2),
        name="inproj",
    )(xn, xn_s, w_in, norm_gmlp_v.reshape(1, sec))
    return outs[:N_INPROJ_OUTS], outs[N_INPROJ_OUTS:]


def _moba_prompt_kernel(pt_ref, slopes_ref, q_ref, kb_ref, vb_ref, p_ref, pown_ref, l_ref, vnew_ref,
                        *rest, n_blocks, steps_per_sample, t_q):
    (cachev_ref, o_ref, os_ref, km_ref, vt_ref, bias_ref, sel_ref, m_ref, t_ref, acc_ref,
     accs_ref, vbuf_ref, vsem_ref) = rest
    hg = pl.program_id(1)
    i = pl.program_id(2)
    step = (pl.program_id(0) * pl.num_programs(1) + hg) * n_blocks + i
    total = pl.num_programs(0) * pl.num_programs(1) * n_blocks
    sample_g = step % steps_per_sample
    blk = MOBA_BLOCK

    @pl.when(step == 0)
    def _():
        accs_ref[...] = jnp.zeros_like(accs_ref)

    v_slot = _page_ring_step(pt_ref, cachev_ref, vbuf_ref, vsem_ref, step, total, steps_per_sample)
    c1 = (HEAD_DIM ** -0.5) * LOG2E
    heads = [(hh, slice(hh * HEAD_DIM, (hh + 1) * HEAD_DIM)) for hh in range(MOBA_HEADS_PER_STEP)]

    def slope2(hh):
        return slopes_ref[hg * MOBA_HEADS_PER_STEP + hh] * LOG2E

    @pl.when(i == 0)
    def _():
        key_f = lax.broadcasted_iota(jnp.int32, (blk, blk), 0).astype(F32)
        for hh, cs in heads:
            bias_ref[hh] = slope2(hh) * key_f
            vt_ref[hh, HEAD_DIM:, :] = jnp.ones((MOBA_ONES_ROWS, n_blocks * blk), BF16)
            for j in range(n_blocks):
                rs = slice(j * blk, (j + 1) * blk)
                km_ref[hh, j:j + 1, :] = jnp.sum(kb_ref[rs, cs].astype(F32), axis=0,
                                                 keepdims=True) * (1.0 / blk)
                vt_ref[hh, :HEAD_DIM, rs] = vb_ref[rs, cs].astype(F32).T.astype(BF16)

    _sample_pv_accumulate(sample_g, p_ref, pown_ref, vnew_ref, vbuf_ref, v_slot, accs_ref)

    blk_i = lax.broadcasted_iota(jnp.int32, (n_blocks, blk), 0)
    causal = (lax.broadcasted_iota(jnp.int32, (blk, blk), 1)
              >= lax.broadcasted_iota(jnp.int32, (blk, blk), 0))
    start = pl.multiple_of(i * blk, blk)

    for hh, cs in heads:
        q = q_ref[:, cs]
        km = km_ref[hh]
        km_hi = km.astype(BF16)
        km_lo = (km - km_hi.astype(F32)).astype(BF16)
        gate = _dot_nt(km_hi, q) + _dot_nt(km_lo, q)
        gate = jnp.where(blk_i < i, gate, -jnp.inf)
        sel_ref[hh] = _top3_mask(gate, blk_i.astype(F32), axis=0)
        t = _dot_nt(kb_ref[pl.ds(start, blk), cs], q) * c1 + bias_ref[hh]
        t = jnp.where(causal, t, NEG_BIG)
        t_ref[hh, i] = t
        m_ref[hh] = jnp.max(t, axis=0, keepdims=True)

    def shift(hh, j):
        return slope2(hh) * lax.convert_element_type((i - j) * blk, F32)

    def pass1(j0, nb):
        for hh, cs in heads:
            m = m_ref[hh]
            for j in [j0 + d for d in range(nb)]:
                off = pl.multiple_of(j * blk, blk)
                t = _dot_nt(kb_ref[pl.ds(off, blk), cs], q_ref[:, cs]) * c1 + bias_ref[hh]
                t_ref[hh, j] = t
                picked = sel_ref[hh, pl.ds(j, 1), :] > 0.0
                m_blk = jnp.max(t, axis=0, keepdims=True) - shift(hh, j)
                m = jnp.where(picked, jnp.maximum(m, m_blk), m)
            m_ref[hh] = m

    def pass2(j0, nb):
        off = pl.multiple_of(j0 * blk, blk)
        for hh, cs in heads:
            ps = []
            for j in [j0 + d for d in range(nb)]:
                picked = sel_ref[hh, pl.ds(j, 1), :] > 0.0
                sub = jnp.where(picked, m_ref[hh] + shift(hh, j), -NEG_BIG)
                ps.append(jnp.exp2(t_ref[hh, j] - sub).astype(BF16))
            p = ps[0] if nb == 1 else jnp.concatenate(ps, axis=0)
            acc_ref[hh] += _dot(vt_ref[hh, :, pl.ds(off, nb * blk)], p)

    def over_past_blocks(fn):
        done = 0
        un = MOBA_BLOCKS_PER_TRIP
        while un >= 1:
            n_trips = (i - done) // un

            def trip(t, carry, un=un, base=done):
                fn(base + t * un, un)
                return carry

            lax.fori_loop(0, n_trips, trip, 0)
            done = done + n_trips * un
            un //= 2

    over_past_blocks(pass1)
    for hh, cs in heads:
        p = jnp.exp2(t_ref[hh, i] - m_ref[hh])
        acc_ref[hh] = _dot(vt_ref[hh, :, pl.ds(start, blk)], p.astype(BF16))
    over_past_blocks(pass2)
    for hh, cs in heads:
        acc = acc_ref[hh]
        o_ref[:, cs] = (acc[:HEAD_DIM] / acc[HEAD_DIM:HEAD_DIM + 1]).T.astype(BF16)

    @pl.when(sample_g == steps_per_sample - 1)
    def _():
        _sample_pv_finish(l_ref, os_ref, accs_ref, t_q)


def _moba_prompt(q, kb, vb, slopes, bsz, seq, page_table, p, pown, l, vnew_pad, cache_v_pages, t_q):
    n_blocks = seq // MOBA_BLOCK
    blk = MOBA_BLOCK
    hps = MOBA_HEADS_PER_STEP
    n_hg = N_HEADS // hps
    width = hps * HEAD_DIM
    s_bsz, rows, _ = p.shape
    n_pages = page_table.shape[1]
    steps_per_sample = (bsz * n_hg * n_blocks) // s_bsz
    assert steps_per_sample * s_bsz == bsz * n_hg * n_blocks
    assert steps_per_sample * PAGES_PER_STEP == n_pages, "every V page is visited exactly once"
    step_keys = PAGES_PER_STEP * PAGE_SIZE

    def sample_step(b, hg, i):
        step = (b * n_hg + hg) * n_blocks + i
        return step // steps_per_sample, step % steps_per_sample

    def sample_batch(b, hg, i, pt):
        return sample_step(b, hg, i)[0], 0, 0

    def sample_keys(b, hg, i, pt):
        sb, g = sample_step(b, hg, i)
        return sb, 0, g

    qmap = lambda b, hg, i, pt: (b * n_blocks + i, hg)
    kvmap = lambda b, hg, i, pt: (b, hg)
    once = pl.Buffered(1)
    grid_spec = pltpu.PrefetchScalarGridSpec(
        num_scalar_prefetch=1,
        grid=(bsz, n_hg, n_blocks),
        in_specs=[
            pl.BlockSpec(memory_space=pltpu.SMEM),
            pl.BlockSpec((blk, width), qmap),
            pl.BlockSpec((seq, width), kvmap, pipeline_mode=once),
            pl.BlockSpec((seq, width), kvmap, pipeline_mode=once),
            pl.BlockSpec((None, rows, step_keys), sample_keys),
            pl.BlockSpec((None, rows, LANES), sample_batch),
            pl.BlockSpec((None, rows, LANES), sample_batch),
            pl.BlockSpec((None, LANES, D_ATTN), sample_batch),
            pl.BlockSpec(memory_space=pl.ANY),
        ],
        out_specs=[
            pl.BlockSpec((blk, width), qmap),
            pl.BlockSpec((None, t_q, D_ATTN), sample_batch),
        ],
        scratch_shapes=[
            pltpu.VMEM((hps, n_blocks, HEAD_DIM), F32),
            pltpu.VMEM((hps, HEAD_DIM + MOBA_ONES_ROWS, seq), BF16),
            pltpu.VMEM((hps, blk, blk), F32),
            pltpu.VMEM((hps, n_blocks, blk), F32),
            pltpu.VMEM((hps, 1, blk), F32),
            pltpu.VMEM((hps, n_blocks, blk, blk), F32),
            pltpu.VMEM((hps, HEAD_DIM + MOBA_ONES_ROWS, blk), F32),
            pltpu.VMEM((rows, D_ATTN), F32),
        ] + _page_ring_scratch(),
    )
    return pl.pallas_call(
        functools.partial(_moba_prompt_kernel, n_blocks=n_blocks,
                          steps_per_sample=steps_per_sample, t_q=t_q),
        grid_spec=grid_spec,
        out_shape=[jax.ShapeDtypeStruct((bsz * seq, D_ATTN), BF16),
                   jax.ShapeDtypeStruct((s_bsz, t_q, D_ATTN), BF16)],
        compiler_params=_cparams(3),
        name="moba_prompt",
    )(page_table, slopes, q, kb, vb, p, pown, l, vnew_pad, cache_v_pages)


PAGE_RING_SLOTS = 2


def _page_ring_scratch():
    return [pltpu.VMEM((PAGE_RING_SLOTS, PAGES_PER_STEP, N_HEADS, PAGE_SIZE, HEAD_DIM), F32),
            pltpu.SemaphoreType.DMA((PAGE_RING_SLOTS,))]


def _page_copies(pt_ref, cache_ref, buf_ref, sem_ref, step, n_steps, slot):
    b = step // n_steps
    g = step % n_steps
    copies = []
    for pp in range(PAGES_PER_STEP):
        page = pt_ref[b, g * PAGES_PER_STEP + pp]
        for h in range(N_HEADS):
            copies.append(pltpu.make_async_copy(cache_ref.at[page, :, h, :],
                                                buf_ref.at[slot, pp, h], sem_ref.at[slot]))
    return copies


def _page_ring_step(pt_ref, cache_ref, buf_ref, sem_ref, step, total, n_steps):
    slot = step % PAGE_RING_SLOTS

    @pl.when(step == 0)
    def _():
        for c in _page_copies(pt_ref, cache_ref, buf_ref, sem_ref, step, n_steps, slot):
            c.start()

    @pl.when(step + 1 < total)
    def _():
        for c in _page_copies(pt_ref, cache_ref, buf_ref, sem_ref, step + 1, n_steps, 1 - slot):
            c.start()

    for c in _page_copies(pt_ref, cache_ref, buf_ref, sem_ref, step, n_steps, slot):
        c.wait()
    return slot


def _page_bf16(buf_ref, slot, pp):
    return jnp.concatenate([buf_ref[slot, pp, h].astype(BF16) for h in range(N_HEADS)], axis=1)


def _sample_scores_kernel(pt_ref, qbd_ref, knew_ref, slope_ref, cache_ref, p_ref, pown_ref, l_ref,
                          km_ref, kbuf_ref, sem_ref, *, t_q, past_len):
    g = pl.program_id(1)
    n_steps = pl.num_programs(1)
    slot = _page_ring_step(pt_ref, cache_ref, kbuf_ref, sem_ref, pl.program_id(0) * n_steps + g,
                           pl.num_programs(0) * n_steps, n_steps)
    rows = qbd_ref.shape[0]
    blk = MOBA_BLOCK
    pages_per_block = blk // PAGE_SIZE
    n_past_blocks = past_len // blk
    qbd = qbd_ref[...]
    lane = lax.broadcasted_iota(jnp.int32, (rows, LANES), 1)
    lane_f = lane.astype(F32)

    @pl.when(g == 0)
    def _():
        km_ref[n_past_blocks:, :] = jnp.zeros((LANES - n_past_blocks, D_ATTN), F32)

    ksum = None
    km_rows = []
    for pp in range(PAGES_PER_STEP):
        heads = [kbuf_ref[slot, pp, h] for h in range(N_HEADS)]
        kpage = jnp.concatenate([s.astype(BF16) for s in heads], axis=1)
        lg = _dot_nt(qbd, kpage)
        off = pl.multiple_of((g * PAGES_PER_STEP + pp) * PAGE_SIZE, PAGE_SIZE)
        p_ref[:, pl.ds(off, PAGE_SIZE)] = lg
        psum = jnp.concatenate(
            [jnp.sum(s.reshape(PAGE_SIZE // SUBLANES, SUBLANES, HEAD_DIM), axis=0) for s in heads],
            axis=1)
        ksum = psum if pp % pages_per_block == 0 else ksum + psum
        if pp % pages_per_block == pages_per_block - 1:
            km_rows.append(jnp.sum(ksum, axis=0, keepdims=True) * (1.0 / blk))
    step_blocks = PAGES_PER_STEP // pages_per_block
    km_ref[pl.ds(pl.multiple_of(g * step_blocks, step_blocks), step_blocks), :] = (
        jnp.concatenate(km_rows, axis=0))

    @pl.when(g == n_steps - 1)
    def _():
        gate = _dot_nt(qbd, km_ref[...].astype(BF16))
        c1 = (HEAD_DIM ** -0.5) * LOG2E
        slope2 = slope_ref[...] * LOG2E
        t_row = (lax.broadcasted_iota(jnp.int32, (rows, 1), 0) % t_q).astype(F32)
        sel = _top3_mask(jnp.where(lane < n_past_blocks, gate, -jnp.inf), lane_f)

        s_own = _dot_nt(qbd, knew_ref[...]) * c1 + slope2 * lane_f
        s_own = jnp.where(lane_f <= t_row, s_own, NEG_BIG)

        cb = SAMPLE_CHUNK_BLOCKS
        ch = cb * blk
        n_chunks = n_past_blocks // cb
        assert cb & (cb - 1) == 0 and blk & (blk - 1) == 0
        widen = jnp.where(
            jnp.bitwise_and(lax.broadcasted_iota(jnp.int32, (LANES, ch), 0), cb - 1)
            == jnp.right_shift(lax.broadcasted_iota(jnp.int32, (LANES, ch), 1), blk.bit_length() - 1),
            1.0, 0.0).astype(BF16)
        key_f = lax.broadcasted_iota(jnp.int32, (1, ch), 1).astype(F32)

        mx = jnp.full((rows, blk), NEG_BIG, F32)
        for c in range(n_chunks):
            sel_c = jnp.where((lane >= c * cb) & (lane < (c + 1) * cb), sel, 0.0).astype(BF16)
            picked = _dot(sel_c, widen)
            s = p_ref[:, c * ch:(c + 1) * ch] * c1 + slope2 * (key_f + float(c * ch - past_len))
            s = jnp.where(picked > 0.0, s, NEG_BIG)
            p_ref[:, c * ch:(c + 1) * ch] = s
            for b in range(cb):
                mx = jnp.maximum(mx, s[:, b * blk:(b + 1) * blk])
        m = jnp.maximum(jnp.max(mx, axis=1, keepdims=True), jnp.max(s_own, axis=1, keepdims=True))

        p_own = jnp.exp2(s_own - m)
        pown_ref[...] = p_own
        acc = jnp.zeros((rows, blk), F32)
        for c in range(n_chunks):
            pc = jnp.exp2(p_ref[:, c * ch:(c + 1) * ch] - m)
            p_ref[:, c * ch:(c + 1) * ch] = pc
            for b in range(cb):
                acc = acc + pc[:, b * blk:(b + 1) * blk]
        l = jnp.sum(acc, axis=1, keepdims=True) + jnp.sum(p_own, axis=1, keepdims=True)
        l_ref[...] = jnp.broadcast_to(l, l_ref.shape)


def _sample_scores(page_table, qbd, knew_pad, slope_rows, cache_k_pages, t_q, past_len):
    bsz, rows, _ = qbd.shape
    n_pages = page_table.shape[1]
    n_steps = n_pages // PAGES_PER_STEP
    grid_spec = pltpu.PrefetchScalarGridSpec(
        num_scalar_prefetch=1,
        grid=(bsz, n_steps),
        in_specs=[
            pl.BlockSpec((None, rows, D_ATTN), lambda b, g, pt: (b, 0, 0)),
            pl.BlockSpec((None, LANES, D_ATTN), lambda b, g, pt: (b, 0, 0)),
            pl.BlockSpec((rows, 1), lambda b, g, pt: (0, 0)),
            pl.BlockSpec(memory_space=pl.ANY),
        ],
        out_specs=[
            pl.BlockSpec((None, rows, past_len), lambda b, g, pt: (b, 0, 0)),
            pl.BlockSpec((None, rows, LANES), lambda b, g, pt: (b, 0, 0)),
            pl.BlockSpec((None, rows, LANES), lambda b, g, pt: (b, 0, 0)),
        ],
        scratch_shapes=[
            pltpu.VMEM((LANES, D_ATTN), F32),
        ] + _page_ring_scratch(),
    )
    return pl.pallas_call(
        functools.partial(_sample_scores_kernel, t_q=t_q, past_len=past_len),
        grid_spec=grid_spec,
        out_shape=[
            jax.ShapeDtypeStruct((bsz, rows, past_len), F32),
            jax.ShapeDtypeStruct((bsz, rows, LANES), F32),
            jax.ShapeDtypeStruct((bsz, rows, LANES), F32),
        ],
        compiler_params=_cparams(2),
        name="sample_scores",
    )(page_table, qbd, knew_pad, slope_rows, cache_k_pages)


def _sample_pv_accumulate(g, p_ref, pown_ref, vnew_ref, vbuf_ref, slot, acc_ref):
    own = _dot(pown_ref[...].astype(BF16), vnew_ref[...])
    acc = jnp.where(g == 0, own, acc_ref[...])
    for pp in range(PAGES_PER_STEP):
        acc = acc + _dot(p_ref[:, pp * PAGE_SIZE:(pp + 1) * PAGE_SIZE].astype(BF16),
                         _page_bf16(vbuf_ref, slot, pp))
    acc_ref[...] = acc


def _sample_pv_finish(l_ref, o_ref, acc_ref, t_q):
    for h in range(N_HEADS):
        rs = slice(h * t_q, (h + 1) * t_q)
        cs = slice(h * HEAD_DIM, (h + 1) * HEAD_DIM)
        o_ref[:, cs] = (acc_ref[rs, cs] / l_ref[rs, 0:1]).astype(BF16)


def _mix_rows(u_ref, vz_ref, ob_ref, ga_ref, gb_ref, x_ref, ws_ref, bst_ref, wbr_ref, wout_ref,
              nffn_ref, h_ref, hn_ref, oa_ref):
    tm = u_ref.shape[0]
    chunk = ws_ref.shape[1]
    causal = (lax.broadcasted_iota(jnp.int32, (chunk, chunk), 0)
              >= lax.broadcasted_iota(jnp.int32, (chunk, chunk), 1))
    for g in range(GMLP_GROUPS):
        cs = slice(g * GMLP_GROUP_DIM, (g + 1) * GMLP_GROUP_DIM)
        w_g = jnp.where(causal, ws_ref[g], 0.0).astype(BF16)
        b_g = bst_ref[:, g:g + 1]
        for c in range(tm // chunk):
            rs = slice(c * chunk, (c + 1) * chunk)
            mixed = _dot(w_g, vz_ref[rs, cs].astype(BF16)) + b_g
            oa_ref[rs, cs] = (u_ref[rs, cs].astype(F32) * mixed).astype(BF16)
    merged = (ga_ref[...].astype(F32) * _dot(oa_ref[...], wbr_ref[0])
              + gb_ref[...].astype(F32) * _dot(ob_ref[...], wbr_ref[1]))
    h = x_ref[...] + _dot(merged.astype(BF16), wout_ref[...])
    h_ref[...] = h
    hn_ref[...] = _rms(h, nffn_ref[...]).astype(BF16)


N_MIX_ROW_OPERANDS = 6


def _mix_kernel(*refs):
    k = N_MIX_ROW_OPERANDS
    prompt_rows, sample_rows = refs[:k], refs[k:2 * k]
    ws_ref, bst_ref, wss_ref, bsts_ref, wbr_ref, wout_ref, nffn_ref = refs[2 * k:2 * k + 7]
    h_ref, hn_ref, hs_ref, hns_ref, oa_ref, oas_ref = refs[2 * k + 7:]
    _mix_rows(*prompt_rows, ws_ref, bst_ref, wbr_ref, wout_ref, nffn_ref, h_ref, hn_ref, oa_ref)

    @pl.when(pl.program_id(0) == 0)
    def _():
        _mix_rows(*sample_rows, wss_ref, bsts_ref, wbr_ref, wout_ref, nffn_ref, hs_ref, hns_ref,
                  oas_ref)


def _mix(prompt_rows, sample_rows, ws_chunk, bs_t, ws_chunk_s, bs_t_s, w_br_b, w_out_b, norm_ffn, tm):
    m_rows = prompt_rows[-1].shape[0]
    s_rows = sample_rows[-1].shape[0]
    widths = (D_GMLP, D_GMLP, D_ATTN, D_MODEL, D_MODEL, D_MODEL)
    row = lambda m: (m, 0)
    const2 = lambda m: (0, 0)
    const3 = lambda m: (0, 0, 0)
    once = pl.Buffered(1)

    def chunk_specs(ws):
        chunk = ws.shape[1]
        return [pl.BlockSpec((GMLP_GROUPS, chunk, chunk), const3, pipeline_mode=once),
                pl.BlockSpec((chunk, GMLP_GROUPS), const2, pipeline_mode=once)]

    return pl.pallas_call(
        _mix_kernel,
        grid=(m_rows // tm,),
        in_specs=(
            [pl.BlockSpec((tm, w), row) for w in widths]
            + [pl.BlockSpec((s_rows, w), const2) for w in widths]
            + chunk_specs(ws_chunk) + chunk_specs(ws_chunk_s)
            + [pl.BlockSpec((2, D_GMLP, D_MODEL), const3, pipeline_mode=once),
               pl.BlockSpec((D_MODEL, D_MODEL), const2, pipeline_mode=once),
               pl.BlockSpec((1, D_MODEL), const2, pipeline_mode=once)]),
        out_specs=[pl.BlockSpec((tm, D_MODEL), row), pl.BlockSpec((tm, D_MODEL), row),
                   pl.BlockSpec((s_rows, D_MODEL), const2), pl.BlockSpec((s_rows, D_MODEL), const2)],
        out_shape=[jax.ShapeDtypeStruct((m_rows, D_MODEL), F32),
                   jax.ShapeDtypeStruct((m_rows, D_MODEL), BF16),
                   jax.ShapeDtypeStruct((s_rows, D_MODEL), F32),
                   jax.ShapeDtypeStruct((s_rows, D_MODEL), BF16)],
        scratch_shapes=[pltpu.VMEM((tm, D_GMLP), BF16), pltpu.VMEM((s_rows, D_GMLP), BF16)],
        compiler_params=_cparams(1),
        name="mix",
    )(*prompt_rows, *sample_rows, ws_chunk, bs_t, ws_chunk_s, bs_t_s, w_br_b, w_out_b,
      norm_ffn.reshape(1, D_MODEL))


def _ffn_tile(hn, wa, wb, wd, wc, bc, history):
    a = _dot(hn, wa)
    b = _dot(hn, wb)
    a1, a2 = history(a, pltpu.roll(a, 1, 0), pltpu.roll(a, 2, 0))
    conv = bc + a * wc[2:3, :] + a2 * wc[0:1, :] + a1 * wc[1:2, :]
    act = (jax.nn.gelu(conv) * b).astype(BF16)
    return a, _dot(act, wd)


def _ffn_kernel(hn_ref, wa_ref, wb_ref, wc_ref, bc_ref, wd_ref, h_ref, nfin_ref, init_ref,
                hns_ref, hs_ref, inits_ref, y_ref, tail_ref, ys_ref, tails_ref, halo_ref,
                *, tiles_per_seq, sample_seq):
    m = pl.program_id(0)
    n = pl.program_id(1)
    last_n = pl.num_programs(1) - 1
    tm = hn_ref.shape[0]
    tf = wa_ref.shape[1]

    @pl.when((m == 0) & (n == 0))
    def _():
        halo_ref[...] = jnp.zeros_like(halo_ref)

    @pl.when(n == 0)
    def _():
        y_ref[...] = h_ref[...]

    wa = wa_ref[...].astype(BF16)
    wb = wb_ref[...].astype(BF16)
    wd = wd_ref[...].astype(BF16)
    wc = wc_ref[...]
    bc = bc_ref[...]

    def carried_history(a, r1, r2):
        row = lax.broadcasted_iota(jnp.int32, (tm, tf), 0)
        prev = jnp.where(m % tiles_per_seq == 0, init_ref[0], halo_ref[n])
        p1 = prev[SUBLANES - 1:SUBLANES, :]
        p2 = prev[SUBLANES - 2:SUBLANES - 1, :]
        return (jnp.where(row == 0, p1, r1),
                jnp.where(row == 0, p2, jnp.where(row == 1, p1, r2)))

    a, part = _ffn_tile(hn_ref[...], wa, wb, wd, wc, bc, carried_history)
    last = a[tm - SUBLANES:, :]
    halo_ref[n] = last
    tail_ref[...] = last
    y_ref[...] += part

    @pl.when(n == last_n)
    def _():
        y_ref[...] = _rms(y_ref[...], nfin_ref[...])

    @pl.when(m == 0)
    def _():
        @pl.when(n == 0)
        def _():
            ys_ref[...] = hs_ref[...]

        def given_history(a_s, r1, r2):
            pos = lax.broadcasted_iota(jnp.int32, a_s.shape, 0) % sample_seq
            return jnp.where(pos >= 1, r1, inits_ref[0]), jnp.where(pos >= 2, r2, inits_ref[1])

        a_s, part_s = _ffn_tile(hns_ref[...], wa, wb, wd, wc, bc, given_history)
        tails_ref[...] = a_s
        ys_ref[...] += part_s

        @pl.when(n == last_n)
        def _():
            ys_ref[...] = _rms(ys_ref[...], nfin_ref[...])


def _ffn(hn, h, hn_s, h_s, w_up, w_conv, b_conv, w_down, norm_final, init, init_s, tm, seq_len,
         sample_seq):
    m_rows = hn.shape[0]
    s_rows = hn_s.shape[0]
    tf = FF_TILE
    n_ff = D_FF // tf
    tiles_per_seq = seq_len // tm
    row = lambda m, n: (m, 0)
    const = lambda m, n: (0, 0)
    sample_tile = lambda m, n: jnp.where(m == 0, n, n_ff - 1)
    once = pl.Buffered(1)
    return pl.pallas_call(
        functools.partial(_ffn_kernel, tiles_per_seq=tiles_per_seq, sample_seq=sample_seq),
        grid=(m_rows // tm, n_ff),
        in_specs=[
            pl.BlockSpec((tm, D_MODEL), row, pipeline_mode=once),
            pl.BlockSpec((D_MODEL, tf), lambda m, n: (0, n)),
            pl.BlockSpec((D_MODEL, tf), lambda m, n: (0, n_ff + n)),
            pl.BlockSpec((CONV_W, tf), lambda m, n: (0, n)),
            pl.BlockSpec((1, tf), lambda m, n: (0, n)),
            pl.BlockSpec((tf, D_MODEL), lambda m, n: (n, 0)),
            pl.BlockSpec((tm, D_MODEL), row, pipeline_mode=once),
            pl.BlockSpec((1, D_MODEL), const),
            pl.BlockSpec((1, SUBLANES, tf), lambda m, n: (0, m // tiles_per_seq, n)),
            pl.BlockSpec((s_rows, D_MODEL), const),
            pl.BlockSpec((s_rows, D_MODEL), const),
            pl.BlockSpec((2, s_rows, tf), lambda m, n: (0, 0, sample_tile(m, n))),
        ],
        out_specs=[
            pl.BlockSpec((tm, D_MODEL), row, pipeline_mode=once),
            pl.BlockSpec((SUBLANES, tf), lambda m, n: (m, n)),
            pl.BlockSpec((s_rows, D_MODEL), const),
            pl.BlockSpec((s_rows, tf), lambda m, n: (0, sample_tile(m, n))),
        ],
        out_shape=[
            jax.ShapeDtypeStruct((m_rows, D_MODEL), F32),
            jax.ShapeDtypeStruct(((m_rows // tm) * SUBLANES, D_FF), F32),
            jax.ShapeDtypeStruct((s_rows, D_MODEL), F32),
            jax.ShapeDtypeStruct((s_rows, D_FF), F32),
        ],
        scratch_shapes=[pltpu.VMEM((n_ff, SUBLANES, tf), F32)],
        compiler_params=_cparams(2),
        name="ffn",
    )(hn, w_up, w_up, w_conv, b_conv.reshape(1, D_FF), w_down, h,
      norm_final.reshape(1, D_MODEL), init, hn_s, h_s, init_s)


def _alibi_slopes():
    return jnp.exp2(-8.0 * jnp.arange(1, N_HEADS + 1, dtype=F32) / N_HEADS)


def kernel(x_prompt, x_sample, cache_k, cache_v, state_ffn_conv, page_table, norm_mix, w_in,
           norm_gmlp_v, w_spatial, b_spatial, w_branch, w_out, norm_ffn, w_up, w_conv, b_conv,
           w_down, norm_final):
    assert w_in.shape[0] == 1, "single layer"
    bp, seq, _ = x_prompt.shape
    bs, t_q, _ = x_sample.shape
    n_pages = page_table.shape[1]
    past_len = n_pages * PAGE_SIZE
    assert seq % MOBA_BLOCK == 0 and past_len % MOBA_BLOCK == 0 and past_len % CHUNK == 0
    assert t_q <= SUBLANES and past_len // MOBA_BLOCK <= LANES and seq // MOBA_BLOCK <= LANES

    slopes = _alibi_slopes()
    w_br_b = w_branch[0].astype(BF16)
    w_out_b = w_out[0].astype(BF16)
    ws, bsp = w_spatial[0], b_spatial[0]

    m_s = bs * t_q
    xp = x_prompt.reshape(bp * seq, D_MODEL)
    xs = x_sample.reshape(m_s, D_MODEL)
    xn_p = _norm_bf16(xp, norm_mix[0], PROMPT_TM_NORM)
    xn_s = _norm_bf16(xs, norm_mix[0], m_s)
    (u, vz, q, k, kb, v, vb, ga, gb), (us, vzs, qs, ks, kbs, vs, vbs, gas, gbs) = _inproj(
        xn_p, xn_s, w_in[0], norm_gmlp_v[0], PROMPT_TM_INPROJ)

    q4 = qs.reshape(bs, t_q, N_HEADS, HEAD_DIM)
    eye = jnp.eye(N_HEADS, dtype=BF16)
    qbd = (q4.transpose(0, 2, 1, 3)[:, :, :, None, :] * eye[None, :, None, :, None]
           ).reshape(bs, N_HEADS * t_q, D_ATTN)
    pad_rows = ((0, 0), (0, LANES - t_q), (0, 0))
    knew = jnp.pad(kbs.reshape(bs, t_q, D_ATTN), pad_rows)
    vnew = jnp.pad(vbs.reshape(bs, t_q, D_ATTN), pad_rows)
    slope_rows = jnp.repeat(slopes, t_q).reshape(N_HEADS * t_q, 1)
    p, pown, l = _sample_scores(page_table, qbd, knew, slope_rows, cache_k[0], t_q, past_len)

    ob, obs = _moba_prompt(q, kb, vb, slopes, bp, seq, page_table, p, pown, l, vnew, cache_v[0],
                           t_q)
    obs = obs.reshape(m_s, D_ATTN)
    ws_s = (jnp.eye(bs, dtype=F32)[None, :, None, :, None]
            * ws[:, None, :t_q, None, :t_q]).reshape(GMLP_GROUPS, m_s, m_s)
    bs_t_s = jnp.tile(bsp[:, :t_q], (1, bs)).T
    h, hn, hs, hns = _mix((u, vz, ob, ga, gb, xp), (us, vzs, obs, gas, gbs, xs), ws, bsp.T, ws_s,
                          bs_t_s, w_br_b, w_out_b, norm_ffn[0], PROMPT_TM_MIX)
    st = state_ffn_conv[0]
    zero_rows = jnp.zeros((bs, t_q - 1, D_FF), F32)
    init1 = jnp.concatenate([st[:, 1:2], zero_rows], axis=1)
    init2 = jnp.concatenate([st[:, 0:1], st[:, 1:2], zero_rows[:, 1:]], axis=1)
    init_s = jnp.stack([init1.reshape(m_s, D_FF), init2.reshape(m_s, D_FF)])

    zero_state = jnp.zeros((1, bp * SUBLANES, D_FF), F32)
    tm_p = PROMPT_TM_FFN
    yp, tail_p, ys, tail_s = _ffn(hn, h, hns, hs, w_up[0], w_conv[0], b_conv[0], w_down[0],
                                  norm_final, zero_state, init_s, tm_p, seq, t_q)

    y_prompt = yp.reshape(bp, seq, D_MODEL)
    gv_p = vz.reshape(bp, seq, D_GMLP)[:, seq - CHUNK:][None]
    k_p = k.reshape(1, bp, seq, N_HEADS, HEAD_DIM)
    v_p = v.reshape(1, bp, seq, N_HEADS, HEAD_DIM)
    c_p = tail_p.reshape(bp, seq // tm_p, SUBLANES, D_FF)[:, -1, SUBLANES - (CONV_W - 1):][None]
    y_sample = ys.reshape(bs, t_q, D_MODEL)
    gv_s = vzs.reshape(1, bs, t_q, D_GMLP)
    k_s = ks.reshape(1, bs, t_q, N_HEADS, HEAD_DIM)
    v_s = vs.reshape(1, bs, t_q, N_HEADS, HEAD_DIM)
    c_s = tail_s.reshape(bs, t_q, D_FF)[:, t_q - (CONV_W - 1):][None]

    return (y_prompt, y_sample, gv_p, gv_s, k_p, v_p, k_s, v_s, c_p, c_s)
```

```python
import functools

import jax
import jax.numpy as jnp
from jax import lax
from jax.experimental import pallas as pl
from jax.experimental.pallas import tpu as pltpu

F32 = jnp.float32
BF16 = jnp.bfloat16

D_MODEL = 2048
D_GMLP = D_MODEL // 2
GMLP_GROUPS = 8
GMLP_GROUP_DIM = D_GMLP // GMLP_GROUPS
CHUNK = 128
D_ATTN = D_MODEL // 2
HEAD_DIM = 128
N_HEADS = D_ATTN // HEAD_DIM
MOBA_BLOCK = 256
MOBA_TOPK = 3
D_FF = 5632
CONV_W = 3
RMS_EPS = 1e-6
PAGE_SIZE = 128
N_SECTIONS = 9

LANES = 128
SUBLANES = 8
V7X_VMEM_BYTES = 64 * 1024 * 1024
VMEM_LIMIT_BYTES = V7X_VMEM_BYTES * 7 // 8
MOBA_VMEM_LIMIT_BYTES = V7X_VMEM_BYTES * 29 // 32
HOST_VMEM_LIMIT_BYTES = V7X_VMEM_BYTES * 15 // 16

NEG_BIG = -1e30
LOG2E = 1.4426950408889634

MOBA_HEADS_PER_STEP = 4
MOBA_BLOCKS_PER_TRIP = 8
MOBA_ONES_ROWS = 16
SAMPLE_CHUNK_BLOCKS = 8
PAGES_PER_STEP = 16
FF_TILE = 512
PROMPT_TM_NORM = 1024
PROMPT_TM_INPROJ = 512
PROMPT_TM_MIX = 256
PROMPT_TM_FFN = 1024

_NT = (((1,), (1,)), ((), ()))


def _dot(a, b):
    return jnp.dot(a, b, preferred_element_type=F32)


def _dot_nt(a, b):
    return lax.dot_general(a, b, _NT, preferred_element_type=F32)


def _rms(x, g):
    return x * lax.rsqrt(jnp.mean(x * x, axis=-1, keepdims=True) + RMS_EPS) * g


def _sigmoid(x):
    return 0.5 * (jnp.tanh(0.5 * x) + 1.0)


def _cparams(n_axes, vmem_limit_bytes=VMEM_LIMIT_BYTES):
    return pltpu.CompilerParams(dimension_semantics=("arbitrary",) * n_axes,
                                vmem_limit_bytes=vmem_limit_bytes)


def _top3_mask(gate, idx_f, axis=1):
    sel = jnp.zeros_like(gate)
    for _ in range(MOBA_TOPK):
        mx = jnp.max(gate, axis=axis, keepdims=True)
        first = jnp.min(jnp.where(gate == mx, idx_f, float(LANES)), axis=axis, keepdims=True)
        pick = (idx_f == first) & (mx > -jnp.inf)
        sel = jnp.where(pick, 1.0, sel)
        gate = jnp.where(pick, -jnp.inf, gate)
    return sel


def _store_heads(dst_ref, dst_b_ref, acc):
    tm = acc.shape[0]
    dst_b_ref[...] = acc.astype(BF16)
    for h in range(N_HEADS):
        dst_ref[pl.ds(h, tm, stride=N_HEADS), :] = acc[:, h * HEAD_DIM:(h + 1) * HEAD_DIM]


def _norm_kernel(x_ref, g_ref, o_ref):
    o_ref[...] = _rms(x_ref[...], g_ref[...]).astype(BF16)


def _norm_bf16(x, g, tm):
    m_rows, d = x.shape
    return pl.pallas_call(
        _norm_kernel,
        grid=(m_rows // tm,),
        in_specs=[pl.BlockSpec((tm, d), lambda m: (m, 0)), pl.BlockSpec((1, d), lambda m: (0, 0))],
        out_specs=pl.BlockSpec((tm, d), lambda m: (m, 0)),
        out_shape=jax.ShapeDtypeStruct((m_rows, d), BF16),
        compiler_params=_cparams(1),
        name="norm",
    )(x, g.reshape(1, d))


N_INPROJ_OUTS = 9


def _inproj_kernel(xn_ref, xs_ref, w_ref, ngv_ref, *rest):
    prompt = rest[:N_INPROJ_OUTS]
    sample = rest[N_INPROJ_OUTS:2 * N_INPROJ_OUTS]
    wb_ref = rest[2 * N_INPROJ_OUTS]
    n = pl.program_id(0)
    first_tile = pl.program_id(1) == 0

    @pl.when(first_tile)
    def _():
        wb_ref[...] = w_ref[...].astype(BF16)

    def emit(pred, epilogue):
        @pl.when(pred)
        def _():
            epilogue(_dot(xn_ref[...], wb_ref[...]), prompt)

            @pl.when(first_tile)
            def _():
                epilogue(_dot(xs_ref[...], wb_ref[...]), sample)

    def gelu_u(acc, outs):
        outs[0][...] = jax.nn.gelu(acc).astype(BF16)

    def gelu_norm_v(acc, outs):
        outs[1][...] = _rms(jax.nn.gelu(acc), ngv_ref[...])

    def plain_q(acc, outs):
        outs[2][...] = acc.astype(BF16)

    def heads_k(acc, outs):
        _store_heads(outs[3], outs[4], acc)

    def heads_v(acc, outs):
        _store_heads(outs[5], outs[6], acc)

    def gate_a(acc, outs):
        outs[7][...] = _sigmoid(acc).astype(BF16)

    def gate_b(acc, outs):
        outs[8][...] = _sigmoid(acc).astype(BF16)

    emit(n == 0, gelu_u)
    emit(n == 1, gelu_norm_v)
    emit(n == 2, plain_q)
    emit(n == 3, heads_k)
    emit(n == 4, heads_v)
    emit((n == 5) | (n == 6), gate_a)
    emit(n >= 7, gate_b)


def _inproj(xn, xn_s, w_in, norm_gmlp_v, tm):
    m_rows = xn.shape[0]
    s_rows = xn_s.shape[0]
    sec = D_GMLP
    last = m_rows // tm - 1

    def rows(first, count):
        def index(n, m):
            return jnp.where(n < first, 0, jnp.where(n >= first + count, last, m))
        return index

    def col(first, count):
        return lambda n: jnp.clip(n - first, 0, count - 1)

    def out(first, count=1):
        r, c = rows(first, count), col(first, count)
        return pl.BlockSpec((tm, sec), lambda n, m: (r(n, m), c(n)))

    def out_heads(first):
        r = rows(first, 1)
        return pl.BlockSpec((tm * N_HEADS, HEAD_DIM), lambda n, m: (r(n, m), 0))

    def out_s(first, count=1):
        c = col(first, count)
        return pl.BlockSpec((s_rows, sec), lambda n, m: (0, c(n)))

    out_heads_s = pl.BlockSpec((s_rows * N_HEADS, HEAD_DIM), lambda n, m: (0, 0))

    def shapes(r):
        return [
            jax.ShapeDtypeStruct((r, sec), BF16),
            jax.ShapeDtypeStruct((r, sec), F32),
            jax.ShapeDtypeStruct((r, sec), BF16),
            jax.ShapeDtypeStruct((r * N_HEADS, HEAD_DIM), F32),
            jax.ShapeDtypeStruct((r, sec), BF16),
            jax.ShapeDtypeStruct((r * N_HEADS, HEAD_DIM), F32),
            jax.ShapeDtypeStruct((r, sec), BF16),
            jax.ShapeDtypeStruct((r, D_MODEL), BF16),
            jax.ShapeDtypeStruct((r, D_MODEL), BF16),
        ]

    in_specs = [
        pl.BlockSpec((tm, D_MODEL), lambda n, m: (m, 0)),
        pl.BlockSpec((s_rows, D_MODEL), lambda n, m: (0, 0)),
        pl.BlockSpec((D_MODEL, sec), lambda n, m: (0, n)),
        pl.BlockSpec((1, sec), lambda n, m: (0, 0)),
    ]
    out_specs = [out(0), out(1), out(2), out_heads(3), out(3), out_heads(4), out(4), out(5, 2),
                 out(7, 2),
                 out_s(0), out_s(1), out_s(2), out_heads_s, out_s(3), out_heads_s, out_s(4),
                 out_s(5, 2), out_s(7, 2)]
    outs = pl.pallas_call(
        _inproj_kernel,
        grid=(N_SECTIONS, m_rows // tm),
        in_specs=in_specs,
        out_specs=out_specs,
        out_shape=shapes(m_rows) + shapes(s_rows),
        scratch_shapes=[pltpu.VMEM((D_MODEL, sec), BF16)],
        compiler_params=_cparams(2),
        name="inproj",
    )(xn, xn_s, w_in, norm_gmlp_v.reshape(1, sec))
    return outs[:N_INPROJ_OUTS], outs[N_INPROJ_OUTS:]


def _moba_prompt_kernel(pt_ref, slopes_ref, q_ref, kb_ref, vb_ref, qbd_ref, knew_ref, srow_ref,
                        *rest, n_blocks, steps_per_sample, t_q, past_len):
    (cachek_ref, o_ref, p_ref, pown_ref, l_ref, km_ref, vt_ref, bias_ref, sel_ref, m_ref, t_ref,
     acc_ref, skm_ref, kbuf_ref, ksem_ref) = rest
    hg = pl.program_id(1)
    i = pl.program_id(2)
    step = (pl.program_id(0) * pl.num_programs(1) + hg) * n_blocks + i
    total = pl.num_programs(0) * pl.num_programs(1) * n_blocks
    sample_g = step % steps_per_sample
    blk = MOBA_BLOCK

    @pl.when(step == 0)
    def _():
        skm_ref[...] = jnp.zeros_like(skm_ref)

    k_slot = _page_ring_step(pt_ref, cachek_ref, kbuf_ref, ksem_ref, step, total, steps_per_sample)
    c1 = (HEAD_DIM ** -0.5) * LOG2E
    heads = [(hh, slice(hh * HEAD_DIM, (hh + 1) * HEAD_DIM)) for hh in range(MOBA_HEADS_PER_STEP)]

    def slope2(hh):
        return slopes_ref[hg * MOBA_HEADS_PER_STEP + hh] * LOG2E

    @pl.when(i == 0)
    def _():
        key_f = lax.broadcasted_iota(jnp.int32, (blk, blk), 0).astype(F32)
        for hh, cs in heads:
            bias_ref[hh] = slope2(hh) * key_f
            vt_ref[hh, HEAD_DIM:, :] = jnp.ones((MOBA_ONES_ROWS, n_blocks * blk), BF16)
            for j in range(n_blocks):
                rs = slice(j * blk, (j + 1) * blk)
                km_ref[hh, j:j + 1, :] = jnp.sum(kb_ref[rs, cs].astype(F32), axis=0,
                                                 keepdims=True) * (1.0 / blk)
                vt_ref[hh, :HEAD_DIM, rs] = vb_ref[rs, cs].astype(F32).T.astype(BF16)

    blk_i =lax.broadcasted_iota(jnp.int32, (n_blocks, blk), 0)
    causal = (lax.broadcasted_iota(jnp.int32, (blk, blk), 1)
              >= lax.broadcasted_iota(jnp.int32, (blk, blk), 0))
    start = pl.multiple_of(i * blk, blk)

    for hh, cs in heads:
        q = q_ref[:, cs]
        km = km_ref[hh]
        km_hi = km.astype(BF16)
        km_lo = (km - km_hi.astype(F32)).astype(BF16)
        gate = _dot_nt(km_hi, q) + _dot_nt(km_lo, q)
        gate = jnp.where(blk_i < i, gate, -jnp.inf)
        sel_ref[hh] = _top3_mask(gate, blk_i.astype(F32), axis=0)
        t = _dot_nt(kb_ref[pl.ds(start, blk), cs], q) * c1 + bias_ref[hh]
        t = jnp.where(causal, t, NEG_BIG)
        t_ref[hh, i] = t
        m_ref[hh] = jnp.max(t, axis=0, keepdims=True)

    _sample_scores_pages(sample_g, k_slot, qbd_ref, kbuf_ref, p_ref, skm_ref)

    def shift(hh, j):
        return slope2(hh) * lax.convert_element_type((i - j) * blk, F32)

    def pass1(j0, nb):
        for hh, cs in heads:
            m = m_ref[hh]
            for j in [j0 + d for d in range(nb)]:
                off = pl.multiple_of(j * blk, blk)
                t = _dot_nt(kb_ref[pl.ds(off, blk), cs], q_ref[:, cs]) * c1 + bias_ref[hh]
                t_ref[hh, j] = t
                picked = sel_ref[hh, pl.ds(j, 1), :] > 0.0
                m_blk = jnp.max(t, axis=0, keepdims=True) - shift(hh, j)
                m = jnp.where(picked, jnp.maximum(m, m_blk), m)
            m_ref[hh] = m

    def pass2(j0, nb):
        off = pl.multiple_of(j0 * blk, blk)
        for hh, cs in heads:
            ps = []
            for j in [j0 + d for d in range(nb)]:
                picked = sel_ref[hh, pl.ds(j, 1), :] > 0.0
                sub = jnp.where(picked, m_ref[hh] + shift(hh, j), -NEG_BIG)
                ps.append(jnp.exp2(t_ref[hh, j] - sub).astype(BF16))
            p = ps[0] if nb == 1 else jnp.concatenate(ps, axis=0)
            acc_ref[hh] += _dot(vt_ref[hh, :, pl.ds(off, nb * blk)], p)

    def over_past_blocks(fn):
        done = 0
        un = MOBA_BLOCKS_PER_TRIP
        while un >= 1:
            n_trips = (i - done) // un

            def trip(t, carry, un=un, base=done):
                fn(base + t * un, un)
                return carry

            lax.fori_loop(0, n_trips, trip, 0)
            done = done + n_trips * un
            un //= 2

    over_past_blocks(pass1)
    for hh, cs in heads:
        p = jnp.exp2(t_ref[hh, i] - m_ref[hh])
        acc_ref[hh] = _dot(vt_ref[hh, :, pl.ds(start, blk)], p.astype(BF16))
    over_past_blocks(pass2)
    for hh, cs in heads:
        acc = acc_ref[hh]
        o_ref[:, cs] = (acc[:HEAD_DIM] / acc[HEAD_DIM:HEAD_DIM + 1]).T.astype(BF16)

    @pl.when(sample_g == steps_per_sample - 1)
    def _():
        _sample_scores_finalize(qbd_ref, knew_ref, srow_ref, p_ref, pown_ref, l_ref, skm_ref, t_q,
                                past_len)


def _moba_prompt(q, kb, vb, slopes, bsz, seq, page_table, qbd, knew_pad, slope_rows, cache_k, t_q):
    n_blocks = seq // MOBA_BLOCK
    blk = MOBA_BLOCK
    hps = MOBA_HEADS_PER_STEP
    n_hg = N_HEADS // hps
    width = hps * HEAD_DIM
    s_bsz, rows, _ = qbd.shape
    n_pages = page_table.shape[1]
    past_len = n_pages * PAGE_SIZE
    steps_per_sample = (bsz * n_hg * n_blocks) // s_bsz
    assert steps_per_sample * s_bsz == bsz * n_hg * n_blocks
    assert steps_per_sample * PAGES_PER_STEP == n_pages, "every K page is visited exactly once"

    def sample_batch(b, hg, i, pt):
        return ((b * n_hg + hg) * n_blocks + i) // steps_per_sample, 0, 0

    qmap = lambda b, hg, i, pt: (b * n_blocks + i, hg)
    kvmap = lambda b, hg, i, pt: (b, hg)
    once = pl.Buffered(1)
    grid_spec = pltpu.PrefetchScalarGridSpec(
        num_scalar_prefetch=1,
        grid=(bsz, n_hg, n_blocks),
        in_specs=[
            pl.BlockSpec(memory_space=pltpu.SMEM),
            pl.BlockSpec((blk, width), qmap),
            pl.BlockSpec((seq, width), kvmap, pipeline_mode=once),
            pl.BlockSpec((seq, width), kvmap, pipeline_mode=once),
            pl.BlockSpec((None, rows, D_ATTN), sample_batch),
            pl.BlockSpec((None, LANES, D_ATTN), sample_batch),
            pl.BlockSpec((rows, 1), lambda b, hg, i, pt: (0, 0)),
            pl.BlockSpec(memory_space=pl.ANY),
        ],
        out_specs=[
            pl.BlockSpec((blk, width), qmap),
            pl.BlockSpec((None, rows, past_len), sample_batch, pipeline_mode=once),
            pl.BlockSpec((None, rows, LANES), sample_batch),
            pl.BlockSpec((None, rows, LANES), sample_batch),
        ],
        scratch_shapes=[
            pltpu.VMEM((hps, n_blocks, HEAD_DIM), F32),
            pltpu.VMEM((hps, HEAD_DIM + MOBA_ONES_ROWS, seq), BF16),
            pltpu.VMEM((hps, blk, blk), F32),
            pltpu.VMEM((hps, n_blocks, blk), F32),
            pltpu.VMEM((hps, 1, blk), F32),
            pltpu.VMEM((hps, n_blocks, blk, blk), F32),
            pltpu.VMEM((hps, HEAD_DIM + MOBA_ONES_ROWS, blk), F32),
            pltpu.VMEM((LANES, D_ATTN), F32),
        ] + _page_ring_scratch(),
    )
    return pl.pallas_call(
        functools.partial(_moba_prompt_kernel, n_blocks=n_blocks,
                          steps_per_sample=steps_per_sample, t_q=t_q, past_len=past_len),
        grid_spec=grid_spec,
        out_shape=[jax.ShapeDtypeStruct((bsz * seq, D_ATTN), BF16),
                   jax.ShapeDtypeStruct((s_bsz, rows, past_len), F32),
                   jax.ShapeDtypeStruct((s_bsz, rows, LANES), F32),
                   jax.ShapeDtypeStruct((s_bsz, rows, LANES), F32)],
        compiler_params=_cparams(3, MOBA_VMEM_LIMIT_BYTES),
        name="moba_prompt",
    )(page_table, slopes, q, kb, vb, qbd, knew_pad, slope_rows, cache_k)


PAGE_RING_SLOTS = 2


def _page_ring_scratch():
    return [pltpu.VMEM((PAGE_RING_SLOTS, PAGES_PER_STEP, N_HEADS, PAGE_SIZE, HEAD_DIM), F32),
            pltpu.SemaphoreType.DMA((PAGE_RING_SLOTS,))]


def _page_copies(pt_ref, cache_ref, buf_ref, sem_ref, step, n_steps, slot):
    b = step // n_steps
    g = step % n_steps
    copies = []
    for pp in range(PAGES_PER_STEP):
        page = pt_ref[b, g * PAGES_PER_STEP + pp]
        for h in range(N_HEADS):
            copies.append(pltpu.make_async_copy(cache_ref.at[page, :, h, :],
                                                buf_ref.at[slot, pp, h], sem_ref.at[slot]))
    return copies


def _page_ring_start(pt_ref, cache_ref, buf_ref, sem_ref, step, total, n_steps):
    @pl.when(step < total)
    def _():
        for c in _page_copies(pt_ref, cache_ref, buf_ref, sem_ref, step, n_steps,
                              step % PAGE_RING_SLOTS):
            c.start()


def _page_ring_wait(pt_ref, cache_ref, buf_ref, sem_ref, step, n_steps):
    slot = step % PAGE_RING_SLOTS
    for c in _page_copies(pt_ref, cache_ref, buf_ref, sem_ref, step, n_steps, slot):
        c.wait()
    return slot


def _page_ring_step(pt_ref, cache_ref, buf_ref, sem_ref, step, total, n_steps):
    @pl.when(step == 0)
    def _():
        _page_ring_start(pt_ref, cache_ref, buf_ref, sem_ref, step, total, n_steps)

    _page_ring_start(pt_ref, cache_ref, buf_ref, sem_ref, step + 1, total, n_steps)
    return _page_ring_wait(pt_ref, cache_ref, buf_ref, sem_ref, step, n_steps)


def _page_bf16(buf_ref, slot, pp):
    return jnp.concatenate([buf_ref[slot, pp, h].astype(BF16) for h in range(N_HEADS)], axis=1)


def _sample_scores_pages(g, slot, qbd_ref, kbuf_ref, p_ref, km_ref):
    blk = MOBA_BLOCK
    pages_per_block = blk // PAGE_SIZE
    qbd = qbd_ref[...]
    ksum = None
    km_rows = []
    for pp in range(PAGES_PER_STEP):
        heads = [kbuf_ref[slot, pp, h] for h in range(N_HEADS)]
        kpage = jnp.concatenate([s.astype(BF16) for s in heads], axis=1)
        lg = _dot_nt(qbd, kpage)
        off = pl.multiple_of((g * PAGES_PER_STEP + pp) * PAGE_SIZE, PAGE_SIZE)
        p_ref[:, pl.ds(off, PAGE_SIZE)] = lg
        psum = jnp.concatenate(
            [jnp.sum(s.reshape(PAGE_SIZE // SUBLANES, SUBLANES, HEAD_DIM), axis=0) for s in heads],
            axis=1)
        ksum = psum if pp % pages_per_block == 0 else ksum + psum
        if pp % pages_per_block == pages_per_block - 1:
            km_rows.append(jnp.sum(ksum, axis=0, keepdims=True) * (1.0 / blk))
    step_blocks = PAGES_PER_STEP // pages_per_block
    km_ref[pl.ds(pl.multiple_of(g * step_blocks, step_blocks), step_blocks), :] = (
        jnp.concatenate(km_rows, axis=0))


def _sample_scores_finalize(qbd_ref, knew_ref, slope_ref, p_ref, pown_ref, l_ref, km_ref, t_q,
                            past_len):
    rows = qbd_ref.shape[0]
    blk = MOBA_BLOCK
    n_past_blocks = past_len // blk
    qbd = qbd_ref[...]
    lane = lax.broadcasted_iota(jnp.int32, (rows, LANES), 1)
    lane_f = lane.astype(F32)

    def numerators():
        gate = _dot_nt(qbd, km_ref[...].astype(BF16))
        c1 = (HEAD_DIM ** -0.5) * LOG2E
        slope2 = slope_ref[...] * LOG2E
        t_row = (lax.broadcasted_iota(jnp.int32, (rows, 1), 0) % t_q).astype(F32)
        sel = _top3_mask(jnp.where(lane < n_past_blocks, gate, -jnp.inf), lane_f)

        s_own = _dot_nt(qbd, knew_ref[...]) * c1 + slope2 * lane_f
        s_own = jnp.where(lane_f <= t_row, s_own, NEG_BIG)

        cb = SAMPLE_CHUNK_BLOCKS
        ch = cb * blk
        n_chunks = n_past_blocks // cb
        assert cb & (cb - 1) == 0 and blk & (blk - 1) == 0
        widen = jnp.where(
            jnp.bitwise_and(lax.broadcasted_iota(jnp.int32, (LANES, ch), 0), cb - 1)
            == jnp.right_shift(lax.broadcasted_iota(jnp.int32, (LANES, ch), 1), blk.bit_length() - 1),
            1.0, 0.0).astype(BF16)
        key_f = lax.broadcasted_iota(jnp.int32, (1, ch), 1).astype(F32)

        mx = jnp.full((rows, blk), NEG_BIG, F32)
        for c in range(n_chunks):
            sel_c = jnp.where((lane >= c * cb) & (lane < (c + 1) * cb), sel, 0.0).astype(BF16)
            picked = _dot(sel_c, widen)
            s = p_ref[:, c * ch:(c + 1) * ch] * c1 + slope2 * (key_f + float(c * ch - past_len))
            s = jnp.where(picked > 0.0, s, NEG_BIG)
            p_ref[:, c * ch:(c + 1) * ch] = s
            for b in range(cb):
                mx = jnp.maximum(mx, s[:, b * blk:(b + 1) * blk])
        m = jnp.maximum(jnp.max(mx, axis=1, keepdims=True), jnp.max(s_own, axis=1, keepdims=True))

        p_own = jnp.exp2(s_own - m)
        pown_ref[...] = p_own
        acc = jnp.zeros((rows, blk), F32)
        for c in range(n_chunks):
            pc = jnp.exp2(p_ref[:, c * ch:(c + 1) * ch] - m)
            p_ref[:, c * ch:(c + 1) * ch] = pc
            for b in range(cb):
                acc = acc + pc[:, b * blk:(b + 1) * blk]
        l = jnp.sum(acc, axis=1, keepdims=True) + jnp.sum(p_own, axis=1, keepdims=True)
        l_ref[...] = jnp.broadcast_to(l, l_ref.shape)

    numerators()


PV_PAGES_PER_DOT = 2


def _sample_pv_accumulate(g, p_ref, pown_ref, vnew_ref, vbuf_ref, slot, acc_ref, t_q, key0=0):
    for h in range(N_HEADS):
        rs = slice(h * t_q, (h + 1) * t_q)
        own = _dot(pown_ref[rs, :].astype(BF16), vnew_ref[:, h * HEAD_DIM:(h + 1) * HEAD_DIM])
        acc = jnp.where(g == 0, own, acc_ref[rs, :])
        for pp in range(0, PAGES_PER_STEP, PV_PAGES_PER_DOT):
            keys = slice(key0 + pp * PAGE_SIZE, key0 + (pp + PV_PAGES_PER_DOT) * PAGE_SIZE)
            v_h = jnp.concatenate([vbuf_ref[slot, pp + d, h].astype(BF16)
                                   for d in range(PV_PAGES_PER_DOT)], axis=0)
            acc = acc + _dot(p_ref[rs, keys].astype(BF16), v_h)
        acc_ref[rs, :] = acc


def _sample_pv_finish(l_ref, o_ref, acc_ref, t_q):
    for h in range(N_HEADS):
        rs = slice(h * t_q, (h + 1) * t_q)
        o_ref[:, h * HEAD_DIM:(h + 1) * HEAD_DIM] = (
            acc_ref[rs, :] / l_ref[rs, 0:1]).astype(o_ref.dtype)


def _mix_rows(u_ref, vz_ref, ob_ref, ga_ref, gb_ref, x_ref, ws_ref, bst_ref, wbr_ref, wout_ref,
              nffn_ref, h_ref, hn_ref, oa_ref):
    tm = u_ref.shape[0]
    chunk = ws_ref.shape[1]
    causal = (lax.broadcasted_iota(jnp.int32, (chunk, chunk), 0)
              >= lax.broadcasted_iota(jnp.int32, (chunk, chunk), 1))
    for g in range(GMLP_GROUPS):
        cs = slice(g * GMLP_GROUP_DIM, (g + 1) * GMLP_GROUP_DIM)
        w_g = jnp.where(causal, ws_ref[g], 0.0).astype(BF16)
        b_g = bst_ref[:, g:g + 1]
        for c in range(tm // chunk):
            rs = slice(c * chunk, (c + 1) * chunk)
            mixed = _dot(w_g, vz_ref[rs, cs].astype(BF16)) + b_g
            oa_ref[rs, cs] = (u_ref[rs, cs].astype(F32) * mixed).astype(BF16)
    merged = (ga_ref[...].astype(F32) * _dot(oa_ref[...], wbr_ref[0])
              + gb_ref[...].astype(F32) * _dot(ob_ref[...].astype(BF16), wbr_ref[1]))
    h = x_ref[...] + _dot(merged.astype(BF16), wout_ref[...])
    h_ref[...] = h
    hn_ref[...] = _rms(h, nffn_ref[...]).astype(BF16)


N_MIX_ROW_OPERANDS = 6


N_MIX_SAMPLE_OPERANDS = 5


def _mix_kernel(pt_ref, *refs, steps_per_sample, ring_per_step, t_q):
    k, ks = N_MIX_ROW_OPERANDS, N_MIX_SAMPLE_OPERANDS
    prompt_rows, (us_ref, vzs_ref, gas_ref, gbs_ref, xs_ref) = refs[:k], refs[k:k + ks]
    p_ref, pown_ref, l_ref, vnew_ref, cachev_ref = refs[k + ks:k + ks + 5]
    ws_ref, bst_ref, wss_ref, bsts_ref, wbr_ref, wout_ref, nffn_ref = refs[k + ks + 5:k + ks + 12]
    (h_ref, hn_ref, hs_ref, hns_ref, oa_ref, oas_ref, accs_ref, obs_ref, vbuf_ref,
     vsem_ref) = refs[k + ks + 12:]
    m = pl.program_id(0)
    ring = (pt_ref, cachev_ref, vbuf_ref, vsem_ref)
    first = m * ring_per_step
    total = pl.num_programs(0) * ring_per_step
    step_keys = PAGES_PER_STEP * PAGE_SIZE

    @pl.when(m == 0)
    def _():
        accs_ref[...] = jnp.zeros_like(accs_ref)
        _page_ring_start(*ring, first, total, steps_per_sample)

    _page_ring_start(*ring, first + 1, total, steps_per_sample)
    _mix_rows(*prompt_rows, ws_ref, bst_ref, wbr_ref, wout_ref, nffn_ref, h_ref, hn_ref, oa_ref)
    for k in range(ring_per_step):
        sample_g = (first + k) % steps_per_sample
        v_slot = _page_ring_wait(*ring, first + k, steps_per_sample)
        _sample_pv_accumulate(sample_g, p_ref, pown_ref, vnew_ref, vbuf_ref, v_slot, accs_ref, t_q,
                              key0=k * step_keys)

        @pl.when(sample_g == steps_per_sample - 1)
        def _(k=k):
            first_row = pl.multiple_of(((first + k) // steps_per_sample) * t_q, t_q)
            _sample_pv_finish(l_ref, obs_ref.at[pl.ds(first_row, t_q)], accs_ref, t_q)

        if k + 2 <= ring_per_step:
            _page_ring_start(*ring, first + k + 2, total, steps_per_sample)

    @pl.when(m == pl.num_programs(0) - 1)
    def _():
        _mix_rows(us_ref, vzs_ref, obs_ref, gas_ref, gbs_ref, xs_ref, wss_ref, bsts_ref, wbr_ref,
                  wout_ref, nffn_ref, hs_ref, hns_ref, oas_ref)


def _mix(prompt_rows, sample_rows, ws_chunk, bs_t, ws_chunk_s, bs_t_s, w_br_b, w_out_b, norm_ffn, tm,
         page_table, p, pown, l, vnew_pad, cache_v, t_q):
    m_rows = prompt_rows[-1].shape[0]
    s_rows = sample_rows[-1].shape[0]
    s_bsz, rows, _ = p.shape
    n_steps = m_rows // tm
    steps_per_sample = page_table.shape[1] // PAGES_PER_STEP
    ring_per_step = (s_bsz * steps_per_sample) // n_steps
    assert ring_per_step * n_steps == s_bsz * steps_per_sample, "every V page visited once"
    assert steps_per_sample % ring_per_step == 0, "a grid step stays inside one sample batch"
    steps_per_batch = steps_per_sample // ring_per_step
    step_keys = ring_per_step * PAGES_PER_STEP * PAGE_SIZE
    widths = (D_GMLP, D_GMLP, D_ATTN, D_MODEL, D_MODEL, D_MODEL)
    widths_s = (D_GMLP, D_GMLP, D_MODEL, D_MODEL, D_MODEL)
    row = lambda m, pt: (m, 0)
    const2 = lambda m, pt: (0, 0)
    const3 = lambda m, pt: (0, 0, 0)
    sample_batch = lambda m, pt: (m // steps_per_batch, 0, 0)
    once = pl.Buffered(1)

    def chunk_specs(ws):
        chunk = ws.shape[1]
        return [pl.BlockSpec((GMLP_GROUPS, chunk, chunk), const3, pipeline_mode=once),
                pl.BlockSpec((chunk, GMLP_GROUPS), const2, pipeline_mode=once)]

    grid_spec = pltpu.PrefetchScalarGridSpec(
        num_scalar_prefetch=1,
        grid=(n_steps,),
        in_specs=(
            [pl.BlockSpec((tm, w), row) for w in widths]
            + [pl.BlockSpec((s_rows, w), const2) for w in widths_s]
            + [pl.BlockSpec((None, rows, step_keys),
                            lambda m, pt: (m // steps_per_batch, 0, m % steps_per_batch)),
               pl.BlockSpec((None, rows, LANES), sample_batch),
               pl.BlockSpec((None, rows, LANES), sample_batch),
               pl.BlockSpec((None, LANES, D_ATTN), sample_batch),
               pl.BlockSpec(memory_space=pl.ANY)]
            + chunk_specs(ws_chunk) + chunk_specs(ws_chunk_s)
            + [pl.BlockSpec((2, D_GMLP, D_MODEL), const3, pipeline_mode=once),
               pl.BlockSpec((D_MODEL, D_MODEL), const2, pipeline_mode=once),
               pl.BlockSpec((1, D_MODEL), const2, pipeline_mode=once)]),
        out_specs=[pl.BlockSpec((tm, D_MODEL), row), pl.BlockSpec((tm, D_MODEL), row),
                   pl.BlockSpec((s_rows, D_MODEL), const2), pl.BlockSpec((s_rows, D_MODEL), const2)],
        scratch_shapes=[pltpu.VMEM((tm, D_GMLP), BF16), pltpu.VMEM((s_rows, D_GMLP), BF16),
                        pltpu.VMEM((rows, HEAD_DIM), F32),
                        pltpu.VMEM((s_rows, D_ATTN), F32)]
        + _page_ring_scratch(),
    )
    return pl.pallas_call(
        functools.partial(_mix_kernel, steps_per_sample=steps_per_sample,
                          ring_per_step=ring_per_step, t_q=t_q),
        grid_spec=grid_spec,
        out_shape=[jax.ShapeDtypeStruct((m_rows, D_MODEL), F32),
                   jax.ShapeDtypeStruct((m_rows, D_MODEL), BF16),
                   jax.ShapeDtypeStruct((s_rows, D_MODEL), F32),
                   jax.ShapeDtypeStruct((s_rows, D_MODEL), BF16)],
        compiler_params=_cparams(1, HOST_VMEM_LIMIT_BYTES),
        name="mix",
    )(page_table, *prompt_rows, *sample_rows, p, pown, l, vnew_pad, cache_v, ws_chunk, bs_t,
      ws_chunk_s, bs_t_s, w_br_b, w_out_b, norm_ffn.reshape(1, D_MODEL))


def _ffn_tile(hn, wa, wb, wd, wc, bc, history):
    a = _dot(hn, wa)
    b = _dot(hn, wb)
    a1, a2 = history(a, pltpu.roll(a, 1, 0), pltpu.roll(a, 2, 0))
    conv = bc + a * wc[2:3, :] + a2 * wc[0:1, :] + a1 * wc[1:2, :]
    act = (jax.nn.gelu(conv) * b).astype(BF16)
    return a, _dot(act, wd)


def _ffn_kernel(hn_ref, wa_ref, wb_ref, wc_ref, bc_ref, wd_ref, h_ref, nfin_ref, init_ref,
                hns_ref, hs_ref, inits_ref, y_ref, tail_ref, ys_ref, tails_ref, halo_ref,
                *, tiles_per_seq, sample_seq):
    m = pl.program_id(0)
    n = pl.program_id(1)
    last_n = pl.num_programs(1) - 1
    tm = hn_ref.shape[0]
    tf = wa_ref.shape[1]

    @pl.when((m == 0) & (n == 0))
    def _():
        halo_ref[...] = jnp.zeros_like(halo_ref)

    @pl.when(n == 0)
    def _():
        y_ref[...] = h_ref[...]

    wa = wa_ref[...].astype(BF16)
    wb = wb_ref[...].astype(BF16)
    wd = wd_ref[...].astype(BF16)
    wc = wc_ref[...]
    bc = bc_ref[...]

    def carried_history(a, r1, r2):
        row = lax.broadcasted_iota(jnp.int32, (tm, tf), 0)
        prev = jnp.where(m % tiles_per_seq == 0, init_ref[0], halo_ref[n])
        p1 = prev[SUBLANES - 1:SUBLANES, :]
        p2 = prev[SUBLANES - 2:SUBLANES - 1, :]
        return (jnp.where(row == 0, p1, r1),
                jnp.where(row == 0, p2, jnp.where(row == 1, p1, r2)))

    a, part = _ffn_tile(hn_ref[...], wa, wb, wd, wc, bc, carried_history)
    last = a[tm - SUBLANES:, :]
    halo_ref[n] = last
    tail_ref[...] = last
    y_ref[...] += part

    @pl.when(n == last_n)
    def _():
        y_ref[...] = _rms(y_ref[...], nfin_ref[...])

    @pl.when(m == 0)
    def _():
        @pl.when(n == 0)
        def _():
            ys_ref[...] = hs_ref[...]

        def given_history(a_s, r1, r2):
            pos = lax.broadcasted_iota(jnp.int32, a_s.shape, 0) % sample_seq
            return jnp.where(pos >= 1, r1, inits_ref[0]), jnp.where(pos >= 2, r2, inits_ref[1])

        a_s, part_s = _ffn_tile(hns_ref[...], wa, wb, wd, wc, bc, given_history)
        tails_ref[...] = a_s
        ys_ref[...] += part_s

        @pl.when(n == last_n)
        def _():
            ys_ref[...] = _rms(ys_ref[...], nfin_ref[...])


def _ffn(hn, h, hn_s, h_s, w_up, w_conv, b_conv, w_down, norm_final, init, init_s, tm, seq_len,
         sample_seq):
    m_rows = hn.shape[0]
    s_rows = hn_s.shape[0]
    tf = FF_TILE
    n_ff = D_FF // tf
    tiles_per_seq = seq_len // tm
    row = lambda m, n: (m, 0)
    const = lambda m, n: (0, 0)
    sample_tile = lambda m, n: jnp.where(m == 0, n, n_ff - 1)
    once = pl.Buffered(1)
    return pl.pallas_call(
        functools.partial(_ffn_kernel, tiles_per_seq=tiles_per_seq, sample_seq=sample_seq),
        grid=(m_rows // tm, n_ff),
        in_specs=[
            pl.BlockSpec((tm, D_MODEL), row, pipeline_mode=once),
            pl.BlockSpec((D_MODEL, tf), lambda m, n: (0, n)),
            pl.BlockSpec((D_MODEL, tf), lambda m, n: (0, n_ff + n)),
            pl.BlockSpec((CONV_W, tf), lambda m, n: (0, n)),
            pl.BlockSpec((1, tf), lambda m, n: (0, n)),
            pl.BlockSpec((tf, D_MODEL), lambda m, n: (n, 0)),
            pl.BlockSpec((tm, D_MODEL), row, pipeline_mode=once),
            pl.BlockSpec((1, D_MODEL), const),
            pl.BlockSpec((1, SUBLANES, tf), lambda m, n: (0, m // tiles_per_seq, n)),
            pl.BlockSpec((s_rows, D_MODEL), const),
            pl.BlockSpec((s_rows, D_MODEL), const),
            pl.BlockSpec((2, s_rows, tf), lambda m, n: (0, 0, sample_tile(m, n))),
        ],
        out_specs=[
            pl.BlockSpec((tm, D_MODEL), row, pipeline_mode=once),
            pl.BlockSpec((SUBLANES, tf), lambda m, n: (m, n)),
            pl.BlockSpec((s_rows, D_MODEL), const),
            pl.BlockSpec((s_rows, tf), lambda m, n: (0, sample_tile(m, n))),
        ],
        out_shape=[
            jax.ShapeDtypeStruct((m_rows, D_MODEL), F32),
            jax.ShapeDtypeStruct(((m_rows // tm) * SUBLANES, D_FF), F32),
            jax.ShapeDtypeStruct((s_rows, D_MODEL), F32),
            jax.ShapeDtypeStruct((s_rows, D_FF), F32),
        ],
        scratch_shapes=[pltpu.VMEM((n_ff, SUBLANES, tf), F32)],
        compiler_params=_cparams(2),
        name="ffn",
    )(hn, w_up, w_up, w_conv, b_conv.reshape(1, D_FF), w_down, h,
      norm_final.reshape(1, D_MODEL), init, hn_s, h_s, init_s)


def _alibi_slopes():
    return jnp.exp2(-8.0 * jnp.arange(1, N_HEADS + 1, dtype=F32) / N_HEADS)


def kernel(x_prompt, x_sample, cache_k, cache_v, state_ffn_conv, page_table, norm_mix, w_in,
           norm_gmlp_v, w_spatial, b_spatial, w_branch, w_out, norm_ffn, w_up, w_conv, b_conv,
           w_down, norm_final):
    assert w_in.shape[0] == 1, "single layer"
    bp, seq, _ = x_prompt.shape
    bs, t_q, _ = x_sample.shape
    n_pages = page_table.shape[1]
    past_len = n_pages * PAGE_SIZE
    assert seq % MOBA_BLOCK == 0 and past_len % MOBA_BLOCK == 0 and past_len % CHUNK == 0
    assert t_q <= SUBLANES and past_len // MOBA_BLOCK <= LANES and seq // MOBA_BLOCK <= LANES

    slopes = _alibi_slopes()
    w_br_b = w_branch[0].astype(BF16)
    w_out_b = w_out[0].astype(BF16)
    ws, bsp = w_spatial[0], b_spatial[0]

    m_s = bs * t_q
    xp = x_prompt.reshape(bp * seq, D_MODEL)
    xs = x_sample.reshape(m_s, D_MODEL)
    xn_p = _norm_bf16(xp, norm_mix[0], PROMPT_TM_NORM)
    xn_s = _norm_bf16(xs, norm_mix[0], m_s)
    (u, vz, q, k, kb, v, vb, ga, gb), (us, vzs, qs, ks, kbs, vs, vbs, gas, gbs) = _inproj(
        xn_p, xn_s, w_in[0], norm_gmlp_v[0], PROMPT_TM_INPROJ)

    q4 = qs.reshape(bs, t_q, N_HEADS, HEAD_DIM)
    eye = jnp.eye(N_HEADS, dtype=BF16)
    qbd = (q4.transpose(0, 2, 1, 3)[:, :, :, None, :] * eye[None, :, None, :, None]
           ).reshape(bs, N_HEADS * t_q, D_ATTN)
    pad_rows = ((0, 0), (0, LANES - t_q), (0, 0))
    knew = jnp.pad(kbs.reshape(bs, t_q, D_ATTN), pad_rows)
    vnew = jnp.pad(vbs.reshape(bs, t_q, D_ATTN), pad_rows)
    slope_rows = jnp.repeat(slopes, t_q).reshape(N_HEADS * t_q, 1)

    ob, p, pown, l = _moba_prompt(q, kb, vb, slopes, bp, seq, page_table, qbd, knew, slope_rows,
                                  cache_k[0], t_q)

    ws_s = (jnp.eye(bs, dtype=F32)[None, :, None, :, None]
            * ws[:, None, :t_q, None, :t_q]).reshape(GMLP_GROUPS, m_s, m_s)
    bs_t_s = jnp.tile(bsp[:, :t_q], (1, bs)).T
    h, hn, hs, hns = _mix((u, vz, ob, ga, gb, xp), (us, vzs, gas, gbs, xs), ws, bsp.T, ws_s,
                          bs_t_s, w_br_b, w_out_b, norm_ffn[0], PROMPT_TM_MIX,
                          page_table, p, pown, l, vnew, cache_v[0], t_q)
    st = state_ffn_conv[0]
    zero_rows = jnp.zeros((bs, t_q - 1, D_FF), F32)
    init1 = jnp.concatenate([st[:, 1:2], zero_rows], axis=1)
    init2 = jnp.concatenate([st[:, 0:1], st[:, 1:2], zero_rows[:, 1:]], axis=1)
    init_s = jnp.stack([init1.reshape(m_s, D_FF), init2.reshape(m_s, D_FF)])

    zero_state = jnp.zeros((1, bp * SUBLANES, D_FF), F32)
    tm_p = PROMPT_TM_FFN
    yp, tail_p, ys, tail_s = _ffn(hn, h, hns, hs, w_up[0], w_conv[0], b_conv[0], w_down[0],
                                  norm_final, zero_state, init_s, tm_p, seq, t_q)

    y_prompt = yp.reshape(bp, seq, D_MODEL)
    gv_p = vz.reshape(bp, seq, D_GMLP)[:, seq - CHUNK:][None]
    k_p = k.reshape(1, bp, seq, N_HEADS, HEAD_DIM)
    v_p = v.reshape(1, bp, seq, N_HEADS, HEAD_DIM)
    c_p = tail_p.reshape(bp, seq // tm_p, SUBLANES, D_FF)[:, -1, SUBLANES - (CONV_W - 1):][None]
    y_sample = ys.reshape(bs, t_q, D_MODEL)
    gv_s = vzs.reshape(1, bs, t_q, D_GMLP)
    k_s = ks.reshape(1, bs, t_q, N_HEADS, HEAD_DIM)
    v_s = vs.reshape(1, bs, t_q, N_HEADS, HEAD_DIM)
    c_s = tail_s.reshape(bs, t_q, D_FF)[:, t_q - (CONV_W - 1):][None]

    return (y_prompt, y_sample, gv_p, gv_s, k_p, v_p, k_s, v_s, c_p, c_s)
```

```python
import functools

import jax
import jax.numpy as jnp
from jax import lax
from jax.experimental import pallas as pl
from jax.experimental.pallas import tpu as pltpu

F32 = jnp.float32
BF16 = jnp.bfloat16

D_MODEL = 2048
D_GMLP = D_MODEL // 2
GMLP_GROUPS = 8
GMLP_GROUP_DIM = D_GMLP // GMLP_GROUPS
CHUNK = 128
D_ATTN = D_MODEL // 2
HEAD_DIM = 128
N_HEADS = D_ATTN // HEAD_DIM
MOBA_BLOCK = 256
MOBA_TOPK = 3
D_FF = 5632
CONV_W = 3
RMS_EPS = 1e-6
PAGE_SIZE = 128
N_SECTIONS = 9

LANES = 128
SUBLANES = 8
V7X_VMEM_BYTES = 64 * 1024 * 1024
VMEM_LIMIT_BYTES = V7X_VMEM_BYTES * 7 // 8
MOBA_VMEM_LIMIT_BYTES = V7X_VMEM_BYTES * 29 // 32
HOST_VMEM_LIMIT_BYTES = V7X_VMEM_BYTES * 15 // 16

NEG_BIG = -1e30
LOG2E = 1.4426950408889634

MOBA_HEADS_PER_STEP = 4
MOBA_BLOCKS_PER_TRIP = 8
MOBA_ONES_ROWS = 16
SAMPLE_CHUNK_BLOCKS = 8
PAGES_PER_STEP = 16
FF_TILE = 512
PROMPT_TM_NORM = 1024
PROMPT_TM_INPROJ = 512
PROMPT_TM_MIX = 256
PROMPT_TM_FFN = 1024

_NT = (((1,), (1,)), ((), ()))


def _dot(a, b):
    return jnp.dot(a, b, preferred_element_type=F32)


def _dot_nt(a, b):
    return lax.dot_general(a, b, _NT, preferred_element_type=F32)


def _rms(x, g):
    return x * lax.rsqrt(jnp.mean(x * x, axis=-1, keepdims=True) + RMS_EPS) * g


def _sigmoid(x):
    return 0.5 * (jnp.tanh(0.5 * x) + 1.0)


def _cparams(n_axes, vmem_limit_bytes=VMEM_LIMIT_BYTES):
    return pltpu.CompilerParams(dimension_semantics=("arbitrary",) * n_axes,
                                vmem_limit_bytes=vmem_limit_bytes)


def _top3_mask(gate, idx_f, axis=1):
    sel = jnp.zeros_like(gate)
    for _ in range(MOBA_TOPK):
        mx = jnp.max(gate, axis=axis, keepdims=True)
        first = jnp.min(jnp.where(gate == mx, idx_f, float(LANES)), axis=axis, keepdims=True)
        pick = (idx_f == first) & (mx > -jnp.inf)
        sel = jnp.where(pick, 1.0, sel)
        gate = jnp.where(pick, -jnp.inf, gate)
    return sel


def _store_heads(dst_ref, dst_b_ref, acc):
    tm = acc.shape[0]
    dst_b_ref[...] = acc.astype(BF16)
    for h in range(N_HEADS):
        dst_ref[pl.ds(h, tm, stride=N_HEADS), :] = acc[:, h * HEAD_DIM:(h + 1) * HEAD_DIM]


def _norm_kernel(x_ref, g_ref, o_ref):
    o_ref[...] = _rms(x_ref[...], g_ref[...]).astype(BF16)


def _norm_bf16(x, g, tm):
    m_rows, d = x.shape
    return pl.pallas_call(
        _norm_kernel,
        grid=(m_rows // tm,),
        in_specs=[pl.BlockSpec((tm, d), lambda m: (m, 0)), pl.BlockSpec((1, d), lambda m: (0, 0))],
        out_specs=pl.BlockSpec((tm, d), lambda m: (m, 0)),
        out_shape=jax.ShapeDtypeStruct((m_rows, d), BF16),
        compiler_params=_cparams(1),
        name="norm",
    )(x, g.reshape(1, d))


N_INPROJ_OUTS = 9


def _inproj_kernel(xn_ref, xs_ref, w_ref, ngv_ref, *rest):
    prompt = rest[:N_INPROJ_OUTS]
    sample = rest[N_INPROJ_OUTS:2 * N_INPROJ_OUTS]
    wb_ref = rest[2 * N_INPROJ_OUTS]
    n = pl.program_id(0)
    first_tile = pl.program_id(1) == 0

    @pl.when(first_tile)
    def _():
        wb_ref[...] = w_ref[...].astype(BF16)

    def emit(pred, epilogue):
        @pl.when(pred)
        def _():
            epilogue(_dot(xn_ref[...], wb_ref[...]), prompt)

            @pl.when(first_tile)
            def _():
                epilogue(_dot(xs_ref[...], wb_ref[...]), sample)

    def gelu_u(acc, outs):
        outs[0][...] = jax.nn.gelu(acc).astype(BF16)

    def gelu_norm_v(acc, outs):
        outs[1][...] = _rms(jax.nn.gelu(acc), ngv_ref[...])

    def plain_q(acc, outs):
        outs[2][...] = acc.astype(BF16)

    def heads_k(acc, outs):
        _store_heads(outs[3], outs[4], acc)

    def heads_v(acc, outs):
        _store_heads(outs[5], outs[6], acc)

    def gate_a(acc, outs):
        outs[7][...] = _sigmoid(acc).astype(BF16)

    def gate_b(acc, outs):
        outs[8][...] = _sigmoid(acc).astype(BF16)

    emit(n == 0, gelu_u)
    emit(n == 1, gelu_norm_v)
    emit(n == 2, plain_q)
    emit(n == 3, heads_k)
    emit(n == 4, heads_v)
    emit((n == 5) | (n == 6), gate_a)
    emit(n >= 7, gate_b)


def _inproj(xn, xn_s, w_in, norm_gmlp_v, tm):
    m_rows = xn.shape[0]
    s_rows = xn_s.shape[0]
    sec = D_GMLP
    last = m_rows // tm - 1

    def rows(first, count):
        def index(n, m):
            return jnp.where(n < first, 0, jnp.where(n >= first + count, last, m))
        return index

    def col(first, count):
        return lambda n: jnp.clip(n - first, 0, count - 1)

    def out(first, count=1):
        r, c = rows(first, count), col(first, count)
        return pl.BlockSpec((tm, sec), lambda n, m: (r(n, m), c(n)))

    def out_heads(first):
        r = rows(first, 1)
        return pl.BlockSpec((tm * N_HEADS, HEAD_DIM), lambda n, m: (r(n, m), 0))

    def out_s(first, count=1):
        c = col(first, count)
        return pl.BlockSpec((s_rows, sec), lambda n, m: (0, c(n)))

    out_heads_s = pl.BlockSpec((s_rows * N_HEADS, HEAD_DIM), lambda n, m: (0, 0))

    def shapes(r):
        return [
            jax.ShapeDtypeStruct((r, sec), BF16),
            jax.ShapeDtypeStruct((r, sec), F32),
            jax.ShapeDtypeStruct((r, sec), BF16),
            jax.ShapeDtypeStruct((r * N_HEADS, HEAD_DIM), F32),
            jax.ShapeDtypeStruct((r, sec), BF16),
            jax.ShapeDtypeStruct((r * N_HEADS, HEAD_DIM), F32),
            jax.ShapeDtypeStruct((r, sec), BF16),
            jax.ShapeDtypeStruct((r, D_MODEL), BF16),
            jax.ShapeDtypeStruct((r, D_MODEL), BF16),
        ]

    in_specs = [
        pl.BlockSpec((tm, D_MODEL), lambda n, m: (m, 0)),
        pl.BlockSpec((s_rows, D_MODEL), lambda n, m: (0, 0)),
        pl.BlockSpec((D_MODEL, sec), lambda n, m: (0, n)),
        pl.BlockSpec((1, sec), lambda n, m: (0, 0)),
    ]
    out_specs = [out(0), out(1), out(2), out_heads(3), out(3), out_heads(4), out(4), out(5, 2),
                 out(7, 2),
                 out_s(0), out_s(1), out_s(2), out_heads_s, out_s(3), out_heads_s, out_s(4),
                 out_s(5, 2), out_s(7, 2)]
    outs = pl.pallas_call(
        _inproj_kernel,
        grid=(N_SECTIONS, m_rows // tm),
        in_specs=in_specs,
        out_specs=out_specs,
        out_shape=shapes(m_rows) + shapes(s_rows),
        scratch_shapes=[pltpu.VMEM((D_MODEL, sec), BF16)],
        compiler_params=_cparams(2),
        name="inproj",
    )(xn, xn_s, w_in, norm_gmlp_v.reshape(1, sec))
    return outs[:N_INPROJ_OUTS], outs[N_INPROJ_OUTS:]


def _moba_prompt_kernel(pt_ref, slopes_ref, q_ref, kb_ref, vb_ref, qbd_ref, knew_ref, srow_ref,
                        *rest, n_blocks, steps_per_sample, t_q, past_len):
    (cachek_ref, o_ref, p_ref, pown_ref, l_ref, km_ref, vt_ref, bias_ref, sel_ref, m_ref, t_ref,
     acc_ref, skm_ref, kbuf_ref, ksem_ref) = rest
    hg = pl.program_id(1)
    i = pl.program_id(2)
    step = (pl.program_id(0) * pl.num_programs(1) + hg) * n_blocks + i
    total = pl.num_programs(0) * pl.num_programs(1) * n_blocks
    sample_g = step % steps_per_sample
    blk = MOBA_BLOCK

    @pl.when(step == 0)
    def _():
        skm_ref[...] = jnp.zeros_like(skm_ref)

    k_slot = _page_ring_step(pt_ref, cachek_ref, kbuf_ref, ksem_ref, step, total, steps_per_sample)
    c1 = (HEAD_DIM ** -0.5) * LOG2E
    heads = [(hh, slice(hh * HEAD_DIM, (hh + 1) * HEAD_DIM)) for hh in range(MOBA_HEADS_PER_STEP)]

    def slope2(hh):
        return slopes_ref[hg * MOBA_HEADS_PER_STEP + hh] * LOG2E

    @pl.when(i == 0)
    def _():
        key_f = lax.broadcasted_iota(jnp.int32, (blk, blk), 0).astype(F32)
        for hh, cs in heads:
            bias_ref[hh] = slope2(hh) * key_f
            vt_ref[hh, HEAD_DIM:, :] = jnp.ones((MOBA_ONES_ROWS, n_blocks * blk), BF16)
            for j in range(n_blocks):
                rs = slice(j * blk, (j + 1) * blk)
                km_ref[hh, j:j + 1, :] = jnp.sum(kb_ref[rs, cs].astype(F32), axis=0,
                                                 keepdims=True) * (1.0 / blk)
                vt_ref[hh, :HEAD_DIM, rs] = vb_ref[rs, cs].astype(F32).T.astype(BF16)

    blk_i =lax.broadcasted_iota(jnp.int32, (n_blocks, blk), 0)
    causal = (lax.broadcasted_iota(jnp.int32, (blk, blk), 1)
              >= lax.broadcasted_iota(jnp.int32, (blk, blk), 0))
    start = pl.multiple_of(i * blk, blk)

    for hh, cs in heads:
        q = q_ref[:, cs]
        km = km_ref[hh]
        km_hi = km.astype(BF16)
        km_lo = (km - km_hi.astype(F32)).astype(BF16)
        gate = _dot_nt(km_hi, q) + _dot_nt(km_lo, q)
        gate = jnp.where(blk_i < i, gate, -jnp.inf)
        sel_ref[hh] = _top3_mask(gate, blk_i.astype(F32), axis=0)
        t = _dot_nt(kb_ref[pl.ds(start, blk), cs], q) * c1 + bias_ref[hh]
        t = jnp.where(causal, t, NEG_BIG)
        t_ref[hh, i] = t
        m_ref[hh] = jnp.max(t, axis=0, keepdims=True)

    _sample_scores_pages(sample_g, k_slot, qbd_ref, kbuf_ref, p_ref, skm_ref)

    def shift(hh, j):
        return slope2(hh) * lax.convert_element_type((i - j) * blk, F32)

    def pass1(j0, nb):
        for hh, cs in heads:
            m = m_ref[hh]
            for j in [j0 + d for d in range(nb)]:
                off = pl.multiple_of(j * blk, blk)
                t = _dot_nt(kb_ref[pl.ds(off, blk), cs], q_ref[:, cs]) * c1 + bias_ref[hh]
                t_ref[hh, j] = t
                picked = sel_ref[hh, pl.ds(j, 1), :] > 0.0
                m_blk = jnp.max(t, axis=0, keepdims=True) - shift(hh, j)
                m = jnp.where(picked, jnp.maximum(m, m_blk), m)
            m_ref[hh] = m

    def pass2(j0, nb):
        off = pl.multiple_of(j0 * blk, blk)
        for hh, cs in heads:
            ps = []
            for j in [j0 + d for d in range(nb)]:
                picked = sel_ref[hh, pl.ds(j, 1), :] > 0.0
                sub = jnp.where(picked, m_ref[hh] + shift(hh, j), -NEG_BIG)
                ps.append(jnp.exp2(t_ref[hh, j] - sub).astype(BF16))
            p = ps[0] if nb == 1 else jnp.concatenate(ps, axis=0)
            acc_ref[hh] += _dot(vt_ref[hh, :, pl.ds(off, nb * blk)], p)

    def over_past_blocks(fn):
        done = 0
        un = MOBA_BLOCKS_PER_TRIP
        while un >= 1:
            n_trips = (i - done) // un

            def trip(t, carry, un=un, base=done):
                fn(base + t * un, un)
                return carry

            lax.fori_loop(0, n_trips, trip, 0)
            done = done + n_trips * un
            un //= 2

    over_past_blocks(pass1)
    for hh, cs in heads:
        p = jnp.exp2(t_ref[hh, i] - m_ref[hh])
        acc_ref[hh] = _dot(vt_ref[hh, :, pl.ds(start, blk)], p.astype(BF16))
    over_past_blocks(pass2)
    for hh, cs in heads:
        acc = acc_ref[hh]
        o_ref[:, cs] = (acc[:HEAD_DIM] / acc[HEAD_DIM:HEAD_DIM + 1]).T.astype(BF16)

    @pl.when(sample_g == steps_per_sample - 1)
    def _():
        _sample_scores_finalize(qbd_ref, knew_ref, srow_ref, p_ref, pown_ref, l_ref, skm_ref, t_q,
                                past_len)


def _moba_prompt(q, kb, vb, slopes, bsz, seq, page_table, qbd, knew_pad, slope_rows, cache_k, t_q):
    n_blocks = seq // MOBA_BLOCK
    blk = MOBA_BLOCK
    hps = MOBA_HEADS_PER_STEP
    n_hg = N_HEADS // hps
    width = hps * HEAD_DIM
    s_bsz, rows, _ = qbd.shape
    n_pages = page_table.shape[1]
    past_len = n_pages * PAGE_SIZE
    steps_per_sample = (bsz * n_hg * n_blocks) // s_bsz
    assert steps_per_sample * s_bsz == bsz * n_hg * n_blocks
    assert steps_per_sample * PAGES_PER_STEP == n_pages, "every K page is visited exactly once"

    def sample_batch(b, hg, i, pt):
        return ((b * n_hg + hg) * n_blocks + i) // steps_per_sample, 0, 0

    qmap = lambda b, hg, i, pt: (b * n_blocks + i, hg)
    kvmap = lambda b, hg, i, pt: (b, hg)
    once = pl.Buffered(1)
    grid_spec = pltpu.PrefetchScalarGridSpec(
        num_scalar_prefetch=1,
        grid=(bsz, n_hg, n_blocks),
        in_specs=[
            pl.BlockSpec(memory_space=pltpu.SMEM),
            pl.BlockSpec((blk, width), qmap),
            pl.BlockSpec((seq, width), kvmap, pipeline_mode=once),
            pl.BlockSpec((seq, width), kvmap, pipeline_mode=once),
            pl.BlockSpec((None, rows, D_ATTN), sample_batch),
            pl.BlockSpec((None, LANES, D_ATTN), sample_batch),
            pl.BlockSpec((rows, 1), lambda b, hg, i, pt: (0, 0)),
            pl.BlockSpec(memory_space=pl.ANY),
        ],
        out_specs=[
            pl.BlockSpec((blk, width), qmap),
            pl.BlockSpec((None, rows, past_len), sample_batch, pipeline_mode=once),
            pl.BlockSpec((None, rows, LANES), sample_batch),
            pl.BlockSpec((None, rows, LANES), sample_batch),
        ],
        scratch_shapes=[
            pltpu.VMEM((hps, n_blocks, HEAD_DIM), F32),
            pltpu.VMEM((hps, HEAD_DIM + MOBA_ONES_ROWS, seq), BF16),
            pltpu.VMEM((hps, blk, blk), F32),
            pltpu.VMEM((hps, n_blocks, blk), F32),
            pltpu.VMEM((hps, 1, blk), F32),
            pltpu.VMEM((hps, n_blocks, blk, blk), F32),
            pltpu.VMEM((hps, HEAD_DIM + MOBA_ONES_ROWS, blk), F32),
            pltpu.VMEM((LANES, D_ATTN), F32),
        ] + _page_ring_scratch(),
    )
    return pl.pallas_call(
        functools.partial(_moba_prompt_kernel, n_blocks=n_blocks,
                          steps_per_sample=steps_per_sample, t_q=t_q, past_len=past_len),
        grid_spec=grid_spec,
        out_shape=[jax.ShapeDtypeStruct((bsz * seq, D_ATTN), BF16),
                   jax.ShapeDtypeStruct((s_bsz, rows, past_len), F32),
                   jax.ShapeDtypeStruct((s_bsz, rows, LANES), F32),
                   jax.ShapeDtypeStruct((s_bsz, rows, LANES), F32)],
        compiler_params=_cparams(3, MOBA_VMEM_LIMIT_BYTES),
        name="moba_prompt",
    )(page_table, slopes, q, kb, vb, qbd, knew_pad, slope_rows, cache_k)


PAGE_RING_SLOTS = 2


def _page_ring_scratch():
    return [pltpu.VMEM((PAGE_RING_SLOTS, PAGES_PER_STEP, N_HEADS, PAGE_SIZE, HEAD_DIM), F32),
            pltpu.SemaphoreType.DMA((PAGE_RING_SLOTS,))]


def _page_copies(pt_ref, cache_ref, buf_ref, sem_ref, step, n_steps, slot):
    b = step // n_steps
    g = step % n_steps
    copies = []
    for pp in range(PAGES_PER_STEP):
        page = pt_ref[b, g * PAGES_PER_STEP + pp]
        for h in range(N_HEADS):
            copies.append(pltpu.make_async_copy(cache_ref.at[page, :, h, :],
                                                buf_ref.at[slot, pp, h], sem_ref.at[slot]))
    return copies


def _page_ring_start(pt_ref, cache_ref, buf_ref, sem_ref, step, total, n_steps):
    @pl.when(step < total)
    def _():
        for c in _page_copies(pt_ref, cache_ref, buf_ref, sem_ref, step, n_steps,
                              step % PAGE_RING_SLOTS):
            c.start()


def _page_ring_wait(pt_ref, cache_ref, buf_ref, sem_ref, step, n_steps):
    slot = step % PAGE_RING_SLOTS
    for c in _page_copies(pt_ref, cache_ref, buf_ref, sem_ref, step, n_steps, slot):
        c.wait()
    return slot


def _page_ring_step(pt_ref, cache_ref, buf_ref, sem_ref, step, total, n_steps):
    @pl.when(step == 0)
    def _():
        _page_ring_start(pt_ref, cache_ref, buf_ref, sem_ref, step, total, n_steps)

    _page_ring_start(pt_ref, cache_ref, buf_ref, sem_ref, step + 1, total, n_steps)
    return _page_ring_wait(pt_ref, cache_ref, buf_ref, sem_ref, step, n_steps)


def _sample_scores_pages(g, slot, qbd_ref, kbuf_ref, p_ref, km_ref):
    blk = MOBA_BLOCK
    pages_per_block = blk // PAGE_SIZE
    qbd = qbd_ref[...]
    ksum = None
    km_rows = []
    for pp in range(PAGES_PER_STEP):
        heads = [kbuf_ref[slot, pp, h] for h in range(N_HEADS)]
        kpage = jnp.concatenate([s.astype(BF16) for s in heads], axis=1)
        lg = _dot_nt(qbd, kpage)
        off = pl.multiple_of((g * PAGES_PER_STEP + pp) * PAGE_SIZE, PAGE_SIZE)
        p_ref[:, pl.ds(off, PAGE_SIZE)] = lg
        psum = jnp.concatenate(
            [jnp.sum(s.reshape(PAGE_SIZE // SUBLANES, SUBLANES, HEAD_DIM), axis=0) for s in heads],
            axis=1)
        ksum = psum if pp % pages_per_block == 0 else ksum + psum
        if pp % pages_per_block == pages_per_block - 1:
            km_rows.append(jnp.sum(ksum, axis=0, keepdims=True) * (1.0 / blk))
    step_blocks = PAGES_PER_STEP // pages_per_block
    km_ref[pl.ds(pl.multiple_of(g * step_blocks, step_blocks), step_blocks), :] = (
        jnp.concatenate(km_rows, axis=0))


def _sample_scores_finalize(qbd_ref, knew_ref, slope_ref, p_ref, pown_ref, l_ref, km_ref, t_q,
                            past_len):
    rows = qbd_ref.shape[0]
    blk = MOBA_BLOCK
    n_past_blocks = past_len // blk
    qbd = qbd_ref[...]
    lane = lax.broadcasted_iota(jnp.int32, (rows, LANES), 1)
    lane_f = lane.astype(F32)

    def numerators():
        gate = _dot_nt(qbd, km_ref[...].astype(BF16))
        c1 = (HEAD_DIM ** -0.5) * LOG2E
        slope2 = slope_ref[...] * LOG2E
        t_row = (lax.broadcasted_iota(jnp.int32, (rows, 1), 0) % t_q).astype(F32)
        sel = _top3_mask(jnp.where(lane < n_past_blocks, gate, -jnp.inf), lane_f)

        s_own = _dot_nt(qbd, knew_ref[...]) * c1 + slope2 * lane_f
        s_own = jnp.where(lane_f <= t_row, s_own, NEG_BIG)

        cb = SAMPLE_CHUNK_BLOCKS
        ch = cb * blk
        n_chunks = n_past_blocks // cb
        assert cb & (cb - 1) == 0 and blk & (blk - 1) == 0
        widen = jnp.where(
            jnp.bitwise_and(lax.broadcasted_iota(jnp.int32, (LANES, ch), 0), cb - 1)
            == jnp.right_shift(lax.broadcasted_iota(jnp.int32, (LANES, ch), 1), blk.bit_length() - 1),
            1.0, 0.0).astype(BF16)
        key_f = lax.broadcasted_iota(jnp.int32, (1, ch), 1).astype(F32)

        mx = jnp.full((rows, blk), NEG_BIG, F32)
        for c in range(n_chunks):
            sel_c = jnp.where((lane >= c * cb) & (lane < (c + 1) * cb), sel, 0.0).astype(BF16)
            picked = _dot(sel_c, widen)
            s = p_ref[:, c * ch:(c + 1) * ch] * c1 + slope2 * (key_f + float(c * ch - past_len))
            s = jnp.where(picked > 0.0, s, NEG_BIG)
            p_ref[:, c * ch:(c + 1) * ch] = s
            for b in range(cb):
                mx = jnp.maximum(mx, s[:, b * blk:(b + 1) * blk])
        m = jnp.maximum(jnp.max(mx, axis=1, keepdims=True), jnp.max(s_own, axis=1, keepdims=True))

        p_own = jnp.exp2(s_own - m)
        pown_ref[...] = p_own
        acc = jnp.zeros((rows, blk), F32)
        for c in range(n_chunks):
            pc = jnp.exp2(p_ref[:, c * ch:(c + 1) * ch] - m)
            p_ref[:, c * ch:(c + 1) * ch] = pc
            for b in range(cb):
                acc = acc + pc[:, b * blk:(b + 1) * blk]
        l = jnp.sum(acc, axis=1, keepdims=True) + jnp.sum(p_own, axis=1, keepdims=True)
        l_ref[...] = jnp.broadcast_to(l, l_ref.shape)

    numerators()


PV_PAGES_PER_DOT = 2


def _sample_pv_accumulate(g, p_ref, pown_ref, vnew_ref, vbuf_ref, slot, acc_ref, t_q, key0=0):
    for h in range(N_HEADS):
        rs = slice(h * t_q, (h + 1) * t_q)
        own = _dot(pown_ref[rs, :].astype(BF16), vnew_ref[:, h * HEAD_DIM:(h + 1) * HEAD_DIM])
        acc = jnp.where(g == 0, own, acc_ref[rs, :])
        for pp in range(0, PAGES_PER_STEP, PV_PAGES_PER_DOT):
            keys = slice(key0 + pp * PAGE_SIZE, key0 + (pp + PV_PAGES_PER_DOT) * PAGE_SIZE)
            v_h = jnp.concatenate([vbuf_ref[slot, pp + d, h].astype(BF16)
                                   for d in range(PV_PAGES_PER_DOT)], axis=0)
            acc = acc + _dot(p_ref[rs, keys].astype(BF16), v_h)
        acc_ref[rs, :] = acc


def _sample_pv_finish(l_ref, o_ref, acc_ref, t_q):
    for h in range(N_HEADS):
        rs = slice(h * t_q, (h + 1) * t_q)
        o_ref[:, h * HEAD_DIM:(h + 1) * HEAD_DIM] = (
            acc_ref[rs, :] / l_ref[rs, 0:1]).astype(o_ref.dtype)


def _mix_rows(u_ref, vz_ref, ob_ref, ga_ref, gb_ref, x_ref, ws_ref, bst_ref, wbr_ref, wout_ref,
              nffn_ref, h_ref, hn_ref, oa_ref):
    tm = u_ref.shape[0]
    chunk = ws_ref.shape[1]
    causal = (lax.broadcasted_iota(jnp.int32, (chunk, chunk), 0)
              >= lax.broadcasted_iota(jnp.int32, (chunk, chunk), 1))
    for g in range(GMLP_GROUPS):
        cs = slice(g * GMLP_GROUP_DIM, (g + 1) * GMLP_GROUP_DIM)
        w_g = jnp.where(causal, ws_ref[g], 0.0).astype(BF16)
        b_g = bst_ref[:, g:g + 1]
        for c in range(tm // chunk):
            rs = slice(c * chunk, (c + 1) * chunk)
            mixed = _dot(w_g, vz_ref[rs, cs].astype(BF16)) + b_g
            oa_ref[rs, cs] = (u_ref[rs, cs].astype(F32) * mixed).astype(BF16)
    merged = (ga_ref[...].astype(F32) * _dot(oa_ref[...], wbr_ref[0])
              + gb_ref[...].astype(F32) * _dot(ob_ref[...].astype(BF16), wbr_ref[1]))
    h = x_ref[...] + _dot(merged.astype(BF16), wout_ref[...])
    h_ref[...] = h
    hn_ref[...] = _rms(h, nffn_ref[...]).astype(BF16)


N_MIX_ROW_OPERANDS = 6


N_MIX_SAMPLE_OPERANDS = 5


def _mix_kernel(pt_ref, *refs, steps_per_sample, ring_per_step, t_q):
    k, ks = N_MIX_ROW_OPERANDS, N_MIX_SAMPLE_OPERANDS
    prompt_rows, (us_ref, vzs_ref, gas_ref, gbs_ref, xs_ref) = refs[:k], refs[k:k + ks]
    p_ref, pown_ref, l_ref, vnew_ref, cachev_ref = refs[k + ks:k + ks + 5]
    ws_ref, bst_ref, wss_ref, bsts_ref, wbr_ref, wout_ref, nffn_ref = refs[k + ks + 5:k + ks + 12]
    (h_ref, hn_ref, hs_ref, hns_ref, oa_ref, oas_ref, accs_ref, obs_ref, vbuf_ref,
     vsem_ref) = refs[k + ks + 12:]
    m = pl.program_id(0)
    ring = (pt_ref, cachev_ref, vbuf_ref, vsem_ref)
    first = m * ring_per_step
    total = pl.num_programs(0) * ring_per_step
    step_keys = PAGES_PER_STEP * PAGE_SIZE

    @pl.when(m == 0)
    def _():
        accs_ref[...] = jnp.zeros_like(accs_ref)
        _page_ring_start(*ring, first, total, steps_per_sample)

    _page_ring_start(*ring, first + 1, total, steps_per_sample)
    _mix_rows(*prompt_rows, ws_ref, bst_ref, wbr_ref, wout_ref, nffn_ref, h_ref, hn_ref, oa_ref)
    for k in range(ring_per_step):
        sample_g = (first + k) % steps_per_sample
        v_slot = _page_ring_wait(*ring, first + k, steps_per_sample)
        _sample_pv_accumulate(sample_g, p_ref, pown_ref, vnew_ref, vbuf_ref, v_slot, accs_ref, t_q,
                              key0=k * step_keys)

        @pl.when(sample_g == steps_per_sample - 1)
        def _(k=k):
            first_row = pl.multiple_of(((first + k) // steps_per_sample) * t_q, t_q)
            _sample_pv_finish(l_ref, obs_ref.at[pl.ds(first_row, t_q)], accs_ref, t_q)

        if k + 2 <= ring_per_step:
            _page_ring_start(*ring, first + k + 2, total, steps_per_sample)

    @pl.when(m == pl.num_programs(0) - 1)
    def _():
        _mix_rows(us_ref, vzs_ref, obs_ref, gas_ref, gbs_ref, xs_ref, wss_ref, bsts_ref, wbr_ref,
                  wout_ref, nffn_ref, hs_ref, hns_ref, oas_ref)


def _mix(prompt_rows, sample_rows, ws_chunk, bs_t, ws_chunk_s, bs_t_s, w_br_b, w_out_b, norm_ffn, tm,
         page_table, p, pown, l, vnew_pad, cache_v, t_q):
    m_rows = prompt_rows[-1].shape[0]
    s_rows = sample_rows[-1].shape[0]
    s_bsz, rows, _ = p.shape
    n_steps = m_rows // tm
    steps_per_sample = page_table.shape[1] // PAGES_PER_STEP
    ring_per_step = (s_bsz * steps_per_sample) // n_steps
    assert ring_per_step * n_steps == s_bsz * steps_per_sample, "every V page visited once"
    assert steps_per_sample % ring_per_step == 0, "a grid step stays inside one sample batch"
    steps_per_batch = steps_per_sample // ring_per_step
    step_keys = ring_per_step * PAGES_PER_STEP * PAGE_SIZE
    widths = (D_GMLP, D_GMLP, D_ATTN, D_MODEL, D_MODEL, D_MODEL)
    widths_s = (D_GMLP, D_GMLP, D_MODEL, D_MODEL, D_MODEL)
    row = lambda m, pt: (m, 0)
    const2 = lambda m, pt: (0, 0)
    const3 = lambda m, pt: (0, 0, 0)
    sample_batch = lambda m, pt: (m // steps_per_batch, 0, 0)
    once = pl.Buffered(1)

    def chunk_specs(ws):
        chunk = ws.shape[1]
        return [pl.BlockSpec((GMLP_GROUPS, chunk, chunk), const3, pipeline_mode=once),
                pl.BlockSpec((chunk, GMLP_GROUPS), const2, pipeline_mode=once)]

    grid_spec = pltpu.PrefetchScalarGridSpec(
        num_scalar_prefetch=1,
        grid=(n_steps,),
        in_specs=(
            [pl.BlockSpec((tm, w), row) for w in widths]
            + [pl.BlockSpec((s_rows, w), const2) for w in widths_s]
            + [pl.BlockSpec((None, rows, step_keys),
                            lambda m, pt: (m // steps_per_batch, 0, m % steps_per_batch)),
               pl.BlockSpec((None, rows, LANES), sample_batch),
               pl.BlockSpec((None, rows, LANES), sample_batch),
               pl.BlockSpec((None, LANES, D_ATTN), sample_batch),
               pl.BlockSpec(memory_space=pl.ANY)]
            + chunk_specs(ws_chunk) + chunk_specs(ws_chunk_s)
            + [pl.BlockSpec((2, D_GMLP, D_MODEL), const3, pipeline_mode=once),
               pl.BlockSpec((D_MODEL, D_MODEL), const2, pipeline_mode=once),
               pl.BlockSpec((1, D_MODEL), const2, pipeline_mode=once)]),
        out_specs=[pl.BlockSpec((tm, D_MODEL), row), pl.BlockSpec((tm, D_MODEL), row),
                   pl.BlockSpec((s_rows, D_MODEL), const2), pl.BlockSpec((s_rows, D_MODEL), const2)],
        scratch_shapes=[pltpu.VMEM((tm, D_GMLP), BF16), pltpu.VMEM((s_rows, D_GMLP), BF16),
                        pltpu.VMEM((rows, HEAD_DIM), F32),
                        pltpu.VMEM((s_rows, D_ATTN), F32)]
        + _page_ring_scratch(),
    )
    return pl.pallas_call(
        functools.partial(_mix_kernel, steps_per_sample=steps_per_sample,
                          ring_per_step=ring_per_step, t_q=t_q),
        grid_spec=grid_spec,
        out_shape=[jax.ShapeDtypeStruct((m_rows, D_MODEL), F32),
                   jax.ShapeDtypeStruct((m_rows, D_MODEL), BF16),
                   jax.ShapeDtypeStruct((s_rows, D_MODEL), F32),
                   jax.ShapeDtypeStruct((s_rows, D_MODEL), BF16)],
        compiler_params=_cparams(1, HOST_VMEM_LIMIT_BYTES),
        name="mix",
    )(page_table, *prompt_rows, *sample_rows, p, pown, l, vnew_pad, cache_v, ws_chunk, bs_t,
      ws_chunk_s, bs_t_s, w_br_b, w_out_b, norm_ffn.reshape(1, D_MODEL))


def _ffn_tile(hn, wa, wb, wd, wc, bc, history):
    a = _dot(hn, wa)
    b = _dot(hn, wb)
    a1, a2 = history(a, pltpu.roll(a, 1, 0), pltpu.roll(a, 2, 0))
    conv = bc + a * wc[2:3, :] + a2 * wc[0:1, :] + a1 * wc[1:2, :]
    act = (jax.nn.gelu(conv) * b).astype(BF16)
    return a, _dot(act, wd)


def _ffn_kernel(hn_ref, wa_ref, wb_ref, wc_ref, bc_ref, wd_ref, h_ref, nfin_ref, init_ref,
                hns_ref, hs_ref, inits_ref, y_ref, tail_ref, ys_ref, tails_ref, halo_ref,
                *, tiles_per_seq, sample_seq):
    m = pl.program_id(0)
    n = pl.program_id(1)
    last_n = pl.num_programs(1) - 1
    tm = hn_ref.shape[0]
    tf = wa_ref.shape[1]

    @pl.when((m == 0) & (n == 0))
    def _():
        halo_ref[...] = jnp.zeros_like(halo_ref)

    @pl.when(n == 0)
    def _():
        y_ref[...] = h_ref[...]

    wa = wa_ref[...].astype(BF16)
    wb = wb_ref[...].astype(BF16)
    wd = wd_ref[...].astype(BF16)
    wc = wc_ref[...]
    bc = bc_ref[...]

    def carried_history(a, r1, r2):
        row = lax.broadcasted_iota(jnp.int32, (tm, tf), 0)
        prev = jnp.where(m % tiles_per_seq == 0, init_ref[0], halo_ref[n])
        p1 = prev[SUBLANES - 1:SUBLANES, :]
        p2 = prev[SUBLANES - 2:SUBLANES - 1, :]
        return (jnp.where(row == 0, p1, r1),
                jnp.where(row == 0, p2, jnp.where(row == 1, p1, r2)))

    a, part = _ffn_tile(hn_ref[...], wa, wb, wd, wc, bc, carried_history)
    last = a[tm - SUBLANES:, :]
    halo_ref[n] = last
    tail_ref[...] = last
    y_ref[...] += part

    @pl.when(n == last_n)
    def _():
        y_ref[...] = _rms(y_ref[...], nfin_ref[...])

    @pl.when(m == 0)
    def _():
        @pl.when(n == 0)
        def _():
            ys_ref[...] = hs_ref[...]

        def given_history(a_s, r1, r2):
            pos = lax.broadcasted_iota(jnp.int32, a_s.shape, 0) % sample_seq
            return jnp.where(pos >= 1, r1, inits_ref[0]), jnp.where(pos >= 2, r2, inits_ref[1])

        a_s, part_s = _ffn_tile(hns_ref[...], wa, wb, wd, wc, bc, given_history)
        tails_ref[...] = a_s
        ys_ref[...] += part_s

        @pl.when(n == last_n)
        def _():
            ys_ref[...] = _rms(ys_ref[...], nfin_ref[...])


def _ffn(hn, h, hn_s, h_s, w_up, w_conv, b_conv, w_down, norm_final, init, init_s, tm, seq_len,
         sample_seq):
    m_rows = hn.shape[0]
    s_rows = hn_s.shape[0]
    tf = FF_TILE
    n_ff = D_FF // tf
    tiles_per_seq = seq_len // tm
    row = lambda m, n: (m, 0)
    const = lambda m, n: (0, 0)
    sample_tile = lambda m, n: jnp.where(m == 0, n, n_ff - 1)
    once = pl.Buffered(1)
    return pl.pallas_call(
        functools.partial(_ffn_kernel, tiles_per_seq=tiles_per_seq, sample_seq=sample_seq),
        grid=(m_rows // tm, n_ff),
        in_specs=[
            pl.BlockSpec((tm, D_MODEL), row, pipeline_mode=once),
            pl.BlockSpec((D_MODEL, tf), lambda m, n: (0, n)),
            pl.BlockSpec((D_MODEL, tf), lambda m, n: (0, n_ff + n)),
            pl.BlockSpec((CONV_W, tf), lambda m, n: (0, n)),
            pl.BlockSpec((1, tf), lambda m, n: (0, n)),
            pl.BlockSpec((tf, D_MODEL), lambda m, n: (n, 0)),
            pl.BlockSpec((tm, D_MODEL), row, pipeline_mode=once),
            pl.BlockSpec((1, D_MODEL), const),
            pl.BlockSpec((1, SUBLANES, tf), lambda m, n: (0, m // tiles_per_seq, n)),
            pl.BlockSpec((s_rows, D_MODEL), const),
            pl.BlockSpec((s_rows, D_MODEL), const),
            pl.BlockSpec((2, s_rows, tf), lambda m, n: (0, 0, sample_tile(m, n))),
        ],
        out_specs=[
            pl.BlockSpec((tm, D_MODEL), row, pipeline_mode=once),
            pl.BlockSpec((SUBLANES, tf), lambda m, n: (m, n)),
            pl.BlockSpec((s_rows, D_MODEL), const),
            pl.BlockSpec((s_rows, tf), lambda m, n: (0, sample_tile(m, n))),
        ],
        out_shape=[
            jax.ShapeDtypeStruct((m_rows, D_MODEL), F32),
            jax.ShapeDtypeStruct(((m_rows // tm) * SUBLANES, D_FF), F32),
            jax.ShapeDtypeStruct((s_rows, D_MODEL), F32),
            jax.ShapeDtypeStruct((s_rows, D_FF), F32),
        ],
        scratch_shapes=[pltpu.VMEM((n_ff, SUBLANES, tf), F32)],
        compiler_params=_cparams(2),
        name="ffn",
    )(hn, w_up, w_up, w_conv, b_conv.reshape(1, D_FF), w_down, h,
      norm_final.reshape(1, D_MODEL), init, hn_s, h_s, init_s)


def _alibi_slopes():
    return jnp.exp2(-8.0 * jnp.arange(1, N_HEADS + 1, dtype=F32) / N_HEADS)


def kernel(x_prompt, x_sample, cache_k, cache_v, state_ffn_conv, page_table, norm_mix, w_in,
           norm_gmlp_v, w_spatial, b_spatial, w_branch, w_out, norm_ffn, w_up, w_conv, b_conv,
           w_down, norm_final):
    assert w_in.shape[0] == 1, "single layer"
    bp, seq, _ = x_prompt.shape
    bs, t_q, _ = x_sample.shape
    n_pages = page_table.shape[1]
    past_len = n_pages * PAGE_SIZE
    assert seq % MOBA_BLOCK == 0 and past_len % MOBA_BLOCK == 0 and past_len % CHUNK == 0
    assert t_q <= SUBLANES and past_len // MOBA_BLOCK <= LANES and seq // MOBA_BLOCK <= LANES

    slopes = _alibi_slopes()
    w_br_b = w_branch[0].astype(BF16)
    w_out_b = w_out[0].astype(BF16)
    ws, bsp = w_spatial[0], b_spatial[0]

    m_s = bs * t_q
    xp = x_prompt.reshape(bp * seq, D_MODEL)
    xs = x_sample.reshape(m_s, D_MODEL)
    xn_p = _norm_bf16(xp, norm_mix[0], PROMPT_TM_NORM)
    xn_s = _norm_bf16(xs, norm_mix[0], m_s)
    (u, vz, q, k, kb, v, vb, ga, gb), (us, vzs, qs, ks, kbs, vs, vbs, gas, gbs) = _inproj(
        xn_p, xn_s, w_in[0], norm_gmlp_v[0], PROMPT_TM_INPROJ)

    q4 = qs.reshape(bs, t_q, N_HEADS, HEAD_DIM)
    eye = jnp.eye(N_HEADS, dtype=BF16)
    qbd = (q4.transpose(0, 2, 1, 3)[:, :, :, None, :] * eye[None, :, None, :, None]
           ).reshape(bs, N_HEADS * t_q, D_ATTN)
    pad_rows = ((0, 0), (0, LANES - t_q), (0, 0))
    knew = jnp.pad(kbs.reshape(bs, t_q, D_ATTN), pad_rows)
    vnew = jnp.pad(vbs.reshape(bs, t_q, D_ATTN), pad_rows)
    slope_rows = jnp.repeat(slopes, t_q).reshape(N_HEADS * t_q, 1)

    ob, p, pown, l = _moba_prompt(q, kb, vb, slopes, bp, seq, page_table, qbd, knew, slope_rows,
                                  cache_k[0], t_q)

    ws_s = (jnp.eye(bs, dtype=F32)[None, :, None, :, None]
            * ws[:, None, :t_q, None, :t_q]).reshape(GMLP_GROUPS, m_s, m_s)
    bs_t_s = jnp.tile(bsp[:, :t_q], (1, bs)).T
    h, hn, hs, hns = _mix((u, vz, ob, ga, gb, xp), (us, vzs, gas, gbs, xs), ws, bsp.T, ws_s,
                          bs_t_s, w_br_b, w_out_b, norm_ffn[0], PROMPT_TM_MIX,
                          page_table, p, pown, l, vnew, cache_v[0], t_q)
    st = state_ffn_conv[0]
    zero_rows = jnp.zeros((bs, t_q - 1, D_FF), F32)
    init1 = jnp.concatenate([st[:, 1:2], zero_rows], axis=1)
    init2 = jnp.concatenate([st[:, 0:1], st[:, 1:2], zero_rows[:, 1:]], axis=1)
    init_s = jnp.stack([init1.reshape(m_s, D_FF), init2.reshape(m_s, D_FF)])

    zero_state = jnp.zeros((1, bp * SUBLANES, D_FF), F32)
    tm_p = PROMPT_TM_FFN
    yp, tail_p, ys, tail_s = _ffn(hn, h, hns, hs, w_up[0], w_conv[0], b_conv[0], w_down[0],
                                  norm_final, zero_state, init_s, tm_p, seq, t_q)

    y_prompt = yp.reshape(bp, seq, D_MODEL)
    gv_p = vz.reshape(bp, seq, D_GMLP)[:, seq - CHUNK:][None]
    k_p = k.reshape(1, bp, seq, N_HEADS, HEAD_DIM)
    v_p = v.reshape(1, bp, seq, N_HEADS, HEAD_DIM)
    c_p = tail_p.reshape(bp, seq // tm_p, SUBLANES, D_FF)[:, -1, SUBLANES - (CONV_W - 1):][None]
    y_sample = ys.reshape(bs, t_q, D_MODEL)
    gv_s = vzs.reshape(1, bs, t_q, D_GMLP)
    k_s = ks.reshape(1, bs, t_q, N_HEADS, HEAD_DIM)
    v_s = vs.reshape(1, bs, t_q, N_HEADS, HEAD_DIM)
    c_s = tail_s.reshape(bs, t_q, D_FF)[:, t_q - (CONV_W - 1):][None]

    return (y_prompt, y_sample, gv_p, gv_s, k_p, v_p, k_s, v_s, c_p, c_s)
```

```python
import functools

import jax
import jax.numpy as jnp
from jax import lax
from jax.experimental import pallas as pl
from jax.experimental.pallas import tpu as pltpu

F32 = jnp.float32
BF16 = jnp.bfloat16

D_MODEL = 2048
D_GMLP = D_MODEL // 2
GMLP_GROUPS = 8
GMLP_GROUP_DIM = D_GMLP // GMLP_GROUPS
CHUNK = 128
D_ATTN = D_MODEL // 2
HEAD_DIM = 128
N_HEADS = D_ATTN // HEAD_DIM
MOBA_BLOCK = 256
MOBA_TOPK = 3
D_FF = 5632
CONV_W = 3
RMS_EPS = 1e-6
PAGE_SIZE = 128
N_SECTIONS = 9

LANES = 128
SUBLANES = 8
V7X_VMEM_BYTES = 64 * 1024 * 1024
VMEM_LIMIT_BYTES = V7X_VMEM_BYTES * 7 // 8
MOBA_VMEM_LIMIT_BYTES = V7X_VMEM_BYTES * 29 // 32
HOST_VMEM_LIMIT_BYTES = V7X_VMEM_BYTES * 15 // 16

NEG_BIG = -1e30
LOG2E = 1.4426950408889634

MOBA_HEADS_PER_STEP = 4
MOBA_BLOCKS_PER_TRIP = 8
MOBA_ONES_ROWS = 16
SAMPLE_CHUNK_BLOCKS = 8
PAGES_PER_STEP = 16
FF_TILE = 512
PROMPT_TM_NORM = 1024
PROMPT_TM_INPROJ = 512
PROMPT_TM_MIX = 256
PROMPT_TM_FFN = 1024

_NT = (((1,), (1,)), ((), ()))


def _dot(a, b):
    return jnp.dot(a, b, preferred_element_type=F32)


def _dot_nt(a, b):
    return lax.dot_general(a, b, _NT, preferred_element_type=F32)


def _rms(x, g):
    return x * lax.rsqrt(jnp.mean(x * x, axis=-1, keepdims=True) + RMS_EPS) * g


def _sigmoid(x):
    return 0.5 * (jnp.tanh(0.5 * x) + 1.0)


def _cparams(n_axes, vmem_limit_bytes=VMEM_LIMIT_BYTES):
    return pltpu.CompilerParams(dimension_semantics=("arbitrary",) * n_axes,
                                vmem_limit_bytes=vmem_limit_bytes)


def _top3_mask(gate, idx_f, axis=1):
    sel = jnp.zeros_like(gate)
    for _ in range(MOBA_TOPK):
        mx = jnp.max(gate, axis=axis, keepdims=True)
        first = jnp.min(jnp.where(gate == mx, idx_f, float(LANES)), axis=axis, keepdims=True)
        pick = (idx_f == first) & (mx > -jnp.inf)
        sel = jnp.where(pick, 1.0, sel)
        gate = jnp.where(pick, -jnp.inf, gate)
    return sel


def _store_heads(dst_ref, dst_b_ref, acc):
    tm = acc.shape[0]
    dst_b_ref[...] = acc.astype(BF16)
    for h in range(N_HEADS):
        dst_ref[pl.ds(h, tm, stride=N_HEADS), :] = acc[:, h * HEAD_DIM:(h + 1) * HEAD_DIM]


def _norm_kernel(x_ref, g_ref, o_ref):
    o_ref[...] = _rms(x_ref[...], g_ref[...]).astype(BF16)


def _norm_bf16(x, g, tm):
    m_rows, d = x.shape
    return pl.pallas_call(
        _norm_kernel,
        grid=(m_rows // tm,),
        in_specs=[pl.BlockSpec((tm, d), lambda m: (m, 0)), pl.BlockSpec((1, d), lambda m: (0, 0))],
        out_specs=pl.BlockSpec((tm, d), lambda m: (m, 0)),
        out_shape=jax.ShapeDtypeStruct((m_rows, d), BF16),
        compiler_params=_cparams(1),
        name="norm",
    )(x, g.reshape(1, d))


N_INPROJ_OUTS = 9


def _inproj_kernel(xn_ref, xs_ref, w_ref, ngv_ref, *rest):
    prompt = rest[:N_INPROJ_OUTS]
    sample = rest[N_INPROJ_OUTS:2 * N_INPROJ_OUTS]
    wb_ref = rest[2 * N_INPROJ_OUTS]
    n = pl.program_id(0)
    first_tile = pl.program_id(1) == 0

    @pl.when(first_tile)
    def _():
        wb_ref[...] = w_ref[...].astype(BF16)

    def emit(pred, epilogue):
        @pl.when(pred)
        def _():
            epilogue(_dot(xn_ref[...], wb_ref[...]), prompt)

            @pl.when(first_tile)
            def _():
                epilogue(_dot(xs_ref[...], wb_ref[...]), sample)

    def gelu_u(acc, outs):
        outs[0][...] = jax.nn.gelu(acc).astype(BF16)

    def gelu_norm_v(acc, outs):
        outs[1][...] = _rms(jax.nn.gelu(acc), ngv_ref[...])

    def plain_q(acc, outs):
        outs[2][...] = acc.astype(BF16)

    def heads_k(acc, outs):
        _store_heads(outs[3], outs[4], acc)

    def heads_v(acc, outs):
        _store_heads(outs[5], outs[6], acc)

    def gate_a(acc, outs):
        outs[7][...] = _sigmoid(acc).astype(BF16)

    def gate_b(acc, outs):
        outs[8][...] = _sigmoid(acc).astype(BF16)

    emit(n == 0, gelu_u)
    emit(n == 1, gelu_norm_v)
    emit(n == 2, plain_q)
    emit(n == 3, heads_k)
    emit(n == 4, heads_v)
    emit((n == 5) | (n == 6), gate_a)
    emit(n >= 7, gate_b)


def _inproj(xn, xn_s, w_in, norm_gmlp_v, tm):
    m_rows = xn.shape[0]
    s_rows = xn_s.shape[0]
    sec = D_GMLP
    last = m_rows // tm - 1

    def rows(first, count):
        def index(n, m):
            return jnp.where(n < first, 0, jnp.where(n >= first + count, last, m))
        return index

    def col(first, count):
        return lambda n: jnp.clip(n - first, 0, count - 1)

    def out(first, count=1):
        r, c = rows(first, count), col(first, count)
        return pl.BlockSpec((tm, sec), lambda n, m: (r(n, m), c(n)))

    def out_heads(first):
        r = rows(first, 1)
        return pl.BlockSpec((tm * N_HEADS, HEAD_DIM), lambda n, m: (r(n, m), 0))

    def out_s(first, count=1):
        c = col(first, count)
        return pl.BlockSpec((s_rows, sec), lambda n, m: (0, c(n)))

    out_heads_s = pl.BlockSpec((s_rows * N_HEADS, HEAD_DIM), lambda n, m: (0, 0))

    def shapes(r):
        return [
            jax.ShapeDtypeStruct((r, sec), BF16),
            jax.ShapeDtypeStruct((r, sec), F32),
            jax.ShapeDtypeStruct((r, sec), BF16),
            jax.ShapeDtypeStruct((r * N_HEADS, HEAD_DIM), F32),
            jax.ShapeDtypeStruct((r, sec), BF16),
            jax.ShapeDtypeStruct((r * N_HEADS, HEAD_DIM), F32),
            jax.ShapeDtypeStruct((r, sec), BF16),
            jax.ShapeDtypeStruct((r, D_MODEL), BF16),
            jax.ShapeDtypeStruct((r, D_MODEL), BF16),
        ]

    in_specs = [
        pl.BlockSpec((tm, D_MODEL), lambda n, m: (m, 0)),
        pl.BlockSpec((s_rows, D_MODEL), lambda n, m: (0, 0)),
        pl.BlockSpec((D_MODEL, sec), lambda n, m: (0, n)),
        pl.BlockSpec((1, sec), lambda n, m: (0, 0)),
    ]
    out_specs = [out(0), out(1), out(2), out_heads(3), out(3), out_heads(4), out(4), out(5, 2),
                 out(7, 2),
                 out_s(0), out_s(1), out_s(2), out_heads_s, out_s(3), out_heads_s, out_s(4),
                 out_s(5, 2), out_s(7, 2)]
    outs = pl.pallas_call(
        _inproj_kernel,
        grid=(N_SECTIONS, m_rows // tm),
        in_specs=in_specs,
        out_specs=out_specs,
        out_shape=shapes(m_rows) + shapes(s_rows),
        scratch_shapes=[pltpu.VMEM((D_MODEL, sec), BF16)],
        compiler_params=_cparams(2),
        name="inproj",
    )(xn, xn_s, w_in, norm_gmlp_v.reshape(1, sec))
    return outs[:N_INPROJ_OUTS], outs[N_INPROJ_OUTS:]


def _moba_prompt_kernel(pt_ref, slopes_ref, q_ref, kb_ref, vb_ref, qbd_ref, knew_ref, srow_ref,
                        *rest, n_blocks, steps_per_sample, t_q, past_len):
    (cachek_ref, o_ref, p_ref, pown_ref, l_ref, km_ref, vt_ref, bias_ref, sel_ref, m_ref, t_ref,
     acc_ref, skm_ref, kbuf_ref, ksem_ref) = rest
    hg = pl.program_id(1)
    i = pl.program_id(2)
    step = (pl.program_id(0) * pl.num_programs(1) + hg) * n_blocks + i
    total = pl.num_programs(0) * pl.num_programs(1) * n_blocks
    sample_g = step % steps_per_sample
    blk = MOBA_BLOCK

    @pl.when(step == 0)
    def _():
        skm_ref[...] = jnp.zeros_like(skm_ref)

    k_slot = _page_ring_step(pt_ref, cachek_ref, kbuf_ref, ksem_ref, step, total, steps_per_sample)
    c1 = (HEAD_DIM ** -0.5) * LOG2E
    heads = [(hh, slice(hh * HEAD_DIM, (hh + 1) * HEAD_DIM)) for hh in range(MOBA_HEADS_PER_STEP)]

    def slope2(hh):
        return slopes_ref[hg * MOBA_HEADS_PER_STEP + hh] * LOG2E

    @pl.when(i == 0)
    def _():
        key_f = lax.broadcasted_iota(jnp.int32, (blk, blk), 0).astype(F32)
        for hh, cs in heads:
            bias_ref[hh] = slope2(hh) * key_f
            vt_ref[hh, HEAD_DIM:, :] = jnp.ones((MOBA_ONES_ROWS, n_blocks * blk), BF16)
            for j in range(n_blocks):
                rs = slice(j * blk, (j + 1) * blk)
                km_ref[hh, j:j + 1, :] = jnp.sum(kb_ref[rs, cs].astype(F32), axis=0,
                                                 keepdims=True) * (1.0 / blk)
                vt_ref[hh, :HEAD_DIM, rs] = vb_ref[rs, cs].astype(F32).T.astype(BF16)

    blk_i =lax.broadcasted_iota(jnp.int32, (n_blocks, blk), 0)
    causal = (lax.broadcasted_iota(jnp.int32, (blk, blk), 1)
              >= lax.broadcasted_iota(jnp.int32, (blk, blk), 0))
    start = pl.multiple_of(i * blk, blk)

    for hh, cs in heads:
        q = q_ref[:, cs]
        km = km_ref[hh]
        km_hi = km.astype(BF16)
        km_lo = (km - km_hi.astype(F32)).astype(BF16)
        gate = _dot_nt(km_hi, q) + _dot_nt(km_lo, q)
        gate = jnp.where(blk_i < i, gate, -jnp.inf)
        sel_ref[hh] = _top3_mask(gate, blk_i.astype(F32), axis=0)
        t = _dot_nt(kb_ref[pl.ds(start, blk), cs], q) * c1 + bias_ref[hh]
        t = jnp.where(causal, t, NEG_BIG)
        t_ref[hh, i] = t
        m_ref[hh] = jnp.max(t, axis=0, keepdims=True)

    _sample_scores_pages(sample_g, k_slot, qbd_ref, kbuf_ref, p_ref, skm_ref)

    def shift(hh, j):
        return slope2(hh) * lax.convert_element_type((i - j) * blk, F32)

    def pass1(j0, nb):
        for hh, cs in heads:
            m = m_ref[hh]
            for j in [j0 + d for d in range(nb)]:
                off = pl.multiple_of(j * blk, blk)
                t = _dot_nt(kb_ref[pl.ds(off, blk), cs], q_ref[:, cs]) * c1 + bias_ref[hh]
                t_ref[hh, j] = t
                picked = sel_ref[hh, pl.ds(j, 1), :] > 0.0
                m_blk = jnp.max(t, axis=0, keepdims=True) - shift(hh, j)
                m = jnp.where(picked, jnp.maximum(m, m_blk), m)
            m_ref[hh] = m

    def pass2(j0, nb):
        off = pl.multiple_of(j0 * blk, blk)
        for hh, cs in heads:
            ps = []
            for j in [j0 + d for d in range(nb)]:
                picked = sel_ref[hh, pl.ds(j, 1), :] > 0.0
                sub = jnp.where(picked, m_ref[hh] + shift(hh, j), -NEG_BIG)
                ps.append(jnp.exp2(t_ref[hh, j] - sub).astype(BF16))
            p = ps[0] if nb == 1 else jnp.concatenate(ps, axis=0)
            acc_ref[hh] += _dot(vt_ref[hh, :, pl.ds(off, nb * blk)], p)

    def over_past_blocks(fn):
        done = 0
        un = MOBA_BLOCKS_PER_TRIP
        while un >= 1:
            n_trips = (i - done) // un

            def trip(t, carry, un=un, base=done):
                fn(base + t * un, un)
                return carry

            lax.fori_loop(0, n_trips, trip, 0)
            done = done + n_trips * un
            un //= 2

    over_past_blocks(pass1)
    for hh, cs in heads:
        p = jnp.exp2(t_ref[hh, i] - m_ref[hh])
        acc_ref[hh] = _dot(vt_ref[hh, :, pl.ds(start, blk)], p.astype(BF16))
    over_past_blocks(pass2)
    for hh, cs in heads:
        acc = acc_ref[hh]
        o_ref[:, cs] = (acc[:HEAD_DIM] / acc[HEAD_DIM:HEAD_DIM + 1]).T.astype(BF16)

    @pl.when(sample_g == steps_per_sample - 1)
    def _():
        _sample_scores_finalize(qbd_ref, knew_ref, srow_ref, p_ref, pown_ref, l_ref, skm_ref, t_q,
                                past_len)


def _moba_prompt(q, kb, vb, slopes, bsz, seq, page_table, qbd, knew_pad, slope_rows, cache_k, t_q):
    n_blocks = seq // MOBA_BLOCK
    blk = MOBA_BLOCK
    hps = MOBA_HEADS_PER_STEP
    n_hg = N_HEADS // hps
    width = hps * HEAD_DIM
    s_bsz, rows, _ = qbd.shape
    n_pages = page_table.shape[1]
    past_len = n_pages * PAGE_SIZE
    steps_per_sample = (bsz * n_hg * n_blocks) // s_bsz
    assert steps_per_sample * s_bsz == bsz * n_hg * n_blocks
    assert steps_per_sample * PAGES_PER_STEP == n_pages, "every K page is visited exactly once"

    def sample_batch(b, hg, i, pt):
        return ((b * n_hg + hg) * n_blocks + i) // steps_per_sample, 0, 0

    qmap = lambda b, hg, i, pt: (b * n_blocks + i, hg)
    kvmap = lambda b, hg, i, pt: (b, hg)
    once = pl.Buffered(1)
    grid_spec = pltpu.PrefetchScalarGridSpec(
        num_scalar_prefetch=1,
        grid=(bsz, n_hg, n_blocks),
        in_specs=[
            pl.BlockSpec(memory_space=pltpu.SMEM),
            pl.BlockSpec((blk, width), qmap),
            pl.BlockSpec((seq, width), kvmap, pipeline_mode=once),
            pl.BlockSpec((seq, width), kvmap, pipeline_mode=once),
            pl.BlockSpec((None, rows, D_ATTN), sample_batch),
            pl.BlockSpec((None, LANES, D_ATTN), sample_batch),
            pl.BlockSpec((rows, 1), lambda b, hg, i, pt: (0, 0)),
            pl.BlockSpec(memory_space=pl.ANY),
        ],
        out_specs=[
            pl.BlockSpec((blk, width), qmap),
            pl.BlockSpec((None, rows, past_len), sample_batch, pipeline_mode=once),
            pl.BlockSpec((None, rows, LANES), sample_batch),
            pl.BlockSpec((None, rows, LANES), sample_batch),
        ],
        scratch_shapes=[
            pltpu.VMEM((hps, n_blocks, HEAD_DIM), F32),
            pltpu.VMEM((hps, HEAD_DIM + MOBA_ONES_ROWS, seq), BF16),
            pltpu.VMEM((hps, blk, blk), F32),
            pltpu.VMEM((hps, n_blocks, blk), F32),
            pltpu.VMEM((hps, 1, blk), F32),
            pltpu.VMEM((hps, n_blocks, blk, blk), F32),
            pltpu.VMEM((hps, HEAD_DIM + MOBA_ONES_ROWS, blk), F32),
            pltpu.VMEM((LANES, D_ATTN), F32),
        ] + _page_ring_scratch(),
    )
    return pl.pallas_call(
        functools.partial(_moba_prompt_kernel, n_blocks=n_blocks,
                          steps_per_sample=steps_per_sample, t_q=t_q, past_len=past_len),
        grid_spec=grid_spec,
        out_shape=[jax.ShapeDtypeStruct((bsz * seq, D_ATTN), BF16),
                   jax.ShapeDtypeStruct((s_bsz, rows, past_len), F32),
                   jax.ShapeDtypeStruct((s_bsz, rows, LANES), F32),
                   jax.ShapeDtypeStruct((s_bsz, rows, LANES), F32)],
        compiler_params=_cparams(3, MOBA_VMEM_LIMIT_BYTES),
        name="moba_prompt",
    )(page_table, slopes, q, kb, vb, qbd, knew_pad, slope_rows, cache_k)


PAGE_RING_SLOTS = 2


def _page_ring_scratch():
    return [pltpu.VMEM((PAGE_RING_SLOTS, PAGES_PER_STEP, N_HEADS, PAGE_SIZE, HEAD_DIM), F32),
            pltpu.SemaphoreType.DMA((PAGE_RING_SLOTS,))]


def _page_copies(pt_ref, cache_ref, buf_ref, sem_ref, step, n_steps, slot):
    b = step // n_steps
    g = step % n_steps
    copies = []
    for pp in range(PAGES_PER_STEP):
        page = pt_ref[b, g * PAGES_PER_STEP + pp]
        for h in range(N_HEADS):
            copies.append(pltpu.make_async_copy(cache_ref.at[page, :, h, :],
                                                buf_ref.at[slot, pp, h], sem_ref.at[slot]))
    return copies


def _page_ring_start(pt_ref, cache_ref, buf_ref, sem_ref, step, total, n_steps):
    @pl.when(step < total)
    def _():
        copies = _page_copies(pt_ref, cache_ref, buf_ref, sem_ref, step, n_steps,
                              step % PAGE_RING_SLOTS)
        for n, c in enumerate(copies):
            c.start(priority=n % 2)


def _page_ring_wait(pt_ref, cache_ref, buf_ref, sem_ref, step, n_steps):
    slot = step % PAGE_RING_SLOTS
    for c in _page_copies(pt_ref, cache_ref, buf_ref, sem_ref, step, n_steps, slot):
        c.wait()
    return slot


def _page_ring_step(pt_ref, cache_ref, buf_ref, sem_ref, step, total, n_steps):
    @pl.when(step == 0)
    def _():
        _page_ring_start(pt_ref, cache_ref, buf_ref, sem_ref, step, total, n_steps)

    _page_ring_start(pt_ref, cache_ref, buf_ref, sem_ref, step + 1, total, n_steps)
    return _page_ring_wait(pt_ref, cache_ref, buf_ref, sem_ref, step, n_steps)


def _sample_scores_pages(g, slot, qbd_ref, kbuf_ref, p_ref, km_ref):
    blk = MOBA_BLOCK
    pages_per_block = blk // PAGE_SIZE
    qbd = qbd_ref[...]
    ksum = None
    km_rows = []
    for pp in range(PAGES_PER_STEP):
        heads = [kbuf_ref[slot, pp, h] for h in range(N_HEADS)]
        kpage = jnp.concatenate([s.astype(BF16) for s in heads], axis=1)
        lg = _dot_nt(qbd, kpage)
        off = pl.multiple_of((g * PAGES_PER_STEP + pp) * PAGE_SIZE, PAGE_SIZE)
        p_ref[:, pl.ds(off, PAGE_SIZE)] = lg
        psum = jnp.concatenate(
            [jnp.sum(s.reshape(PAGE_SIZE // SUBLANES, SUBLANES, HEAD_DIM), axis=0) for s in heads],
            axis=1)
        ksum = psum if pp % pages_per_block == 0 else ksum + psum
        if pp % pages_per_block == pages_per_block - 1:
            km_rows.append(jnp.sum(ksum, axis=0, keepdims=True) * (1.0 / blk))
    step_blocks = PAGES_PER_STEP // pages_per_block
    km_ref[pl.ds(pl.multiple_of(g * step_blocks, step_blocks), step_blocks), :] = (
        jnp.concatenate(km_rows, axis=0))


def _sample_scores_finalize(qbd_ref, knew_ref, slope_ref, p_ref, pown_ref, l_ref, km_ref, t_q,
                            past_len):
    rows = qbd_ref.shape[0]
    blk = MOBA_BLOCK
    n_past_blocks = past_len // blk
    qbd = qbd_ref[...]
    lane = lax.broadcasted_iota(jnp.int32, (rows, LANES), 1)
    lane_f = lane.astype(F32)

    def numerators():
        gate = _dot_nt(qbd, km_ref[...].astype(BF16))
        c1 = (HEAD_DIM ** -0.5) * LOG2E
        slope2 = slope_ref[...] * LOG2E
        t_row = (lax.broadcasted_iota(jnp.int32, (rows, 1), 0) % t_q).astype(F32)
        sel = _top3_mask(jnp.where(lane < n_past_blocks, gate, -jnp.inf), lane_f)

        s_own = _dot_nt(qbd, knew_ref[...]) * c1 + slope2 * lane_f
        s_own = jnp.where(lane_f <= t_row, s_own, NEG_BIG)

        cb = SAMPLE_CHUNK_BLOCKS
        ch = cb * blk
        n_chunks = n_past_blocks // cb
        assert cb & (cb - 1) == 0 and blk & (blk - 1) == 0
        widen = jnp.where(
            jnp.bitwise_and(lax.broadcasted_iota(jnp.int32, (LANES, ch), 0), cb - 1)
            == jnp.right_shift(lax.broadcasted_iota(jnp.int32, (LANES, ch), 1), blk.bit_length() - 1),
            1.0, 0.0).astype(BF16)
        key_f = lax.broadcasted_iota(jnp.int32, (1, ch), 1).astype(F32)

        mx = jnp.full((rows, blk), NEG_BIG, F32)
        for c in range(n_chunks):
            sel_c = jnp.where((lane >= c * cb) & (lane < (c + 1) * cb), sel, 0.0).astype(BF16)
            picked = _dot(sel_c, widen)
            s = p_ref[:, c * ch:(c + 1) * ch] * c1 + slope2 * (key_f + float(c * ch - past_len))
            s = jnp.where(picked > 0.0, s, NEG_BIG)
            p_ref[:, c * ch:(c + 1) * ch] = s
            for b in range(cb):
                mx = jnp.maximum(mx, s[:, b * blk:(b + 1) * blk])
        m = jnp.maximum(jnp.max(mx, axis=1, keepdims=True), jnp.max(s_own, axis=1, keepdims=True))

        p_own = jnp.exp2(s_own - m)
        pown_ref[...] = p_own
        acc = jnp.zeros((rows, blk), F32)
        for c in range(n_chunks):
            pc = jnp.exp2(p_ref[:, c * ch:(c + 1) * ch] - m)
            p_ref[:, c * ch:(c + 1) * ch] = pc
            for b in range(cb):
                acc = acc + pc[:, b * blk:(b + 1) * blk]
        l = jnp.sum(acc, axis=1, keepdims=True) + jnp.sum(p_own, axis=1, keepdims=True)
        l_ref[...] = jnp.broadcast_to(l, l_ref.shape)

    numerators()


PV_PAGES_PER_DOT = 2


def _sample_pv_accumulate(g, p_ref, pown_ref, vnew_ref, vbuf_ref, slot, acc_ref, t_q, key0=0):
    for h in range(N_HEADS):
        rs = slice(h * t_q, (h + 1) * t_q)
        own = _dot(pown_ref[rs, :].astype(BF16), vnew_ref[:, h * HEAD_DIM:(h + 1) * HEAD_DIM])
        acc = jnp.where(g == 0, own, acc_ref[rs, :])
        for pp in range(0, PAGES_PER_STEP, PV_PAGES_PER_DOT):
            keys = slice(key0 + pp * PAGE_SIZE, key0 + (pp + PV_PAGES_PER_DOT) * PAGE_SIZE)
            v_h = jnp.concatenate([vbuf_ref[slot, pp + d, h].astype(BF16)
                                   for d in range(PV_PAGES_PER_DOT)], axis=0)
            acc = acc + _dot(p_ref[rs, keys].astype(BF16), v_h)
        acc_ref[rs, :] = acc


def _sample_pv_finish(l_ref, o_ref, acc_ref, t_q):
    for h in range(N_HEADS):
        rs = slice(h * t_q, (h + 1) * t_q)
        o_ref[:, h * HEAD_DIM:(h + 1) * HEAD_DIM] = (
            acc_ref[rs, :] / l_ref[rs, 0:1]).astype(o_ref.dtype)


def _mix_rows(u_ref, vz_ref, ob_ref, ga_ref, gb_ref, x_ref, ws_ref, bst_ref, wbr_ref, wout_ref,
              nffn_ref, h_ref, hn_ref, oa_ref):
    tm = u_ref.shape[0]
    chunk = ws_ref.shape[1]
    causal = (lax.broadcasted_iota(jnp.int32, (chunk, chunk), 0)
              >= lax.broadcasted_iota(jnp.int32, (chunk, chunk), 1))
    for g in range(GMLP_GROUPS):
        cs = slice(g * GMLP_GROUP_DIM, (g + 1) * GMLP_GROUP_DIM)
        w_g = jnp.where(causal, ws_ref[g], 0.0).astype(BF16)
        b_g = bst_ref[:, g:g + 1]
        for c in range(tm // chunk):
            rs = slice(c * chunk, (c + 1) * chunk)
            mixed = _dot(w_g, vz_ref[rs, cs].astype(BF16)) + b_g
            oa_ref[rs, cs] = (u_ref[rs, cs].astype(F32) * mixed).astype(BF16)
    merged = (ga_ref[...].astype(F32) * _dot(oa_ref[...], wbr_ref[0])
              + gb_ref[...].astype(F32) * _dot(ob_ref[...].astype(BF16), wbr_ref[1]))
    h = x_ref[...] + _dot(merged.astype(BF16), wout_ref[...])
    h_ref[...] = h
    hn_ref[...] = _rms(h, nffn_ref[...]).astype(BF16)


N_MIX_ROW_OPERANDS = 6


N_MIX_SAMPLE_OPERANDS = 5


def _mix_kernel(pt_ref, *refs, steps_per_sample, ring_per_step, t_q):
    k, ks = N_MIX_ROW_OPERANDS, N_MIX_SAMPLE_OPERANDS
    prompt_rows, (us_ref, vzs_ref, gas_ref, gbs_ref, xs_ref) = refs[:k], refs[k:k + ks]
    p_ref, pown_ref, l_ref, vnew_ref, cachev_ref = refs[k + ks:k + ks + 5]
    ws_ref, bst_ref, wss_ref, bsts_ref, wbr_ref, wout_ref, nffn_ref = refs[k + ks + 5:k + ks + 12]
    (h_ref, hn_ref, hs_ref, hns_ref, oa_ref, oas_ref, accs_ref, obs_ref, vbuf_ref,
     vsem_ref) = refs[k + ks + 12:]
    m = pl.program_id(0)
    ring = (pt_ref, cachev_ref, vbuf_ref, vsem_ref)
    first = m * ring_per_step
    total = pl.num_programs(0) * ring_per_step
    step_keys = PAGES_PER_STEP * PAGE_SIZE

    @pl.when(m == 0)
    def _():
        accs_ref[...] = jnp.zeros_like(accs_ref)
        _page_ring_start(*ring, first, total, steps_per_sample)

    _page_ring_start(*ring, first + 1, total, steps_per_sample)
    _mix_rows(*prompt_rows, ws_ref, bst_ref, wbr_ref, wout_ref, nffn_ref, h_ref, hn_ref, oa_ref)
    for k in range(ring_per_step):
        sample_g = (first + k) % steps_per_sample
        v_slot = _page_ring_wait(*ring, first + k, steps_per_sample)
        _sample_pv_accumulate(sample_g, p_ref, pown_ref, vnew_ref, vbuf_ref, v_slot, accs_ref, t_q,
                              key0=k * step_keys)

        @pl.when(sample_g == steps_per_sample - 1)
        def _(k=k):
            first_row = pl.multiple_of(((first + k) // steps_per_sample) * t_q, t_q)
            _sample_pv_finish(l_ref, obs_ref.at[pl.ds(first_row, t_q)], accs_ref, t_q)

        if k + 2 <= ring_per_step:
            _page_ring_start(*ring, first + k + 2, total, steps_per_sample)

    @pl.when(m == pl.num_programs(0) - 1)
    def _():
        _mix_rows(us_ref, vzs_ref, obs_ref, gas_ref, gbs_ref, xs_ref, wss_ref, bsts_ref, wbr_ref,
                  wout_ref, nffn_ref, hs_ref, hns_ref, oas_ref)


def _mix(prompt_rows, sample_rows, ws_chunk, bs_t, ws_chunk_s, bs_t_s, w_br_b, w_out_b, norm_ffn, tm,
         page_table, p, pown, l, vnew_pad, cache_v, t_q):
    m_rows = prompt_rows[-1].shape[0]
    s_rows = sample_rows[-1].shape[0]
    s_bsz, rows, _ = p.shape
    n_steps = m_rows // tm
    steps_per_sample = page_table.shape[1] // PAGES_PER_STEP
    ring_per_step = (s_bsz * steps_per_sample) // n_steps
    assert ring_per_step * n_steps == s_bsz * steps_per_sample, "every V page visited once"
    assert steps_per_sample % ring_per_step == 0, "a grid step stays inside one sample batch"
    steps_per_batch = steps_per_sample // ring_per_step
    step_keys = ring_per_step * PAGES_PER_STEP * PAGE_SIZE
    widths = (D_GMLP, D_GMLP, D_ATTN, D_MODEL, D_MODEL, D_MODEL)
    widths_s = (D_GMLP, D_GMLP, D_MODEL, D_MODEL, D_MODEL)
    row = lambda m, pt: (m, 0)
    const2 = lambda m, pt: (0, 0)
    const3 = lambda m, pt: (0, 0, 0)
    sample_batch = lambda m, pt: (m // steps_per_batch, 0, 0)
    once = pl.Buffered(1)

    def chunk_specs(ws):
        chunk = ws.shape[1]
        return [pl.BlockSpec((GMLP_GROUPS, chunk, chunk), const3, pipeline_mode=once),
                pl.BlockSpec((chunk, GMLP_GROUPS), const2, pipeline_mode=once)]

    grid_spec = pltpu.PrefetchScalarGridSpec(
        num_scalar_prefetch=1,
        grid=(n_steps,),
        in_specs=(
            [pl.BlockSpec((tm, w), row) for w in widths]
            + [pl.BlockSpec((s_rows, w), const2) for w in widths_s]
            + [pl.BlockSpec((None, rows, step_keys),
                            lambda m, pt: (m // steps_per_batch, 0, m % steps_per_batch)),
               pl.BlockSpec((None, rows, LANES), sample_batch),
               pl.BlockSpec((None, rows, LANES), sample_batch),
               pl.BlockSpec((None, LANES, D_ATTN), sample_batch),
               pl.BlockSpec(memory_space=pl.ANY)]
            + chunk_specs(ws_chunk) + chunk_specs(ws_chunk_s)
            + [pl.BlockSpec((2, D_GMLP, D_MODEL), const3, pipeline_mode=once),
               pl.BlockSpec((D_MODEL, D_MODEL), const2, pipeline_mode=once),
               pl.BlockSpec((1, D_MODEL), const2, pipeline_mode=once)]),
        out_specs=[pl.BlockSpec((tm, D_MODEL), row), pl.BlockSpec((tm, D_MODEL), row),
                   pl.BlockSpec((s_rows, D_MODEL), const2), pl.BlockSpec((s_rows, D_MODEL), const2)],
        scratch_shapes=[pltpu.VMEM((tm, D_GMLP), BF16), pltpu.VMEM((s_rows, D_GMLP), BF16),
                        pltpu.VMEM((rows, HEAD_DIM), F32),
                        pltpu.VMEM((s_rows, D_ATTN), F32)]
        + _page_ring_scratch(),
    )
    return pl.pallas_call(
        functools.partial(_mix_kernel, steps_per_sample=steps_per_sample,
                          ring_per_step=ring_per_step, t_q=t_q),
        grid_spec=grid_spec,
        out_shape=[jax.ShapeDtypeStruct((m_rows, D_MODEL), F32),
                   jax.ShapeDtypeStruct((m_rows, D_MODEL), BF16),
                   jax.ShapeDtypeStruct((s_rows, D_MODEL), F32),
                   jax.ShapeDtypeStruct((s_rows, D_MODEL), BF16)],
        compiler_params=_cparams(1, HOST_VMEM_LIMIT_BYTES),
        name="mix",
    )(page_table, *prompt_rows, *sample_rows, p, pown, l, vnew_pad, cache_v, ws_chunk, bs_t,
      ws_chunk_s, bs_t_s, w_br_b, w_out_b, norm_ffn.reshape(1, D_MODEL))


def _ffn_tile(hn, wa, wb, wd, wc, bc, history):
    a = _dot(hn, wa)
    b = _dot(hn, wb)
    a1, a2 = history(a, pltpu.roll(a, 1, 0), pltpu.roll(a, 2, 0))
    conv = bc + a * wc[2:3, :] + a2 * wc[0:1, :] + a1 * wc[1:2, :]
    act = (jax.nn.gelu(conv) * b).astype(BF16)
    return a, _dot(act, wd)


def _ffn_kernel(hn_ref, wa_ref, wb_ref, wc_ref, bc_ref, wd_ref, h_ref, nfin_ref, init_ref,
                hns_ref, hs_ref, inits_ref, y_ref, tail_ref, ys_ref, tails_ref, halo_ref,
                *, tiles_per_seq, sample_seq):
    m = pl.program_id(0)
    n = pl.program_id(1)
    last_n = pl.num_programs(1) - 1
    tm = hn_ref.shape[0]
    tf = wa_ref.shape[1]

    @pl.when((m == 0) & (n == 0))
    def _():
        halo_ref[...] = jnp.zeros_like(halo_ref)

    @pl.when(n == 0)
    def _():
        y_ref[...] = h_ref[...]

    wa = wa_ref[...].astype(BF16)
    wb = wb_ref[...].astype(BF16)
    wd = wd_ref[...].astype(BF16)
    wc = wc_ref[...]
    bc = bc_ref[...]

    def carried_history(a, r1, r2):
        row = lax.broadcasted_iota(jnp.int32, (tm, tf), 0)
        prev = jnp.where(m % tiles_per_seq == 0, init_ref[0], halo_ref[n])
        p1 = prev[SUBLANES - 1:SUBLANES, :]
        p2 = prev[SUBLANES - 2:SUBLANES - 1, :]
        return (jnp.where(row == 0, p1, r1),
                jnp.where(row == 0, p2, jnp.where(row == 1, p1, r2)))

    a, part = _ffn_tile(hn_ref[...], wa, wb, wd, wc, bc, carried_history)
    last = a[tm - SUBLANES:, :]
    halo_ref[n] = last
    tail_ref[...] = last
    y_ref[...] += part

    @pl.when(n == last_n)
    def _():
        y_ref[...] = _rms(y_ref[...], nfin_ref[...])

    @pl.when(m == 0)
    def _():
        @pl.when(n == 0)
        def _():
            ys_ref[...] = hs_ref[...]

        def given_history(a_s, r1, r2):
            pos = lax.broadcasted_iota(jnp.int32, a_s.shape, 0) % sample_seq
            return jnp.where(pos >= 1, r1, inits_ref[0]), jnp.where(pos >= 2, r2, inits_ref[1])

        a_s, part_s = _ffn_tile(hns_ref[...], wa, wb, wd, wc, bc, given_history)
        tails_ref[...] = a_s
        ys_ref[...] += part_s

        @pl.when(n == last_n)
        def _():
            ys_ref[...] = _rms(ys_ref[...], nfin_ref[...])


def _ffn(hn, h, hn_s, h_s, w_up, w_conv, b_conv, w_down, norm_final, init, init_s, tm, seq_len,
         sample_seq):
    m_rows = hn.shape[0]
    s_rows = hn_s.shape[0]
    tf = FF_TILE
    n_ff = D_FF // tf
    tiles_per_seq = seq_len // tm
    row = lambda m, n: (m, 0)
    const = lambda m, n: (0, 0)
    sample_tile = lambda m, n: jnp.where(m == 0, n, n_ff - 1)
    once = pl.Buffered(1)
    return pl.pallas_call(
        functools.partial(_ffn_kernel, tiles_per_seq=tiles_per_seq, sample_seq=sample_seq),
        grid=(m_rows // tm, n_ff),
        in_specs=[
            pl.BlockSpec((tm, D_MODEL), row, pipeline_mode=once),
            pl.BlockSpec((D_MODEL, tf), lambda m, n: (0, n)),
            pl.BlockSpec((D_MODEL, tf), lambda m, n: (0, n_ff + n)),
            pl.BlockSpec((CONV_W, tf), lambda m, n: (0, n)),
            pl.BlockSpec((1, tf), lambda m, n: (0, n)),
            pl.BlockSpec((tf, D_MODEL), lambda m, n: (n, 0)),
            pl.BlockSpec((tm, D_MODEL), row, pipeline_mode=once),
            pl.BlockSpec((1, D_MODEL), const),
            pl.BlockSpec((1, SUBLANES, tf), lambda m, n: (0, m // tiles_per_seq, n)),
            pl.BlockSpec((s_rows, D_MODEL), const),
            pl.BlockSpec((s_rows, D_MODEL), const),
            pl.BlockSpec((2, s_rows, tf), lambda m, n: (0, 0, sample_tile(m, n))),
        ],
        out_specs=[
            pl.BlockSpec((tm, D_MODEL), row, pipeline_mode=once),
            pl.BlockSpec((SUBLANES, tf), lambda m, n: (m, n)),
            pl.BlockSpec((s_rows, D_MODEL), const),
            pl.BlockSpec((s_rows, tf), lambda m, n: (0, sample_tile(m, n))),
        ],
        out_shape=[
            jax.ShapeDtypeStruct((m_rows, D_MODEL), F32),
            jax.ShapeDtypeStruct(((m_rows // tm) * SUBLANES, D_FF), F32),
            jax.ShapeDtypeStruct((s_rows, D_MODEL), F32),
            jax.ShapeDtypeStruct((s_rows, D_FF), F32),
        ],
        scratch_shapes=[pltpu.VMEM((n_ff, SUBLANES, tf), F32)],
        compiler_params=_cparams(2),
        name="ffn",
    )(hn, w_up, w_up, w_conv, b_conv.reshape(1, D_FF), w_down, h,
      norm_final.reshape(1, D_MODEL), init, hn_s, h_s, init_s)


def _alibi_slopes():
    return jnp.exp2(-8.0 * jnp.arange(1, N_HEADS + 1, dtype=F32) / N_HEADS)


def kernel(x_prompt, x_sample, cache_k, cache_v, state_ffn_conv, page_table, norm_mix, w_in,
           norm_gmlp_v, w_spatial, b_spatial, w_branch, w_out, norm_ffn, w_up, w_conv, b_conv,
           w_down, norm_final):
    assert w_in.shape[0] == 1, "single layer"
    bp, seq, _ = x_prompt.shape
    bs, t_q, _ = x_sample.shape
    n_pages = page_table.shape[1]
    past_len = n_pages * PAGE_SIZE
    assert seq % MOBA_BLOCK == 0 and past_len % MOBA_BLOCK == 0 and past_len % CHUNK == 0
    assert t_q <= SUBLANES and past_len // MOBA_BLOCK <= LANES and seq // MOBA_BLOCK <= LANES

    slopes = _alibi_slopes()
    w_br_b = w_branch[0].astype(BF16)
    w_out_b = w_out[0].astype(BF16)
    ws, bsp = w_spatial[0], b_spatial[0]

    m_s = bs * t_q
    xp = x_prompt.reshape(bp * seq, D_MODEL)
    xs = x_sample.reshape(m_s, D_MODEL)
    xn_p = _norm_bf16(xp, norm_mix[0], PROMPT_TM_NORM)
    xn_s = _norm_bf16(xs, norm_mix[0], m_s)
    (u, vz, q, k, kb, v, vb, ga, gb), (us, vzs, qs, ks, kbs, vs, vbs, gas, gbs) = _inproj(
        xn_p, xn_s, w_in[0], norm_gmlp_v[0], PROMPT_TM_INPROJ)

    q4 = qs.reshape(bs, t_q, N_HEADS, HEAD_DIM)
    eye = jnp.eye(N_HEADS, dtype=BF16)
    qbd = (q4.transpose(0, 2, 1, 3)[:, :, :, None, :] * eye[None, :, None, :, None]
           ).reshape(bs, N_HEADS * t_q, D_ATTN)
    pad_rows = ((0, 0), (0, LANES - t_q), (0, 0))
    knew = jnp.pad(kbs.reshape(bs, t_q, D_ATTN), pad_rows)
    vnew = jnp.pad(vbs.reshape(bs, t_q, D_ATTN), pad_rows)
    slope_rows = jnp.repeat(slopes, t_q).reshape(N_HEADS * t_q, 1)

    ob, p, pown, l = _moba_prompt(q, kb, vb, slopes, bp, seq, page_table, qbd, knew, slope_rows,
                                  cache_k[0], t_q)

    ws_s = (jnp.eye(bs, dtype=F32)[None, :, None, :, None]
            * ws[:, None, :t_q, None, :t_q]).reshape(GMLP_GROUPS, m_s, m_s)
    bs_t_s = jnp.tile(bsp[:, :t_q], (1, bs)).T
    h, hn, hs, hns = _mix((u, vz, ob, ga, gb, xp), (us, vzs, gas, gbs, xs), ws, bsp.T, ws_s,
                          bs_t_s, w_br_b, w_out_b, norm_ffn[0], PROMPT_TM_MIX,
                          page_table, p, pown, l, vnew, cache_v[0], t_q)
    st = state_ffn_conv[0]
    zero_rows = jnp.zeros((bs, t_q - 1, D_FF), F32)
    init1 = jnp.concatenate([st[:, 1:2], zero_rows], axis=1)
    init2 = jnp.concatenate([st[:, 0:1], st[:, 1:2], zero_rows[:, 1:]], axis=1)
    init_s = jnp.stack([init1.reshape(m_s, D_FF), init2.reshape(m_s, D_FF)])

    zero_state = jnp.zeros((1, bp * SUBLANES, D_FF), F32)
    tm_p = PROMPT_TM_FFN
    yp, tail_p, ys, tail_s = _ffn(hn, h, hns, hs, w_up[0], w_conv[0], b_conv[0], w_down[0],
                                  norm_final, zero_state, init_s, tm_p, seq, t_q)

    y_prompt = yp.reshape(bp, seq, D_MODEL)
    gv_p = vz.reshape(bp, seq, D_GMLP)[:, seq - CHUNK:][None]
    k_p = k.reshape(1, bp, seq, N_HEADS, HEAD_DIM)
    v_p = v.reshape(1, bp, seq, N_HEADS, HEAD_DIM)
    c_p = tail_p.reshape(bp, seq // tm_p, SUBLANES, D_FF)[:, -1, SUBLANES - (CONV_W - 1):][None]
    y_sample = ys.reshape(bs, t_q, D_MODEL)
    gv_s = vzs.reshape(1, bs, t_q, D_GMLP)
    k_s = ks.reshape(1, bs, t_q, N_HEADS, HEAD_DIM)
    v_s = vs.reshape(1, bs, t_q, N_HEADS, HEAD_DIM)
    c_s = tail_s.reshape(bs, t_q, D_FF)[:, t_q - (CONV_W - 1):][None]

    return (y_prompt, y_sample, gv_p, gv_s, k_p, v_p, k_s, v_s, c_p, c_s)
```
